```python
import jax, jax.numpy as jnp
from jax import lax
import numpy as np

D_MODEL = 1024
BATCH = 8
SEQ = 2048
DEPTH = 4

CHUNK = 64
SPATIAL_BLOCK = 128
MIX_WIDTH = 2 * D_MODEL
GMLP_WIDTH = MIX_WIDTH // 2
GMLP_GROUPS = 8
GMLP_GROUP_DIM = GMLP_WIDTH // GMLP_GROUPS
GDN_HEAD_DIM = 128
GDN_WIDTH = MIX_WIDTH - GMLP_WIDTH
GDN_HEADS = GDN_WIDTH // GDN_HEAD_DIM
CONV_K = 4
PLE_DIM = 256
EPS = 1e-6

OFF_UA = 0
OFF_VA = OFF_UA + GMLP_WIDTH
OFF_ZA = OFF_VA + GMLP_WIDTH
OFF_QKV = OFF_ZA + GMLP_WIDTH
OFF_ZB = OFF_QKV + 3 * GDN_WIDTH
OFF_A = OFF_ZB + GDN_WIDTH
OFF_B = OFF_A + GDN_HEADS
IN_DIM = OFF_B + GDN_HEADS

kernel_name = "hybrid_gmlp_gdn_streaming_trunk"


def rmsnorm(x, w):
    xf = x.astype(jnp.float32)
    y = xf * lax.rsqrt(jnp.mean(xf * xf, axis=-1, keepdims=True) + EPS)
    return (y * w.astype(jnp.float32)).astype(x.dtype)


def layernorm(x, g, b):
    xf = x.astype(jnp.float32)
    mu = jnp.mean(xf, axis=-1, keepdims=True)
    var = jnp.mean(jnp.square(xf - mu), axis=-1, keepdims=True)
    y = (xf - mu) * lax.rsqrt(var + EPS)
    return (y * g.astype(jnp.float32) + b.astype(jnp.float32)).astype(x.dtype)


def l2norm(t):
    return t * lax.rsqrt(jnp.sum(t * t, axis=-1, keepdims=True) + EPS)


def causal_conv(x, w):
    S = x.shape[1]
    xp = jnp.pad(x, ((0, 0), (CONV_K - 1, 0), (0, 0)))
    return sum(xp[:, j:j + S, :] * w[j] for j in range(CONV_K))


def gmlp_branch(u, v, z, ln_g, ln_b, w_s, b_s):
    Bsz, S, _ = v.shape
    u = jax.nn.gelu(u)
    v = layernorm(jax.nn.gelu(v), ln_g, ln_b)
    pos_chunk = jnp.arange(SPATIAL_BLOCK) // CHUNK
    mask = pos_chunk[None, :] <= pos_chunk[:, None]
    w = jnp.where(mask, w_s, jnp.zeros_like(w_s))
    vb = v.reshape(Bsz, S // SPATIAL_BLOCK, SPATIAL_BLOCK, GMLP_GROUPS, GMLP_GROUP_DIM)
    mixed = jnp.einsum('gij,bnjgc->bnigc', w, vb) + b_s.T[None, None, :, :, None]
    mixed = mixed.reshape(Bsz, S, GMLP_WIDTH)
    return u * mixed * jax.nn.silu(z)


def chunk_gated_delta_rule(q, k, v, g, beta):
    C = q.shape[-2]
    causal = jnp.tril(jnp.ones((C, C), dtype=bool))
    strict = jnp.tril(jnp.ones((C, C), dtype=bool), -1)
    decay = jnp.cumsum(g, axis=-1)
    L = jnp.exp(jnp.where(causal, decay[..., :, None] - decay[..., None, :], -jnp.inf))
    k_beta = k * beta[..., None]
    A = jnp.where(strict, jnp.einsum('bhnik,bhnjk->bhnij', k_beta, k) * L, 0.0)
    eye = jnp.eye(C, dtype=q.dtype)
    T = lax.linalg.triangular_solve(A + eye, jnp.broadcast_to(eye, A.shape),
                                    left_side=True, lower=True, unit_diagonal=True)
    u = jnp.einsum('bhnij,bhnjv->bhniv', T, v * beta[..., None])
    w = jnp.einsum('bhnij,bhnjk->bhnik', T, k_beta * jnp.exp(decay)[..., None])
    qk = jnp.einsum('bhnik,bhnjk->bhnij', q, k) * L
    q_dec = q * jnp.exp(decay)[..., None]
    k_dec = k * jnp.exp(decay[..., -1:] - decay)[..., None]
    chunk_decay = jnp.exp(decay[..., -1])
    xs = tuple(jnp.moveaxis(t, 2, 0) for t in (q_dec, k_dec, u, w, qk, chunk_decay))

    def step(state, inp):
        q_c, k_c, u_c, w_c, qk_c, a_c = inp
        v_new = u_c - jnp.einsum('bhck,bhkv->bhcv', w_c, state)
        o_c = (jnp.einsum('bhck,bhkv->bhcv', q_c, state)
               + jnp.einsum('bhij,bhjv->bhiv', qk_c, v_new))
        state = state * a_c[..., None, None] + jnp.einsum('bhck,bhcv->bhkv', k_c, v_new)
        return state, o_c

    state0 = jnp.zeros(q.shape[:2] + (q.shape[-1], v.shape[-1]), q.dtype)
    _, o = lax.scan(step, state0, xs)
    return jnp.moveaxis(o, 0, 2)


def gdn_branch(qkv, z, a, b, conv_w, A_log, dt_bias, norm_w):
    Bsz, S, _ = qkv.shape
    N = S // CHUNK
    out_dtype = qkv.dtype
    qkv = jax.nn.silu(causal_conv(qkv, conv_w)).astype(jnp.float32)
    q, k, v = jnp.split(qkv, 3, axis=-1)

    def heads(t):
        return t.reshape(Bsz, N, CHUNK, GDN_HEADS, GDN_HEAD_DIM).transpose(0, 3, 1, 2, 4)

    def per_head(t):
        return t.reshape(Bsz, N, CHUNK, GDN_HEADS).transpose(0, 3, 1, 2)

    q = l2norm(heads(q)) * (GDN_HEAD_DIM ** -0.5)
    k = l2norm(heads(k))
    v = heads(v)
    beta = per_head(jax.nn.sigmoid(b.astype(jnp.float32)))
    g = per_head(-jnp.exp(A_log.astype(jnp.float32))
                 * jax.nn.softplus(a.astype(jnp.float32) + dt_bias.astype(jnp.float32)))
    o = chunk_gated_delta_rule(q, k, v, g, beta)
    o = o.transpose(0, 2, 3, 1, 4).reshape(Bsz, S, GDN_HEADS, GDN_HEAD_DIM)
    o = o * lax.rsqrt(jnp.mean(o * o, axis=-1, keepdims=True) + EPS) * norm_w.astype(jnp.float32)
    o = o * jax.nn.silu(z.astype(jnp.float32).reshape(Bsz, S, GDN_HEADS, GDN_HEAD_DIM))
    return o.reshape(Bsz, S, GDN_WIDTH).astype(out_dtype)


def setup_inputs(seed: int = 0) -> dict:
    key = jax.random.key(seed)
    ks = jax.random.split(key, 20)
    f32 = jnp.float32
    nrm = lambda k, shape, scale: jax.random.normal(k, shape, f32) * scale
    dt = jnp.exp(jax.random.uniform(ks[9], (DEPTH, GDN_HEADS), f32, np.log(1e-3), np.log(1e-1)))
    return {
        "x": nrm(ks[0], (BATCH, SEQ, D_MODEL), 1.0),
        "p": nrm(ks[1], (DEPTH, BATCH, SEQ, PLE_DIM), 1.0),
        "norm_w": 1.0 + nrm(ks[2], (DEPTH, D_MODEL), 0.02),
        "w_in": nrm(ks[3], (DEPTH, D_MODEL, IN_DIM), D_MODEL ** -0.5),
        "ln_v_g": 1.0 + nrm(ks[4], (DEPTH, GMLP_WIDTH), 0.02),
        "ln_v_b": nrm(ks[5], (DEPTH, GMLP_WIDTH), 0.01),
        "w_spatial": nrm(ks[6], (DEPTH, GMLP_GROUPS, SPATIAL_BLOCK, SPATIAL_BLOCK), SPATIAL_BLOCK ** -0.5),
        "b_spatial": 1.0 + nrm(ks[7], (DEPTH, GMLP_GROUPS, SPATIAL_BLOCK), 0.01),
        "conv_w": nrm(ks[8], (DEPTH, CONV_K, 3 * GDN_WIDTH), CONV_K ** -0.5),
        "A_log": jnp.log(jax.random.uniform(ks[10], (DEPTH, GDN_HEADS), f32, 1.0, 16.0)),
        "dt_bias": dt + jnp.log(-jnp.expm1(-dt)),
        "gdn_norm_w": 1.0 + nrm(ks[11], (DEPTH, GDN_HEAD_DIM), 0.02),
        "w_out": nrm(ks[12], (DEPTH, MIX_WIDTH, D_MODEL), MIX_WIDTH ** -0.5),
        "w_ple": nrm(ks[13], (DEPTH, PLE_DIM, D_MODEL), PLE_DIM ** -0.5),
        "ple_norm_w": 1.0 + nrm(ks[14], (DEPTH, D_MODEL), 0.02),
        "ple_gate_norm_w": 1.0 + nrm(ks[15], (DEPTH, D_MODEL), 0.02),
        "w_ple_gate": nrm(ks[16], (DEPTH, D_MODEL, D_MODEL), D_MODEL ** -0.5),
        "final_norm_w": 1.0 + nrm(ks[17], (D_MODEL,), 0.02),
    }


def reference(x, p, norm_w, w_in, ln_v_g, ln_v_b, w_spatial, b_spatial, conv_w, A_log, dt_bias,
              gdn_norm_w, w_out, w_ple, ple_norm_w, ple_gate_norm_w, w_ple_gate, final_norm_w):
    h = x
    for i in range(DEPTH):
        xn = rmsnorm(h, norm_w[i])
        proj = jnp.einsum('bsd,de->bse', xn, w_in[i])
        y_a = gmlp_branch(proj[..., OFF_UA:OFF_VA], proj[..., OFF_VA:OFF_ZA], proj[..., OFF_ZA:OFF_QKV],
                          ln_v_g[i], ln_v_b[i], w_spatial[i], b_spatial[i])
        y_b = gdn_branch(proj[..., OFF_QKV:OFF_ZB], proj[..., OFF_ZB:OFF_A],
                         proj[..., OFF_A:OFF_B], proj[..., OFF_B:IN_DIM],
                         conv_w[i], A_log[i], dt_bias[i], gdn_norm_w[i])
        y = jnp.concatenate([y_a, y_b], axis=-1)
        h = h + jnp.einsum('bse,ed->bsd', y, w_out[i])
        e = rmsnorm(jnp.einsum('bsp,pd->bsd', p[i], w_ple[i]), ple_norm_w[i])
        gate = jax.nn.sigmoid(jnp.einsum('bsd,de->bse', rmsnorm(h, ple_gate_norm_w[i]), w_ple_gate[i]))
        h = h + gate * e
    return rmsnorm(h, final_norm_w)
```

```python
import functools

import jax
import jax.numpy as jnp
from jax import lax
from jax.experimental import pallas as pl
from jax.experimental.pallas import tpu as pltpu

D_MODEL = 1024
GMLP_WIDTH = 1024
GMLP_GROUPS = 8
GROUP_DIM = 128
SPATIAL_BLOCK = 128
SPATIAL_CHUNK = 64
GDN_WIDTH = 1024
GDN_HEADS = 8
HEAD_DIM = 128
CONV_K = 4
PLE_DIM = 256
EPS = 1e-6
MAIN_COLS = 3 * GMLP_WIDTH + 4 * GDN_WIDTH

GDN_CHUNK = 128
TILE_PROJ = 256
TILE_OUT = 512
CONV_PAD = 8
VMEM_LIMIT_BYTES = 56 * 1024 * 1024

F32 = jnp.float32
BF16 = jnp.bfloat16


def _bf(x):
    return x.astype(BF16)


def _dot(a, b):
    return jnp.dot(a, b, preferred_element_type=F32)


def _dot_nt(a, b):
    return lax.dot_general(a, b, (((1,), (1,)), ((), ())), preferred_element_type=F32)


def _dot_tn(a, b):
    return lax.dot_general(a, b, (((0,), (0,)), ((), ())), preferred_element_type=F32)


def _rms(x, w):
    return x * lax.rsqrt(jnp.mean(x * x, axis=-1, keepdims=True) + EPS) * w


def _silu(x):
    return x * jax.nn.sigmoid(x)


def _softplus(x):
    return jnp.maximum(x, 0.0) + jnp.log1p(jnp.exp(-jnp.abs(x)))


def _proj_mix_kernel(tiles_per_seq,
                     h_ref, nw_ref, w_ref, wab_ref, wabt_ref, lng_ref, lnb_ref, wsp_ref, bspt_ref,
                     cw_ref, alog_c_ref, dtb_c_ref, alog_r_ref, dtb_r_ref,
                     ya_ref, qkv_ref, zb_ref, col_ref, row_ref,
                     cs_ref):
    tm = h_ref.shape[0]
    xn = _rms(h_ref[...], nw_ref[...]).astype(BF16)

    v = jax.nn.gelu(_dot(xn, w_ref[:, GMLP_WIDTH:2 * GMLP_WIDTH]))
    mu = jnp.mean(v, axis=-1, keepdims=True)
    vc = v - mu
    var = jnp.mean(vc * vc, axis=-1, keepdims=True)
    vln = (vc * lax.rsqrt(var + EPS) * lng_ref[...] + lnb_ref[...]).astype(BF16)
    u = jax.nn.gelu(_dot(xn, w_ref[:, 0:GMLP_WIDTH]))
    z = _silu(_dot(xn, w_ref[:, 2 * GMLP_WIDTH:3 * GMLP_WIDTH]))
    uz = u * z
    ri = lax.broadcasted_iota(jnp.int32, (SPATIAL_BLOCK, SPATIAL_BLOCK), 0) // SPATIAL_CHUNK
    ci = lax.broadcasted_iota(jnp.int32, (SPATIAL_BLOCK, SPATIAL_BLOCK), 1) // SPATIAL_CHUNK
    chunk_causal = ci <= ri
    for g in range(GMLP_GROUPS):
        gs = slice(g * GROUP_DIM, (g + 1) * GROUP_DIM)
        wm = jnp.where(chunk_causal, wsp_ref[g], jnp.zeros_like(wsp_ref[g]))
        bias = bspt_ref[:, g:g + 1]
        for r in range(tm // SPATIAL_BLOCK):
            rs = slice(r * SPATIAL_BLOCK, (r + 1) * SPATIAL_BLOCK)
            mixed = _dot(wm, vln[rs, gs]) + bias
            ya_ref[rs, gs] = (uz[rs, gs] * mixed).astype(ya_ref.dtype)

    @pl.when(pl.program_id(0) % tiles_per_seq == 0)
    def _():
        cs_ref[0:CONV_PAD, :] = jnp.zeros((CONV_PAD, 3 * GDN_WIDTH), F32)

    for part in range(3):
        cols = slice(part * GDN_WIDTH, (part + 1) * GDN_WIDTH)
        wcols = slice(3 * GMLP_WIDTH + part * GDN_WIDTH, 3 * GMLP_WIDTH + (part + 1) * GDN_WIDTH)
        cs_ref[CONV_PAD:CONV_PAD + tm, cols] = _dot(xn, w_ref[:, wcols])
        acc = cs_ref[CONV_PAD:CONV_PAD + tm, cols] * cw_ref[CONV_K - 1:CONV_K, cols]
        for j in range(CONV_K - 1):
            off = CONV_PAD - (CONV_K - 1) + j
            acc = acc + cs_ref[off:off + tm, cols] * cw_ref[j:j + 1, cols]
        cs_ref[0:CONV_PAD, cols] = cs_ref[tm:tm + CONV_PAD, cols]
        act = _silu(acc)
        if part == 2:
            qkv_ref[:, cols] = act.astype(qkv_ref.dtype)
        else:
            scale = HEAD_DIM ** -0.5 if part == 0 else 1.0
            for hd in range(GDN_HEADS):
                hs = slice(hd * HEAD_DIM, (hd + 1) * HEAD_DIM)
                t = act[:, hs]
                t = t * (lax.rsqrt(jnp.sum(t * t, axis=-1, keepdims=True) + EPS) * scale)
                qkv_ref[:, part * GDN_WIDTH + hd * HEAD_DIM:part * GDN_WIDTH + (hd + 1) * HEAD_DIM] = (
                    t.astype(qkv_ref.dtype))

    zb_ref[...] = _silu(_dot(xn, w_ref[:, 3 * GMLP_WIDTH + 3 * GDN_WIDTH:MAIN_COLS])).astype(zb_ref.dtype)

    ab = _dot(xn, wab_ref[...])
    g_col = -jnp.exp(alog_c_ref[...]) * _softplus(ab[:, 0:GDN_HEADS] + dtb_c_ref[...])
    beta = jax.nn.sigmoid(ab[:, GDN_HEADS:2 * GDN_HEADS])
    a_row = _dot_nt(wabt_ref[...], xn)
    g_row = -jnp.exp(alog_r_ref[...]) * _softplus(a_row + dtb_r_ref[...])
    ii = lax.broadcasted_iota(jnp.int32, (GDN_CHUNK, GDN_CHUNK), 0)
    jj = lax.broadcasted_iota(jnp.int32, (GDN_CHUNK, GDN_CHUNK), 1)
    tri_lower = (jj <= ii).astype(F32)
    tri_upper = (ii <= jj).astype(F32)
    for r in range(tm // GDN_CHUNK):
        rs = slice(r * GDN_CHUNK, (r + 1) * GDN_CHUNK)
        dcol = jnp.dot(tri_lower, g_col[rs, :], precision=lax.Precision.HIGHEST,
                       preferred_element_type=F32)
        col_ref[rs, 0:GDN_HEADS] = dcol
        col_ref[rs, GDN_HEADS:2 * GDN_HEADS] = beta[rs, :]
        row_ref[:, rs] = jnp.dot(g_row[:, rs], tri_upper, precision=lax.Precision.HIGHEST,
                                 preferred_element_type=F32)


def _proj_mix(h, nw, w_main, wab, wabt, lng, lnb, wsp, bspt, cw, alog, dtb, seq):
    t = h.shape[0]
    tm = TILE_PROJ
    const = lambda shape: pl.BlockSpec(shape, lambda i: (0,) * len(shape))
    return pl.pallas_call(
        functools.partial(_proj_mix_kernel, seq // tm),
        out_shape=(
            jax.ShapeDtypeStruct((t, GMLP_WIDTH), BF16),
            jax.ShapeDtypeStruct((t, 3 * GDN_WIDTH), BF16),
            jax.ShapeDtypeStruct((t, GDN_WIDTH), BF16),
            jax.ShapeDtypeStruct((t, 2 * GDN_HEADS), F32),
            jax.ShapeDtypeStruct((GDN_HEADS, t), F32),
        ),
        grid=(t // tm,),
        in_specs=[
            pl.BlockSpec((tm, D_MODEL), lambda i: (i, 0)),
            const((1, D_MODEL)),
            const((D_MODEL, MAIN_COLS)),
            const((D_MODEL, 2 * GDN_HEADS)),
            const((GDN_HEADS, D_MODEL)),
            const((1, GMLP_WIDTH)),
            const((1, GMLP_WIDTH)),
            const((GMLP_GROUPS, SPATIAL_BLOCK, SPATIAL_BLOCK)),
            const((SPATIAL_BLOCK, GMLP_GROUPS)),
            const((CONV_K, 3 * GDN_WIDTH)),
            const((1, GDN_HEADS)),
            const((1, GDN_HEADS)),
            const((GDN_HEADS, 1)),
            const((GDN_HEADS, 1)),
        ],
        out_specs=(
            pl.BlockSpec((tm, GMLP_WIDTH), lambda i: (i, 0)),
            pl.BlockSpec((tm, 3 * GDN_WIDTH), lambda i: (i, 0)),
            pl.BlockSpec((tm, GDN_WIDTH), lambda i: (i, 0)),
            pl.BlockSpec((tm, 2 * GDN_HEADS), lambda i: (i, 0)),
            pl.BlockSpec((GDN_HEADS, tm), lambda i: (0, i)),
        ),
        scratch_shapes=[pltpu.VMEM((tm + CONV_PAD, 3 * GDN_WIDTH), F32)],
        compiler_params=pltpu.CompilerParams(
            dimension_semantics=("arbitrary",), vmem_limit_bytes=VMEM_LIMIT_BYTES),
        name="proj_mix",
    )(h, nw, w_main, wab, wabt, lng, lnb, wsp, bspt, cw,
      alog[None, :], dtb[None, :], alog[:, None], dtb[:, None])


def _gdn_kernel(q_ref, k_ref, v_ref, zb_ref, col_ref, row_ref, gnw_ref, yb_ref, state_ref):
    c = GDN_CHUNK

    @pl.when(pl.program_id(1) == 0)
    def _():
        state_ref[...] = jnp.zeros(state_ref.shape, F32)

    col = col_ref[...]
    dcols = col[:, 0:GDN_HEADS]
    betas = col[:, GDN_HEADS:2 * GDN_HEADS]
    dlast = dcols[c - 1:c, :]
    e_dec = jnp.exp(dcols)
    e_rem = jnp.exp(dlast - dcols)
    e_last = jnp.exp(dlast)
    ii = lax.broadcasted_iota(jnp.int32, (c, c), 0)
    jj = lax.broadcasted_iota(jnp.int32, (c, c), 1)
    causal = jj <= ii
    strict = jj < ii
    eye = (ii == jj).astype(F32)
    heads = range(GDN_HEADS)

    def hsl(hd):
        return slice(hd * HEAD_DIM, (hd + 1) * HEAD_DIM)

    q = [q_ref[:, hsl(hd)] for hd in heads]
    k = [k_ref[:, hsl(hd)] for hd in heads]
    kf = [k[hd].astype(F32) for hd in heads]
    kb = [kf[hd] * betas[:, hd:hd + 1] for hd in heads]
    gram = [_dot_nt(jnp.concatenate([_bf(kb[hd]), q[hd]], axis=0), k[hd]) for hd in heads]
    lmat = [jnp.exp(jnp.where(causal, dcols[:, hd:hd + 1] - row_ref[hd:hd + 1, :], -jnp.inf))
            for hd in heads]
    a = [jnp.where(strict, gram[hd][0:c] * lmat[hd], 0.0) for hd in heads]
    qk = [_bf(gram[hd][c:2 * c] * lmat[hd]) for hd in heads]
    t = [eye - jnp.where((ii // 2) == (jj // 2), a[hd], 0.0) for hd in heads]
    b = 2
    while b < c:
        off = ((ii // (2 * b)) == (jj // (2 * b))) & ((ii // b) != (jj // b))
        tb = [_bf(t[hd]) for hd in heads]
        m = [_dot(tb[hd], _bf(jnp.where(off, a[hd], 0.0))) for hd in heads]
        t = [t[hd] - _dot(_bf(m[hd]), tb[hd]) for hd in heads]
        b *= 2
    r = [jnp.concatenate([v_ref[:, hsl(hd)].astype(F32) * betas[:, hd:hd + 1],
                          kb[hd] * e_dec[:, hd:hd + 1]], axis=1) for hd in heads]
    uw = [_dot(_bf(t[hd]), _bf(r[hd])) for hd in heads]
    s = [state_ref[hd] for hd in heads]
    lhs = [jnp.concatenate([_bf(uw[hd][:, HEAD_DIM:2 * HEAD_DIM]),
                            _bf(q[hd].astype(F32) * e_dec[:, hd:hd + 1])], axis=0) for hd in heads]
    ws_qs = [_dot(lhs[hd], _bf(s[hd])) for hd in heads]
    v_new = [_bf(uw[hd][:, 0:HEAD_DIM] - ws_qs[hd][0:c]) for hd in heads]
    o = [ws_qs[hd][c:2 * c] + _dot(qk[hd], v_new[hd]) for hd in heads]
    for hd in heads:
        kd = _bf(kf[hd] * e_rem[:, hd:hd + 1])
        state_ref[hd] = s[hd] * e_last[:, hd:hd + 1] + _dot_tn(kd, v_new[hd])
    for hd in heads:
        on = o[hd] * lax.rsqrt(jnp.mean(o[hd] * o[hd], axis=-1, keepdims=True) + EPS) * gnw_ref[...]
        yb_ref[:, hsl(hd)] = (on * zb_ref[:, hsl(hd)].astype(F32)).astype(yb_ref.dtype)


def _gdn(qkv, zb, col, row, gnw, batch, seq):
    t = qkv.shape[0]
    c = GDN_CHUNK
    n = seq // c
    tok = lambda b, j: (b * n + j, 0)
    return pl.pallas_call(
        _gdn_kernel,
        out_shape=jax.ShapeDtypeStruct((t, GDN_WIDTH), BF16),
        grid=(batch, n),
        in_specs=[
            pl.BlockSpec((c, GDN_WIDTH), lambda b, j: (b * n + j, 0)),
            pl.BlockSpec((c, GDN_WIDTH), lambda b, j: (b * n + j, 1)),
            pl.BlockSpec((c, GDN_WIDTH), lambda b, j: (b * n + j, 2)),
            pl.BlockSpec((c, GDN_WIDTH), tok),
            pl.BlockSpec((c, 2 * GDN_HEADS), tok),
            pl.BlockSpec((GDN_HEADS, c), lambda b, j: (0, b * n + j)),
            pl.BlockSpec((1, HEAD_DIM), lambda b, j: (0, 0)),
        ],
        out_specs=pl.BlockSpec((c, GDN_WIDTH), tok),
        scratch_shapes=[pltpu.VMEM((GDN_HEADS, HEAD_DIM, HEAD_DIM), F32)],
        compiler_params=pltpu.CompilerParams(
            dimension_semantics=("arbitrary", "arbitrary"), vmem_limit_bytes=VMEM_LIMIT_BYTES),
        name="gdn",
    )(qkv, qkv, qkv, zb, col, row, gnw)


def _out_ple_kernel(apply_final_norm,
                    h_ref, ya_ref, yb_ref, p_ref, wout_ref, wple_ref, pnw_ref, gnw_ref, wg_ref, fnw_ref,
                    o_ref):
    h1 = (h_ref[...] + _dot(ya_ref[...], wout_ref[0:GMLP_WIDTH, :])
          + _dot(yb_ref[...], wout_ref[GMLP_WIDTH:GMLP_WIDTH + GDN_WIDTH, :]))
    e = _rms(_dot(p_ref[...].astype(BF16), wple_ref[...]), pnw_ref[...])
    gate = jax.nn.sigmoid(_dot(_rms(h1, gnw_ref[...]).astype(BF16), wg_ref[...]))
    h2 = h1 + gate * e
    if apply_final_norm:
        h2 = _rms(h2, fnw_ref[...])
    o_ref[...] = h2


def _out_ple(h, ya, yb, p, wout, wple, pnw, gnw, wg, fnw, apply_final_norm):
    t = h.shape[0]
    tm = TILE_OUT
    const = lambda shape: pl.BlockSpec(shape, lambda i: (0,) * len(shape))
    tok = lambda width: pl.BlockSpec((tm, width), lambda i: (i, 0))
    return pl.pallas_call(
        functools.partial(_out_ple_kernel, apply_final_norm),
        out_shape=jax.ShapeDtypeStruct((t, D_MODEL), F32),
        grid=(t // tm,),
        in_specs=[
            tok(D_MODEL), tok(GMLP_WIDTH), tok(GDN_WIDTH), tok(PLE_DIM),
            const((GMLP_WIDTH + GDN_WIDTH, D_MODEL)),
            const((PLE_DIM, D_MODEL)),
            const((1, D_MODEL)),
            const((1, D_MODEL)),
            const((D_MODEL, D_MODEL)),
            const((1, D_MODEL)),
        ],
        out_specs=tok(D_MODEL),
        compiler_params=pltpu.CompilerParams(
            dimension_semantics=("arbitrary",), vmem_limit_bytes=VMEM_LIMIT_BYTES),
        name="out_ple",
    )(h, ya, yb, p, wout, wple, pnw, gnw, wg, fnw)


def kernel(x, p, norm_w, w_in, ln_v_g, ln_v_b, w_spatial, b_spatial, conv_w, A_log, dt_bias,
           gdn_norm_w, w_out, w_ple, ple_norm_w, ple_gate_norm_w, w_ple_gate, final_norm_w):
    batch, seq, d = x.shape
    depth = w_in.shape[0]
    assert d == D_MODEL and seq % TILE_PROJ == 0 and (batch * seq) % TILE_OUT == 0
    assert w_in.shape[2] == MAIN_COLS + 2 * GDN_HEADS
    t = batch * seq
    h = x.reshape(t, d)
    p2 = p.reshape(depth, t, PLE_DIM)
    for i in range(depth):
        w_main = w_in[i, :, 0:MAIN_COLS].astype(BF16)
        wab = w_in[i, :, MAIN_COLS:].astype(BF16)
        wabt = w_in[i, :, MAIN_COLS:MAIN_COLS + GDN_HEADS].T.astype(BF16)
        ya, qkv, zb, col, row = _proj_mix(
            h, norm_w[i][None], w_main, wab, wabt, ln_v_g[i][None], ln_v_b[i][None],
            w_spatial[i].astype(BF16), b_spatial[i].T, conv_w[i], A_log[i], dt_bias[i], seq)
        yb = _gdn(qkv, zb, col, row, gdn_norm_w[i][None], batch, seq)
        h = _out_ple(h, ya, yb, p2[i], w_out[i].astype(BF16), w_ple[i].astype(BF16),
                     ple_norm_w[i][None], ple_gate_norm_w[i][None], w_ple_gate[i].astype(BF16),
                     final_norm_w[None], apply_final_norm=(i == depth - 1))
    return h.reshape(batch, seq, d)
```

```python
import functools

import jax
import jax.numpy as jnp
from jax import lax
from jax.experimental import pallas as pl
from jax.experimental.pallas import tpu as pltpu

D_MODEL = 1024
GMLP_WIDTH = 1024
GMLP_GROUPS = 8
GROUP_DIM = 128
SPATIAL_BLOCK = 128
SPATIAL_CHUNK = 64
GDN_WIDTH = 1024
GDN_HEADS = 8
HEAD_DIM = 128
CONV_K = 4
PLE_DIM = 256
EPS = 1e-6
MAIN_COLS = 3 * GMLP_WIDTH + 4 * GDN_WIDTH

GDN_CHUNK = 128
TILE_MIX = 256
TILE_OUT = 512
CONV_PAD = 8
VMEM_LIMIT_BYTES = 56 * 1024 * 1024

F32 = jnp.float32
BF16 = jnp.bfloat16


def _bf(x):
    return x.astype(BF16)


def _dot(a, b):
    return jnp.dot(a, b, preferred_element_type=F32)


def _dot_nt(a, b):
    return lax.dot_general(a, b, (((1,), (1,)), ((), ())), preferred_element_type=F32)


def _dot_tn(a, b):
    return lax.dot_general(a, b, (((0,), (0,)), ((), ())), preferred_element_type=F32)


def _rms(x, w):
    return x * lax.rsqrt(jnp.mean(x * x, axis=-1, keepdims=True) + EPS) * w


def _silu(x):
    return x * jax.nn.sigmoid(x)


def _softplus(x):
    return jnp.maximum(x, 0.0) + jnp.log1p(jnp.exp(-jnp.abs(x)))


def _head_cols(base, hd):
    return slice(base + hd * HEAD_DIM, base + (hd + 1) * HEAD_DIM)


def _gmlp_branch(xn, w_ref, lng_ref, lnb_ref, wsp_ref, bspt_ref, ya_ref):
    tm = xn.shape[0]
    v = jax.nn.gelu(_dot(xn, w_ref[:, GMLP_WIDTH:2 * GMLP_WIDTH]))
    mu = jnp.mean(v, axis=-1, keepdims=True)
    vc = v - mu
    var = jnp.mean(vc * vc, axis=-1, keepdims=True)
    vln = _bf(vc * lax.rsqrt(var + EPS) * lng_ref[...] + lnb_ref[...])
    u = jax.nn.gelu(_dot(xn, w_ref[:, 0:GMLP_WIDTH]))
    z = _silu(_dot(xn, w_ref[:, 2 * GMLP_WIDTH:3 * GMLP_WIDTH]))
    uz = u * z
    ri = lax.broadcasted_iota(jnp.int32, (SPATIAL_BLOCK, SPATIAL_BLOCK), 0) // SPATIAL_CHUNK
    ci = lax.broadcasted_iota(jnp.int32, (SPATIAL_BLOCK, SPATIAL_BLOCK), 1) // SPATIAL_CHUNK
    chunk_causal = ci <= ri
    for g in range(GMLP_GROUPS):
        gs = slice(g * GROUP_DIM, (g + 1) * GROUP_DIM)
        wm = jnp.where(chunk_causal, wsp_ref[g], jnp.zeros_like(wsp_ref[g]))
        bias = bspt_ref[:, g:g + 1]
        for r in range(tm // SPATIAL_BLOCK):
            rs = slice(r * SPATIAL_BLOCK, (r + 1) * SPATIAL_BLOCK)
            mixed = _dot(wm, vln[rs, gs]) + bias
            ya_ref[rs, gs] = (uz[rs, gs] * mixed).astype(ya_ref.dtype)


def _gdn_prepare(xn, w_ref, wab_ref, wabt_ref, cw_ref, alog_c_ref, dtb_c_ref, alog_r_ref, dtb_r_ref,
                 cs_ref, qkv_ref, zb_ref, col_ref, row_ref):
    tm = xn.shape[0]
    sub = lax.broadcasted_iota(jnp.int32, (CONV_PAD, GDN_WIDTH), 0)
    for part in range(3):
        cols = slice(part * GDN_WIDTH, (part + 1) * GDN_WIDTH)
        wcols = slice(3 * GMLP_WIDTH + part * GDN_WIDTH, 3 * GMLP_WIDTH + (part + 1) * GDN_WIDTH)
        x = _dot(xn, w_ref[:, wcols])
        prev = cs_ref[:, cols]
        acc = x * cw_ref[CONV_K - 1:CONV_K, cols]
        for shift in range(1, CONV_K):
            xs = pltpu.roll(x, shift, axis=0)
            head = jnp.where(sub < shift, pltpu.roll(prev, shift, axis=0), xs[0:CONV_PAD])
            shifted = jnp.concatenate([head, xs[CONV_PAD:]], axis=0)
            acc = acc + shifted * cw_ref[CONV_K - 1 - shift:CONV_K - shift, cols]
        cs_ref[:, cols] = x[tm - CONV_PAD:tm]
        act = _silu(acc)
        if part == 2:
            qkv_ref[:, cols] = _bf(act)
        else:
            scale = HEAD_DIM ** -0.5 if part == 0 else 1.0
            for hd in range(GDN_HEADS):
                t = act[:, _head_cols(0, hd)]
                t = t * (lax.rsqrt(jnp.sum(t * t, axis=-1, keepdims=True) + EPS) * scale)
                qkv_ref[:, _head_cols(part * GDN_WIDTH, hd)] = _bf(t)

    zb_ref[...] = _bf(_silu(_dot(xn, w_ref[:, 3 * GMLP_WIDTH + 3 * GDN_WIDTH:MAIN_COLS])))

    ab = _dot(xn, wab_ref[...])
    g_col = -jnp.exp(alog_c_ref[...]) * _softplus(ab[:, 0:GDN_HEADS] + dtb_c_ref[...])
    beta = jax.nn.sigmoid(ab[:, GDN_HEADS:2 * GDN_HEADS])
    a_row = _dot_nt(wabt_ref[...], xn)
    g_row = -jnp.exp(alog_r_ref[...]) * _softplus(a_row + dtb_r_ref[...])
    ii = lax.broadcasted_iota(jnp.int32, (GDN_CHUNK, GDN_CHUNK), 0)
    jj = lax.broadcasted_iota(jnp.int32, (GDN_CHUNK, GDN_CHUNK), 1)
    tri_lower = (jj <= ii).astype(F32)
    tri_upper = (ii <= jj).astype(F32)
    for r in range(tm // GDN_CHUNK):
        rs = slice(r * GDN_CHUNK, (r + 1) * GDN_CHUNK)
        col_ref[rs, 0:GDN_HEADS] = jnp.dot(tri_lower, g_col[rs, :], precision=lax.Precision.HIGHEST,
                                           preferred_element_type=F32)
        col_ref[rs, GDN_HEADS:2 * GDN_HEADS] = beta[rs, :]
        row_ref[:, rs] = jnp.dot(g_row[:, rs], tri_upper, precision=lax.Precision.HIGHEST,
                                 preferred_element_type=F32)


def _gdn_chunks(qkv_ref, zb_ref, col_ref, row_ref, gnw_ref, state_ref, yb_ref):
    c = GDN_CHUNK
    n_chunks = qkv_ref.shape[0] // c
    ii = lax.broadcasted_iota(jnp.int32, (c, c), 0)
    jj = lax.broadcasted_iota(jnp.int32, (c, c), 1)
    causal = jj <= ii
    strict = jj < ii
    eye = (ii == jj).astype(F32)
    units = [(j, hd) for j in range(n_chunks) for hd in range(GDN_HEADS)]
    n_units = range(len(units))

    def rows(j):
        return slice(j * c, (j + 1) * c)

    dcols, betas, e_dec, e_rem, e_last = [], [], [], [], []
    for j in range(n_chunks):
        col = col_ref[rows(j), :]
        d = col[:, 0:GDN_HEADS]
        dlast = d[c - 1:c, :]
        dcols.append(d)
        betas.append(col[:, GDN_HEADS:2 * GDN_HEADS])
        e_dec.append(jnp.exp(d))
        e_rem.append(jnp.exp(dlast - d))
        e_last.append(jnp.exp(dlast))

    def colv(per_chunk, u):
        j, hd = units[u]
        return per_chunk[j][:, hd:hd + 1]

    q = [qkv_ref[rows(j), _head_cols(0, hd)] for j, hd in units]
    k = [qkv_ref[rows(j), _head_cols(GDN_WIDTH, hd)] for j, hd in units]
    kf = [k[u].astype(F32) for u in n_units]
    kb = [kf[u] * colv(betas, u) for u in n_units]
    gram = [_dot_nt(jnp.concatenate([_bf(kb[u]), q[u]], axis=0), k[u]) for u in n_units]
    lmat = [jnp.exp(jnp.where(causal, colv(dcols, u) - row_ref[hd:hd + 1, rows(j)], -jnp.inf))
            for u, (j, hd) in enumerate(units)]
    a = [jnp.where(strict, gram[u][0:c] * lmat[u], 0.0) for u in n_units]
    qk = [_bf(gram[u][c:2 * c] * lmat[u]) for u in n_units]
    t = [eye - jnp.where((ii // 2) == (jj // 2), a[u], 0.0) for u in n_units]
    b = 2
    while b < c:
        off = ((ii // (2 * b)) == (jj // (2 * b))) & ((ii // b) != (jj // b))
        tb = [_bf(t[u]) for u in n_units]
        m = [_dot(tb[u], _bf(jnp.where(off, a[u], 0.0))) for u in n_units]
        t = [t[u] - _dot(_bf(m[u]), tb[u]) for u in n_units]
        b *= 2
    r = [jnp.concatenate([qkv_ref[rows(j), _head_cols(2 * GDN_WIDTH, hd)].astype(F32) * colv(betas, u),
                          kb[u] * colv(e_dec, u)], axis=1) for u, (j, hd) in enumerate(units)]
    uw = [_dot(_bf(t[u]), _bf(r[u])) for u in n_units]
    lhs = [jnp.concatenate([_bf(uw[u][:, HEAD_DIM:2 * HEAD_DIM]),
                            _bf(q[u].astype(F32) * colv(e_dec, u))], axis=0) for u in n_units]
    kd = [_bf(kf[u] * colv(e_rem, u)) for u in n_units]
    s = [state_ref[hd] for hd in range(GDN_HEADS)]
    for j in range(n_chunks):
        us = [j * GDN_HEADS + hd for hd in range(GDN_HEADS)]
        ws_qs = [_dot(lhs[u], _bf(s[hd])) for hd, u in enumerate(us)]
        v_new = [_bf(uw[u][:, 0:HEAD_DIM] - ws_qs[hd][0:c]) for hd, u in enumerate(us)]
        o = [ws_qs[hd][c:2 * c] + _dot(qk[u], v_new[hd]) for hd, u in enumerate(us)]
        s = [s[hd] * colv(e_last, u) + _dot_tn(kd[u], v_new[hd]) for hd, u in enumerate(us)]
        for hd in range(GDN_HEADS):
            on = o[hd] * lax.rsqrt(jnp.mean(o[hd] * o[hd], axis=-1, keepdims=True) + EPS) * gnw_ref[...]
            yb_ref[rows(j), _head_cols(0, hd)] = (
                on * zb_ref[rows(j), _head_cols(0, hd)].astype(F32)).astype(yb_ref.dtype)
    for hd in range(GDN_HEADS):
        state_ref[hd] = s[hd]


def _mix_kernel(tiles_per_seq,
                h_ref, nw_ref, w_ref, wab_ref, wabt_ref, lng_ref, lnb_ref, wsp_ref, bspt_ref,
                cw_ref, alog_c_ref, dtb_c_ref, alog_r_ref, dtb_r_ref, gnw_ref,
                ya_ref, yb_ref,
                cs_ref, state_ref, qkv_ref, zb_ref, col_ref, row_ref):
    @pl.when(pl.program_id(0) % tiles_per_seq == 0)
    def _():
        cs_ref[...] = jnp.zeros(cs_ref.shape, F32)
        state_ref[...] = jnp.zeros(state_ref.shape, F32)

    xn = _bf(_rms(h_ref[...], nw_ref[...]))
    _gmlp_branch(xn, w_ref, lng_ref, lnb_ref, wsp_ref, bspt_ref, ya_ref)
    _gdn_prepare(xn, w_ref, wab_ref, wabt_ref, cw_ref, alog_c_ref, dtb_c_ref, alog_r_ref, dtb_r_ref,
                 cs_ref, qkv_ref, zb_ref, col_ref, row_ref)
    _gdn_chunks(qkv_ref, zb_ref, col_ref, row_ref, gnw_ref, state_ref, yb_ref)


def _layer_spec(layer, shape):
    return pl.BlockSpec((None,) + shape, lambda i: (layer,) + (0,) * len(shape))


def _mix(layer, h, nw, w_in, wab, wabt, lng, lnb, wsp, bspt, cw, alog_c, dtb_c, alog_r, dtb_r, gnw, seq):
    t = h.shape[0]
    tm = TILE_MIX
    spec = functools.partial(_layer_spec, layer)
    return pl.pallas_call(
        functools.partial(_mix_kernel, seq // tm),
        out_shape=(
            jax.ShapeDtypeStruct((t, GMLP_WIDTH), BF16),
            jax.ShapeDtypeStruct((t, GDN_WIDTH), BF16),
        ),
        grid=(t // tm,),
        in_specs=[
            pl.BlockSpec((tm, D_MODEL), lambda i: (i, 0)),
            spec((1, D_MODEL)),
            spec((D_MODEL, MAIN_COLS)),
            spec((D_MODEL, 2 * GDN_HEADS)),
            spec((GDN_HEADS, D_MODEL)),
            spec((1, GMLP_WIDTH)),
            spec((1, GMLP_WIDTH)),
            spec((GMLP_GROUPS, SPATIAL_BLOCK, SPATIAL_BLOCK)),
            spec((SPATIAL_BLOCK, GMLP_GROUPS)),
            spec((CONV_K, 3 * GDN_WIDTH)),
            spec((1, GDN_HEADS)),
            spec((1, GDN_HEADS)),
            spec((GDN_HEADS, 1)),
            spec((GDN_HEADS, 1)),
            spec((1, HEAD_DIM)),
        ],
        out_specs=(
            pl.BlockSpec((tm, GMLP_WIDTH), lambda i: (i, 0)),
            pl.BlockSpec((tm, GDN_WIDTH), lambda i: (i, 0)),
        ),
        scratch_shapes=[
            pltpu.VMEM((CONV_PAD, 3 * GDN_WIDTH), F32),
            pltpu.VMEM((GDN_HEADS, HEAD_DIM, HEAD_DIM), F32),
            pltpu.VMEM((tm, 3 * GDN_WIDTH), BF16),
            pltpu.VMEM((tm, GDN_WIDTH), BF16),
            pltpu.VMEM((tm, 2 * GDN_HEADS), F32),
            pltpu.VMEM((GDN_HEADS, tm), F32),
        ],
        compiler_params=pltpu.CompilerParams(
            dimension_semantics=("arbitrary",), vmem_limit_bytes=VMEM_LIMIT_BYTES),
        name="mix",
    )(h, nw, w_in, wab, wabt, lng, lnb, wsp, bspt, cw, alog_c, dtb_c, alog_r, dtb_r, gnw)


def _out_ple_kernel(apply_final_norm,
                    h_ref, ya_ref, yb_ref, p_ref, wout_ref, wple_ref, pnw_ref, gnw_ref, wg_ref, fnw_ref,
                    o_ref):
    h1 = (h_ref[...] + _dot(ya_ref[...], wout_ref[0:GMLP_WIDTH, :])
          + _dot(yb_ref[...], wout_ref[GMLP_WIDTH:GMLP_WIDTH + GDN_WIDTH, :]))
    e = _rms(_dot(_bf(p_ref[...]), wple_ref[...]), pnw_ref[...])
    gate = jax.nn.sigmoid(_dot(_bf(_rms(h1, gnw_ref[...])), wg_ref[...]))
    h2 = h1 + gate * e
    if apply_final_norm:
        h2 = _rms(h2, fnw_ref[...])
    o_ref[...] = h2


def _out_ple(layer, h, ya, yb, p, wout, wple, pnw, gnw, wg, fnw, apply_final_norm):
    t = h.shape[0]
    tm = TILE_OUT
    spec = functools.partial(_layer_spec, layer)
    tok = lambda width: pl.BlockSpec((tm, width), lambda i: (i, 0))
    return pl.pallas_call(
        functools.partial(_out_ple_kernel, apply_final_norm),
        out_shape=jax.ShapeDtypeStruct((t, D_MODEL), F32),
        grid=(t // tm,),
        in_specs=[
            tok(D_MODEL), tok(GMLP_WIDTH), tok(GDN_WIDTH),
            pl.BlockSpec((None, tm, PLE_DIM), lambda i: (layer, i, 0)),
            spec((GMLP_WIDTH + GDN_WIDTH, D_MODEL)),
            spec((PLE_DIM, D_MODEL)),
            spec((1, D_MODEL)),
            spec((1, D_MODEL)),
            spec((D_MODEL, D_MODEL)),
            pl.BlockSpec((1, D_MODEL), lambda i: (0, 0)),
        ],
        out_specs=tok(D_MODEL),
        compiler_params=pltpu.CompilerParams(
            dimension_semantics=("arbitrary",), vmem_limit_bytes=VMEM_LIMIT_BYTES),
        name="out_ple",
    )(h, ya, yb, p, wout, wple, pnw, gnw, wg, fnw)


def kernel(x, p, norm_w, w_in, ln_v_g, ln_v_b, w_spatial, b_spatial, conv_w, A_log, dt_bias,
           gdn_norm_w, w_out, w_ple, ple_norm_w, ple_gate_norm_w, w_ple_gate, final_norm_w):
    batch, seq, d = x.shape
    depth = w_in.shape[0]
    assert d == D_MODEL and seq % TILE_MIX == 0 and (batch * seq) % TILE_OUT == 0
    assert w_in.shape[2] == MAIN_COLS + 2 * GDN_HEADS
    t = batch * seq
    h = x.reshape(t, d)
    p2 = p.reshape(depth, t, PLE_DIM)
    w_in_b = _bf(w_in)
    wab = w_in_b[:, :, MAIN_COLS:]
    wabt = jnp.swapaxes(wab[:, :, 0:GDN_HEADS], 1, 2)
    wsp_b, wout_b, wple_b, wg_b = _bf(w_spatial), _bf(w_out), _bf(w_ple), _bf(w_ple_gate)
    bspt = jnp.swapaxes(b_spatial, 1, 2)
    row3 = lambda a: a[:, None, :]
    col3 = lambda a: a[:, :, None]
    for i in range(depth):
        ya, yb = _mix(i, h, row3(norm_w), w_in_b, wab, wabt, row3(ln_v_g), row3(ln_v_b), wsp_b, bspt,
                      conv_w, row3(A_log), row3(dt_bias), col3(A_log), col3(dt_bias), row3(gdn_norm_w), seq)
        h = _out_ple(i, h, ya, yb, p2, wout_b, wple_b, row3(ple_norm_w), row3(ple_gate_norm_w), wg_b,
                     final_norm_w[None], apply_final_norm=(i == depth - 1))
    return h.reshape(batch, seq, d)
```

```python
import functools

import jax
import jax.numpy as jnp
from jax import lax
from jax.experimental import pallas as pl
from jax.experimental.pallas import tpu as pltpu

D_MODEL = 1024
GMLP_WIDTH = 1024
GMLP_GROUPS = 8
GROUP_DIM = 128
SPATIAL_BLOCK = 128
SPATIAL_CHUNK = 64
GDN_WIDTH = 1024
GDN_HEADS = 8
HEAD_DIM = 128
CONV_K = 4
PLE_DIM = 256
EPS = 1e-6
MAIN_COLS = 3 * GMLP_WIDTH + 4 * GDN_WIDTH

GDN_CHUNK = 128
TILE_MIX = 256
TILE_OUT = 512
STRIP_COLS = 256
BF16_ROWS = 16
CONV_PAD = 8
VMEM_LIMIT_BYTES = 56 * 1024 * 1024

F32 = jnp.float32
BF16 = jnp.bfloat16
LOG2E = 1.4426950408889634
GELU_K1 = -2.0 * 0.7978845608028654 * LOG2E
GELU_K3 = GELU_K1 * 0.044715


def _bf(x):
    return x.astype(BF16)


def _dot(a, b):
    return jnp.dot(a, b, preferred_element_type=F32)


def _dot_nt(a, b):
    return lax.dot_general(a, b, (((1,), (1,)), ((), ())), preferred_element_type=F32)


def _dot_tn(a, b):
    return lax.dot_general(a, b, (((0,), (0,)), ((), ())), preferred_element_type=F32)


def _rms(x, w):
    return x * lax.rsqrt(jnp.mean(x * x, axis=-1, keepdims=True) + EPS) * w


def _silu(x):
    return x / (1.0 + jnp.exp2(x * -LOG2E))


def _gelu_tanh(x):
    return x / (1.0 + jnp.exp2(x * (GELU_K1 + GELU_K3 * (x * x))))


def _softplus(x):
    return jnp.maximum(x, 0.0) + jnp.log1p(jnp.exp(-jnp.abs(x)))


def _head_cols(base, hd):
    return slice(base + hd * HEAD_DIM, base + (hd + 1) * HEAD_DIM)


def _gmlp_branch(xn, w_ref, lng_ref, lnb_ref, wsp_ref, bspt_ref, ya_ref):
    tm = xn.shape[0]
    strips = range(0, GMLP_WIDTH, STRIP_COLS)

    def proj(base, c0):
        return _dot(xn, w_ref[:, base + c0:base + c0 + STRIP_COLS])

    v = jnp.concatenate([_gelu_tanh(proj(GMLP_WIDTH, c0)) for c0 in strips], axis=1)
    mu = jnp.mean(v, axis=-1, keepdims=True)
    vc = v - mu
    var = jnp.mean(vc * vc, axis=-1, keepdims=True)
    vln = _bf(vc * lax.rsqrt(var + EPS) * lng_ref[...] + lnb_ref[...])
    uz = jnp.concatenate([_gelu_tanh(proj(0, c0)) * _silu(proj(2 * GMLP_WIDTH, c0)) for c0 in strips], axis=1)
    ri = lax.broadcasted_iota(jnp.int32, (SPATIAL_BLOCK, SPATIAL_BLOCK), 0) // SPATIAL_CHUNK
    ci = lax.broadcasted_iota(jnp.int32, (SPATIAL_BLOCK, SPATIAL_BLOCK), 1) // SPATIAL_CHUNK
    chunk_causal = ci <= ri
    for g in range(GMLP_GROUPS):
        gs = slice(g * GROUP_DIM, (g + 1) * GROUP_DIM)
        wm = jnp.where(chunk_causal, wsp_ref[g], jnp.zeros_like(wsp_ref[g]))
        bias = bspt_ref[:, g:g + 1]
        for r in range(tm // SPATIAL_BLOCK):
            rs = slice(r * SPATIAL_BLOCK, (r + 1) * SPATIAL_BLOCK)
            mixed = _dot(wm, vln[rs, gs]) + bias
            ya_ref[rs, gs] = (uz[rs, gs] * mixed).astype(ya_ref.dtype)


def _gdn_prepare(xn, w_ref, wab_ref, wabt_ref, cw_ref, alog_c_ref, dtb_c_ref, alog_r_ref, dtb_r_ref,
                 cs_ref, qkv_ref, zb_ref, col_ref, row_ref):
    tm = xn.shape[0]
    sub = lax.broadcasted_iota(jnp.int32, (CONV_PAD, STRIP_COLS), 0)
    for c0 in range(0, 3 * GDN_WIDTH, STRIP_COLS):
        cols = slice(c0, c0 + STRIP_COLS)
        x = _dot(xn, w_ref[:, 3 * GMLP_WIDTH + c0:3 * GMLP_WIDTH + c0 + STRIP_COLS])
        prev = cs_ref[:, cols]
        acc = x * cw_ref[CONV_K - 1:CONV_K, cols]
        for shift in range(1, CONV_K):
            xs = pltpu.roll(x, shift, axis=0)
            head = jnp.where(sub < shift, pltpu.roll(prev, shift, axis=0), xs[0:CONV_PAD])
            shifted = jnp.concatenate([head, xs[CONV_PAD:]], axis=0)
            acc = acc + shifted * cw_ref[CONV_K - 1 - shift:CONV_K - shift, cols]
        cs_ref[:, cols] = x[tm - CONV_PAD:tm]
        act = _silu(acc)
        if c0 >= 2 * GDN_WIDTH:
            qkv_ref[:, cols] = _bf(act)
        else:
            scale = HEAD_DIM ** -0.5 if c0 < GDN_WIDTH else 1.0
            for hd in range(STRIP_COLS // HEAD_DIM):
                t = act[:, _head_cols(0, hd)]
                t = t * (lax.rsqrt(jnp.sum(t * t, axis=-1, keepdims=True) + EPS) * scale)
                qkv_ref[:, _head_cols(c0, hd)] = _bf(t)

    for c0 in range(0, GDN_WIDTH, STRIP_COLS):
        zb_ref[:, c0:c0 + STRIP_COLS] = _bf(_silu(_dot(
            xn, w_ref[:, 3 * GMLP_WIDTH + 3 * GDN_WIDTH + c0:3 * GMLP_WIDTH + 3 * GDN_WIDTH + c0 + STRIP_COLS])))

    ab = _dot(xn, wab_ref[...])
    g_col = -jnp.exp(alog_c_ref[...]) * _softplus(ab[:, 0:GDN_HEADS] + dtb_c_ref[...])
    beta = jax.nn.sigmoid(ab[:, GDN_HEADS:2 * GDN_HEADS])
    a_row = _dot_nt(wabt_ref[...], xn)
    g_row = -jnp.exp(alog_r_ref[...]) * _softplus(a_row + dtb_r_ref[...])
    ii = lax.broadcasted_iota(jnp.int32, (GDN_CHUNK, GDN_CHUNK), 0)
    jj = lax.broadcasted_iota(jnp.int32, (GDN_CHUNK, GDN_CHUNK), 1)
    tri_lower = (jj <= ii).astype(F32)
    tri_upper = (ii <= jj).astype(F32)
    for r in range(tm // GDN_CHUNK):
        rs = slice(r * GDN_CHUNK, (r + 1) * GDN_CHUNK)
        col_ref[rs, 0:GDN_HEADS] = jnp.dot(tri_lower, g_col[rs, :], precision=lax.Precision.HIGHEST,
                                           preferred_element_type=F32)
        col_ref[rs, GDN_HEADS:2 * GDN_HEADS] = beta[rs, :]
        row_ref[:, rs] = jnp.dot(g_row[:, rs], tri_upper, precision=lax.Precision.HIGHEST,
                                 preferred_element_type=F32)


def _gdn_chunks(qkv_ref, zb_ref, col_ref, row_ref, gnw_ref, state_ref, yb_ref):
    c = GDN_CHUNK
    n_chunks = qkv_ref.shape[0] // c
    ii = lax.broadcasted_iota(jnp.int32, (c, c), 0)
    jj = lax.broadcasted_iota(jnp.int32, (c, c), 1)
    causal = jj <= ii
    strict = jj < ii
    eye = (ii == jj).astype(F32)
    units = [(j, hd) for j in range(n_chunks) for hd in range(GDN_HEADS)]
    n_units = range(len(units))

    def rows(j):
        return slice(j * c, (j + 1) * c)

    dcols, betas, e_dec, e_rem, e_last = [], [], [], [], []
    for j in range(n_chunks):
        col = col_ref[rows(j), :]
        d = col[:, 0:GDN_HEADS]
        dlast = d[c - 1:c, :]
        dcols.append(d)
        betas.append(col[:, GDN_HEADS:2 * GDN_HEADS])
        e_dec.append(jnp.exp(d))
        e_rem.append(jnp.exp(dlast - d))
        e_last.append(jnp.exp(dlast))

    def colv(per_chunk, u):
        j, hd = units[u]
        return per_chunk[j][:, hd:hd + 1]

    q = [qkv_ref[rows(j), _head_cols(0, hd)] for j, hd in units]
    k = [qkv_ref[rows(j), _head_cols(GDN_WIDTH, hd)] for j, hd in units]
    kf = [k[u].astype(F32) for u in n_units]
    kb = [kf[u] * colv(betas, u) for u in n_units]
    gram = [_dot_nt(jnp.concatenate([_bf(kb[u]), q[u]], axis=0), k[u]) for u in n_units]
    lmat = [jnp.exp(jnp.where(causal, colv(dcols, u) - row_ref[hd:hd + 1, rows(j)], -jnp.inf))
            for u, (j, hd) in enumerate(units)]
    a = [jnp.where(strict, gram[u][0:c] * lmat[u], 0.0) for u in n_units]
    qk = [_bf(gram[u][c:2 * c] * lmat[u]) for u in n_units]
    a_b = [_bf(a[u]) for u in n_units]
    tb = [_bf(eye) - jnp.where((ii // 2) == (jj // 2), a_b[u], jnp.zeros_like(a_b[u])) for u in n_units]
    b = 2
    while b < c:
        off = _bf((((ii // (2 * b)) == (jj // (2 * b))) & ((ii // b) != (jj // b))).astype(F32))
        if b % BF16_ROWS == 0:
            lower = [slice(s0 + b, s0 + 2 * b) for s0 in range(0, c, 2 * b)]
            m = [_dot(jnp.concatenate([tb[u][rs] for rs in lower], axis=0), a_b[u] * off) for u in n_units]
            upd = [_bf(_dot(_bf(m[u]), tb[u])) for u in n_units]
            tb = [jnp.concatenate(
                [piece for n, rs in enumerate(lower)
                 for piece in (tb[u][rs.start - b:rs.start], tb[u][rs] - upd[u][n * b:(n + 1) * b])], axis=0)
                 for u in n_units]
        else:
            m = [_dot(tb[u], a_b[u] * off) for u in n_units]
            tb = [tb[u] - _bf(_dot(_bf(m[u]), tb[u])) for u in n_units]
        b *= 2
    r = [jnp.concatenate([qkv_ref[rows(j), _head_cols(2 * GDN_WIDTH, hd)].astype(F32) * colv(betas, u),
                          kb[u] * colv(e_dec, u)], axis=1) for u, (j, hd) in enumerate(units)]
    uw = [_dot(tb[u], _bf(r[u])) for u in n_units]
    lhs = [jnp.concatenate([_bf(uw[u][:, HEAD_DIM:2 * HEAD_DIM]),
                            _bf(q[u].astype(F32) * colv(e_dec, u))], axis=0) for u in n_units]
    kd = [_bf(kf[u] * colv(e_rem, u)) for u in n_units]
    s = [state_ref[hd] for hd in range(GDN_HEADS)]
    for j in range(n_chunks):
        us = [j * GDN_HEADS + hd for hd in range(GDN_HEADS)]
        ws_qs = [_dot(lhs[u], _bf(s[hd])) for hd, u in enumerate(us)]
        v_new = [_bf(uw[u][:, 0:HEAD_DIM] - ws_qs[hd][0:c]) for hd, u in enumerate(us)]
        o = [ws_qs[hd][c:2 * c] + _dot(qk[u], v_new[hd]) for hd, u in enumerate(us)]
        s = [s[hd] * colv(e_last, u) + _dot_tn(kd[u], v_new[hd]) for hd, u in enumerate(us)]
        for hd in range(GDN_HEADS):
            on = o[hd] * lax.rsqrt(jnp.mean(o[hd] * o[hd], axis=-1, keepdims=True) + EPS) * gnw_ref[...]
            yb_ref[rows(j), _head_cols(0, hd)] = (
                on * zb_ref[rows(j), _head_cols(0, hd)].astype(F32)).astype(yb_ref.dtype)
    for hd in range(GDN_HEADS):
        state_ref[hd] = s[hd]


def _mix_kernel(tiles_per_seq,
                h_ref, nw_ref, w_ref, wab_ref, wabt_ref, lng_ref, lnb_ref, wsp_ref, bspt_ref,
                cw_ref, alog_c_ref, dtb_c_ref, alog_r_ref, dtb_r_ref, gnw_ref,
                ya_ref, yb_ref,
                cs_ref, state_ref, qkv_ref, zb_ref, col_ref, row_ref):
    @pl.when(pl.program_id(0) % tiles_per_seq == 0)
    def _():
        cs_ref[...] = jnp.zeros(cs_ref.shape, F32)
        state_ref[...] = jnp.zeros(state_ref.shape, F32)

    xn = _bf(_rms(h_ref[...], nw_ref[...]))
    _gmlp_branch(xn, w_ref, lng_ref, lnb_ref, wsp_ref, bspt_ref, ya_ref)
    _gdn_prepare(xn, w_ref, wab_ref, wabt_ref, cw_ref, alog_c_ref, dtb_c_ref, alog_r_ref, dtb_r_ref,
                 cs_ref, qkv_ref, zb_ref, col_ref, row_ref)
    _gdn_chunks(qkv_ref, zb_ref, col_ref, row_ref, gnw_ref, state_ref, yb_ref)


def _layer_spec(layer, shape):
    return pl.BlockSpec((None,) + shape, lambda i: (layer,) + (0,) * len(shape),
                        pipeline_mode=pl.Buffered(1))


def _mix(layer, h, nw, w_main, wab, wabt, lng, lnb, wsp, bspt, cw, alog_c, dtb_c, alog_r, dtb_r, gnw, seq):
    t = h.shape[0]
    tm = TILE_MIX
    spec = functools.partial(_layer_spec, layer)
    return pl.pallas_call(
        functools.partial(_mix_kernel, seq // tm),
        out_shape=(
            jax.ShapeDtypeStruct((t, GMLP_WIDTH), BF16),
            jax.ShapeDtypeStruct((t, GDN_WIDTH), BF16),
        ),
        grid=(t // tm,),
        in_specs=[
            pl.BlockSpec((tm, D_MODEL), lambda i: (i, 0)),
            spec((1, D_MODEL)),
            spec((D_MODEL, MAIN_COLS)),
            spec((D_MODEL, 2 * GDN_HEADS)),
            spec((GDN_HEADS, D_MODEL)),
            spec((1, GMLP_WIDTH)),
            spec((1, GMLP_WIDTH)),
            spec((GMLP_GROUPS, SPATIAL_BLOCK, SPATIAL_BLOCK)),
            spec((SPATIAL_BLOCK, GMLP_GROUPS)),
            spec((CONV_K, 3 * GDN_WIDTH)),
            spec((1, GDN_HEADS)),
            spec((1, GDN_HEADS)),
            spec((GDN_HEADS, 1)),
            spec((GDN_HEADS, 1)),
            spec((1, HEAD_DIM)),
        ],
        out_specs=(
            pl.BlockSpec((tm, GMLP_WIDTH), lambda i: (i, 0)),
            pl.BlockSpec((tm, GDN_WIDTH), lambda i: (i, 0)),
        ),
        scratch_shapes=[
            pltpu.VMEM((CONV_PAD, 3 * GDN_WIDTH), F32),
            pltpu.VMEM((GDN_HEADS, HEAD_DIM, HEAD_DIM), F32),
            pltpu.VMEM((tm, 3 * GDN_WIDTH), BF16),
            pltpu.VMEM((tm, GDN_WIDTH), BF16),
            pltpu.VMEM((tm, 2 * GDN_HEADS), F32),
            pltpu.VMEM((GDN_HEADS, tm), F32),
        ],
        compiler_params=pltpu.CompilerParams(
            dimension_semantics=("arbitrary",), vmem_limit_bytes=VMEM_LIMIT_BYTES),
        name="mix",
    )(h, nw, w_main, wab, wabt, lng, lnb, wsp, bspt, cw, alog_c, dtb_c, alog_r, dtb_r, gnw)


def _out_ple_kernel(apply_final_norm,
                    h_ref, ya_ref, yb_ref, p_ref, wout_ref, wple_ref, pnw_ref, gnw_ref, wg_ref, fnw_ref,
                    o_ref):
    h1 = (h_ref[...] + _dot(ya_ref[...], wout_ref[0:GMLP_WIDTH, :])
          + _dot(yb_ref[...], wout_ref[GMLP_WIDTH:GMLP_WIDTH + GDN_WIDTH, :]))
    e = _rms(_dot(_bf(p_ref[...]), wple_ref[...]), pnw_ref[...])
    gate = jax.nn.sigmoid(_dot(_bf(_rms(h1, gnw_ref[...])), wg_ref[...]))
    h2 = h1 + gate * e
    if apply_final_norm:
        h2 = _rms(h2, fnw_ref[...])
    o_ref[...] = h2


def _out_ple(layer, h, ya, yb, p, wout, wple, pnw, gnw, wg, fnw, apply_final_norm):
    t = h.shape[0]
    tm = TILE_OUT
    spec = functools.partial(_layer_spec, layer)
    tok = lambda width: pl.BlockSpec((tm, width), lambda i: (i, 0))
    return pl.pallas_call(
        functools.partial(_out_ple_kernel, apply_final_norm),
        out_shape=jax.ShapeDtypeStruct((t, D_MODEL), F32),
        grid=(t // tm,),
        in_specs=[
            tok(D_MODEL), tok(GMLP_WIDTH), tok(GDN_WIDTH),
            pl.BlockSpec((None, tm, PLE_DIM), lambda i: (layer, i, 0)),
            spec((GMLP_WIDTH + GDN_WIDTH, D_MODEL)),
            spec((PLE_DIM, D_MODEL)),
            spec((1, D_MODEL)),
            spec((1, D_MODEL)),
            spec((D_MODEL, D_MODEL)),
            pl.BlockSpec((1, D_MODEL), lambda i: (0, 0)),
        ],
        out_specs=tok(D_MODEL),
        compiler_params=pltpu.CompilerParams(
            dimension_semantics=("arbitrary",), vmem_limit_bytes=VMEM_LIMIT_BYTES),
        name="out_ple",
    )(h, ya, yb, p, wout, wple, pnw, gnw, wg, fnw)


def kernel(x, p, norm_w, w_in, ln_v_g, ln_v_b, w_spatial, b_spatial, conv_w, A_log, dt_bias,
           gdn_norm_w, w_out, w_ple, ple_norm_w, ple_gate_norm_w, w_ple_gate, final_norm_w):
    batch, seq, d = x.shape
    depth = w_in.shape[0]
    assert d == D_MODEL and seq % TILE_MIX == 0 and (batch * seq) % TILE_OUT == 0
    assert w_in.shape[2] == MAIN_COLS + 2 * GDN_HEADS
    t = batch * seq
    h = x.reshape(t, d)
    p2 = p.reshape(depth, t, PLE_DIM)
    w_main = _bf(w_in[:, :, 0:MAIN_COLS])
    wab = _bf(w_in[:, :, MAIN_COLS:])
    wabt = jnp.swapaxes(wab[:, :, 0:GDN_HEADS], 1, 2)
    wsp_b, wout_b, wple_b, wg_b = _bf(w_spatial), _bf(w_out), _bf(w_ple), _bf(w_ple_gate)
    bspt = jnp.swapaxes(b_spatial, 1, 2)
    row3 = lambda a: a[:, None, :]
    col3 = lambda a: a[:, :, None]
    for i in range(depth):
        ya, yb = _mix(i, h, row3(norm_w), w_main, wab, wabt, row3(ln_v_g), row3(ln_v_b), wsp_b, bspt,
                      conv_w, row3(A_log), row3(dt_bias), col3(A_log), col3(dt_bias), row3(gdn_norm_w), seq)
        h = _out_ple(i, h, ya, yb, p2, wout_b, wple_b, row3(ple_norm_w), row3(ple_gate_norm_w), wg_b,
                     final_norm_w[None], apply_final_norm=(i == depth - 1))
    return h.reshape(batch, seq, d)
```

```python
import functools

import jax
import jax.numpy as jnp
from jax import lax
from jax.experimental import pallas as pl
from jax.experimental.pallas import tpu as pltpu

D_MODEL = 1024
GMLP_WIDTH = 1024
GMLP_GROUPS = 8
GROUP_DIM = 128
SPATIAL_BLOCK = 128
SPATIAL_CHUNK = 64
GDN_WIDTH = 1024
GDN_HEADS = 8
HEAD_DIM = 128
CONV_K = 4
PLE_DIM = 256
EPS = 1e-6
MAIN_COLS = 3 * GMLP_WIDTH + 4 * GDN_WIDTH

GDN_CHUNK = 128
TILE_MIX = 256
TILE_OUT = 512
STRIP_COLS = 256
BF16_ROWS = 16
CONV_PAD = 8
VMEM_LIMIT_BYTES = 56 * 1024 * 1024

F32 = jnp.float32
BF16 = jnp.bfloat16
LOG2E = 1.4426950408889634
GELU_K1 = -2.0 * 0.7978845608028654 * LOG2E
GELU_K3 = GELU_K1 * 0.044715


def _bf(x):
    return x.astype(BF16)


def _dot(a, b):
    return jnp.dot(a, b, preferred_element_type=F32)


def _dot_nt(a, b):
    return lax.dot_general(a, b, (((1,), (1,)), ((), ())), preferred_element_type=F32)


def _dot_tn(a, b):
    return lax.dot_general(a, b, (((0,), (0,)), ((), ())), preferred_element_type=F32)


def _rms(x, w):
    return x * lax.rsqrt(jnp.mean(x * x, axis=-1, keepdims=True) + EPS) * w


def _silu(x):
    return x / (1.0 + jnp.exp2(x * -LOG2E))


def _gelu_tanh(x):
    return x / (1.0 + jnp.exp2(x * (GELU_K1 + GELU_K3 * (x * x))))


def _softplus(x):
    return jnp.maximum(x, 0.0) + jnp.log1p(jnp.exp(-jnp.abs(x)))


def _head_cols(base, hd):
    return slice(base + hd * HEAD_DIM, base + (hd + 1) * HEAD_DIM)


def _gmlp_branch(xn, w_ref, lng_ref, lnb_ref, wsp_ref, bspt_ref, ya_ref):
    tm = xn.shape[0]
    strips = range(0, GMLP_WIDTH, STRIP_COLS)

    def proj(base, c0):
        return _dot_nt(xn, w_ref[base + c0:base + c0 + STRIP_COLS, :])

    v = jnp.concatenate([_gelu_tanh(proj(GMLP_WIDTH, c0)) for c0 in strips], axis=1)
    mu = jnp.mean(v, axis=-1, keepdims=True)
    vc = v - mu
    var = jnp.mean(vc * vc, axis=-1, keepdims=True)
    vln = _bf(vc * lax.rsqrt(var + EPS) * lng_ref[...] + lnb_ref[...])
    uz = jnp.concatenate([_gelu_tanh(proj(0, c0)) * _silu(proj(2 * GMLP_WIDTH, c0)) for c0 in strips], axis=1)
    ri = lax.broadcasted_iota(jnp.int32, (SPATIAL_BLOCK, SPATIAL_BLOCK), 0) // SPATIAL_CHUNK
    ci = lax.broadcasted_iota(jnp.int32, (SPATIAL_BLOCK, SPATIAL_BLOCK), 1) // SPATIAL_CHUNK
    chunk_causal = ci <= ri
    for g in range(GMLP_GROUPS):
        gs = slice(g * GROUP_DIM, (g + 1) * GROUP_DIM)
        wm = jnp.where(chunk_causal, wsp_ref[g], jnp.zeros_like(wsp_ref[g]))
        bias = bspt_ref[:, g:g + 1]
        for r in range(tm // SPATIAL_BLOCK):
            rs = slice(r * SPATIAL_BLOCK, (r + 1) * SPATIAL_BLOCK)
            mixed = _dot(wm, vln[rs, gs]) + bias
            ya_ref[rs, gs] = (uz[rs, gs] * mixed).astype(ya_ref.dtype)


def _gdn_prepare(xn, w_ref, wab_ref, wabt_ref, cw_ref, alog_c_ref, dtb_c_ref, alog_r_ref, dtb_r_ref,
                 cs_ref, qkv_ref, zb_ref, col_ref, row_ref):
    tm = xn.shape[0]
    sub = lax.broadcasted_iota(jnp.int32, (CONV_PAD, STRIP_COLS), 0)
    for c0 in range(0, 3 * GDN_WIDTH, STRIP_COLS):
        cols = slice(c0, c0 + STRIP_COLS)
        x = _dot_nt(xn, w_ref[3 * GMLP_WIDTH + c0:3 * GMLP_WIDTH + c0 + STRIP_COLS, :])
        prev = cs_ref[:, cols]
        acc = x * cw_ref[CONV_K - 1:CONV_K, cols]
        for shift in range(1, CONV_K):
            xs = pltpu.roll(x, shift, axis=0)
            head = jnp.where(sub < shift, pltpu.roll(prev, shift, axis=0), xs[0:CONV_PAD])
            shifted = jnp.concatenate([head, xs[CONV_PAD:]], axis=0)
            acc = acc + shifted * cw_ref[CONV_K - 1 - shift:CONV_K - shift, cols]
        cs_ref[:, cols] = x[tm - CONV_PAD:tm]
        act = _silu(acc)
        if c0 >= 2 * GDN_WIDTH:
            qkv_ref[:, cols] = _bf(act)
        else:
            scale = HEAD_DIM ** -0.5 if c0 < GDN_WIDTH else 1.0
            for hd in range(STRIP_COLS // HEAD_DIM):
                t = act[:, _head_cols(0, hd)]
                t = t * (lax.rsqrt(jnp.sum(t * t, axis=-1, keepdims=True) + EPS) * scale)
                qkv_ref[:, _head_cols(c0, hd)] = _bf(t)

    for c0 in range(0, GDN_WIDTH, STRIP_COLS):
        zb_ref[:, c0:c0 + STRIP_COLS] = _bf(_silu(_dot_nt(
            xn, w_ref[3 * GMLP_WIDTH + 3 * GDN_WIDTH + c0:3 * GMLP_WIDTH + 3 * GDN_WIDTH + c0 + STRIP_COLS, :])))

    ab = _dot(xn, wab_ref[...])
    g_col = -jnp.exp(alog_c_ref[...]) * _softplus(ab[:, 0:GDN_HEADS] + dtb_c_ref[...])
    beta = jax.nn.sigmoid(ab[:, GDN_HEADS:2 * GDN_HEADS])
    a_row = _dot_nt(wabt_ref[...], xn)
    g_row = -jnp.exp(alog_r_ref[...]) * _softplus(a_row + dtb_r_ref[...])
    ii = lax.broadcasted_iota(jnp.int32, (GDN_CHUNK, GDN_CHUNK), 0)
    jj = lax.broadcasted_iota(jnp.int32, (GDN_CHUNK, GDN_CHUNK), 1)
    tri_lower = (jj <= ii).astype(F32)
    tri_upper = (ii <= jj).astype(F32)
    for r in range(tm // GDN_CHUNK):
        rs = slice(r * GDN_CHUNK, (r + 1) * GDN_CHUNK)
        col_ref[rs, 0:GDN_HEADS] = jnp.dot(tri_lower, g_col[rs, :], precision=lax.Precision.HIGHEST,
                                           preferred_element_type=F32)
        col_ref[rs, GDN_HEADS:2 * GDN_HEADS] = beta[rs, :]
        row_ref[:, rs] = jnp.dot(g_row[:, rs], tri_upper, precision=lax.Precision.HIGHEST,
                                 preferred_element_type=F32)


def _gdn_chunks(qkv_ref, zb_ref, col_ref, row_ref, gnw_ref, state_ref, yb_ref):
    c = GDN_CHUNK
    n_chunks = qkv_ref.shape[0] // c
    ii = lax.broadcasted_iota(jnp.int32, (c, c), 0)
    jj = lax.broadcasted_iota(jnp.int32, (c, c), 1)
    causal = jj <= ii
    strict = jj < ii
    eye = (ii == jj).astype(F32)
    units = [(j, hd) for j in range(n_chunks) for hd in range(GDN_HEADS)]
    n_units = range(len(units))

    def rows(j):
        return slice(j * c, (j + 1) * c)

    dcols, betas, e_dec, e_rem, e_last = [], [], [], [], []
    for j in range(n_chunks):
        col = col_ref[rows(j), :]
        d = col[:, 0:GDN_HEADS]
        dlast = d[c - 1:c, :]
        dcols.append(d)
        betas.append(col[:, GDN_HEADS:2 * GDN_HEADS])
        e_dec.append(jnp.exp(d))
        e_rem.append(jnp.exp(dlast - d))
        e_last.append(jnp.exp(dlast))

    def colv(per_chunk, u):
        j, hd = units[u]
        return per_chunk[j][:, hd:hd + 1]

    q = [qkv_ref[rows(j), _head_cols(0, hd)] for j, hd in units]
    k = [qkv_ref[rows(j), _head_cols(GDN_WIDTH, hd)] for j, hd in units]
    kf = [k[u].astype(F32) for u in n_units]
    kb = [kf[u] * colv(betas, u) for u in n_units]
    gram = [_dot_nt(jnp.concatenate([_bf(kb[u]), q[u]], axis=0), k[u]) for u in n_units]
    lmat = [jnp.exp(jnp.where(causal, colv(dcols, u) - row_ref[hd:hd + 1, rows(j)], -jnp.inf))
            for u, (j, hd) in enumerate(units)]
    a = [jnp.where(strict, gram[u][0:c] * lmat[u], 0.0) for u in n_units]
    qk = [_bf(gram[u][c:2 * c] * lmat[u]) for u in n_units]
    a_b = [_bf(a[u]) for u in n_units]
    tb = [_bf(eye) - jnp.where((ii // 2) == (jj // 2), a_b[u], jnp.zeros_like(a_b[u])) for u in n_units]
    b = 2
    while b < c:
        off = _bf((((ii // (2 * b)) == (jj // (2 * b))) & ((ii // b) != (jj // b))).astype(F32))
        if b % BF16_ROWS == 0:
            lower = [slice(s0 + b, s0 + 2 * b) for s0 in range(0, c, 2 * b)]
            m = [_dot(jnp.concatenate([tb[u][rs] for rs in lower], axis=0), a_b[u] * off) for u in n_units]
            upd = [_bf(_dot(_bf(m[u]), tb[u])) for u in n_units]
            tb = [jnp.concatenate(
                [piece for n, rs in enumerate(lower)
                 for piece in (tb[u][rs.start - b:rs.start], tb[u][rs] - upd[u][n * b:(n + 1) * b])], axis=0)
                 for u in n_units]
        else:
            m = [_dot(tb[u], a_b[u] * off) for u in n_units]
            tb = [tb[u] - _bf(_dot(_bf(m[u]), tb[u])) for u in n_units]
        b *= 2
    r = [jnp.concatenate([qkv_ref[rows(j), _head_cols(2 * GDN_WIDTH, hd)].astype(F32) * colv(betas, u),
                          kb[u] * colv(e_dec, u)], axis=1) for u, (j, hd) in enumerate(units)]
    uw = [_dot(tb[u], _bf(r[u])) for u in n_units]
    lhs = [jnp.concatenate([_bf(uw[u][:, HEAD_DIM:2 * HEAD_DIM]),
                            _bf(q[u].astype(F32) * colv(e_dec, u))], axis=0) for u in n_units]
    kd = [_bf(kf[u] * colv(e_rem, u)) for u in n_units]
    s = [state_ref[hd] for hd in range(GDN_HEADS)]
    for j in range(n_chunks):
        us = [j * GDN_HEADS + hd for hd in range(GDN_HEADS)]
        ws_qs = [_dot(lhs[u], _bf(s[hd])) for hd, u in enumerate(us)]
        v_new = [_bf(uw[u][:, 0:HEAD_DIM] - ws_qs[hd][0:c]) for hd, u in enumerate(us)]
        o = [ws_qs[hd][c:2 * c] + _dot(qk[u], v_new[hd]) for hd, u in enumerate(us)]
        s = [s[hd] * colv(e_last, u) + _dot_tn(kd[u], v_new[hd]) for hd, u in enumerate(us)]
        for hd in range(GDN_HEADS):
            on = o[hd] * lax.rsqrt(jnp.mean(o[hd] * o[hd], axis=-1, keepdims=True) + EPS) * gnw_ref[...]
            yb_ref[rows(j), _head_cols(0, hd)] = (
                on * zb_ref[rows(j), _head_cols(0, hd)].astype(F32)).astype(yb_ref.dtype)
    for hd in range(GDN_HEADS):
        state_ref[hd] = s[hd]


def _mix_kernel(tiles_per_seq,
                h_ref, nw_ref, w_ref, wab_ref, wabt_ref, lng_ref, lnb_ref, wsp_ref, bspt_ref,
                cw_ref, alog_c_ref, dtb_c_ref, alog_r_ref, dtb_r_ref, gnw_ref,
                ya_ref, yb_ref,
                cs_ref, state_ref, qkv_ref, zb_ref, col_ref, row_ref):
    @pl.when(pl.program_id(0) % tiles_per_seq == 0)
    def _():
        cs_ref[...] = jnp.zeros(cs_ref.shape, F32)
        state_ref[...] = jnp.zeros(state_ref.shape, F32)

    xn = _bf(_rms(h_ref[...], nw_ref[...]))
    _gmlp_branch(xn, w_ref, lng_ref, lnb_ref, wsp_ref, bspt_ref, ya_ref)
    _gdn_prepare(xn, w_ref, wab_ref, wabt_ref, cw_ref, alog_c_ref, dtb_c_ref, alog_r_ref, dtb_r_ref,
                 cs_ref, qkv_ref, zb_ref, col_ref, row_ref)
    _gdn_chunks(qkv_ref, zb_ref, col_ref, row_ref, gnw_ref, state_ref, yb_ref)


def _layer_spec(layer, shape):
    return pl.BlockSpec((None,) + shape, lambda i: (layer,) + (0,) * len(shape),
                        pipeline_mode=pl.Buffered(1))


def _mix(layer, h, nw, w_main, wab, wabt, lng, lnb, wsp, bspt, cw, alog_c, dtb_c, alog_r, dtb_r, gnw, seq):
    t = h.shape[0]
    tm = TILE_MIX
    spec = functools.partial(_layer_spec, layer)
    return pl.pallas_call(
        functools.partial(_mix_kernel, seq // tm),
        out_shape=(
            jax.ShapeDtypeStruct((t, GMLP_WIDTH), BF16),
            jax.ShapeDtypeStruct((t, GDN_WIDTH), BF16),
        ),
        grid=(t // tm,),
        in_specs=[
            pl.BlockSpec((tm, D_MODEL), lambda i: (i, 0)),
            spec((1, D_MODEL)),
            spec((MAIN_COLS, D_MODEL)),
            spec((D_MODEL, 2 * GDN_HEADS)),
            spec((GDN_HEADS, D_MODEL)),
            spec((1, GMLP_WIDTH)),
            spec((1, GMLP_WIDTH)),
            spec((GMLP_GROUPS, SPATIAL_BLOCK, SPATIAL_BLOCK)),
            spec((SPATIAL_BLOCK, GMLP_GROUPS)),
            spec((CONV_K, 3 * GDN_WIDTH)),
            spec((1, GDN_HEADS)),
            spec((1, GDN_HEADS)),
            spec((GDN_HEADS, 1)),
            spec((GDN_HEADS, 1)),
            spec((1, HEAD_DIM)),
        ],
        out_specs=(
            pl.BlockSpec((tm, GMLP_WIDTH), lambda i: (i, 0)),
            pl.BlockSpec((tm, GDN_WIDTH), lambda i: (i, 0)),
        ),
        scratch_shapes=[
            pltpu.VMEM((CONV_PAD, 3 * GDN_WIDTH), F32),
            pltpu.VMEM((GDN_HEADS, HEAD_DIM, HEAD_DIM), F32),
            pltpu.VMEM((tm, 3 * GDN_WIDTH), BF16),
            pltpu.VMEM((tm, GDN_WIDTH), BF16),
            pltpu.VMEM((tm, 2 * GDN_HEADS), F32),
            pltpu.VMEM((GDN_HEADS, tm), F32),
        ],
        compiler_params=pltpu.CompilerParams(
            dimension_semantics=("arbitrary",), vmem_limit_bytes=VMEM_LIMIT_BYTES),
        name="mix",
    )(h, nw, w_main, wab, wabt, lng, lnb, wsp, bspt, cw, alog_c, dtb_c, alog_r, dtb_r, gnw)


def _out_ple_kernel(apply_final_norm,
                    h_ref, ya_ref, yb_ref, p_ref, wout_ref, wple_ref, pnw_ref, gnw_ref, wg_ref, fnw_ref,
                    o_ref):
    h1 = (h_ref[...] + _dot(ya_ref[...], wout_ref[0:GMLP_WIDTH, :])
          + _dot(yb_ref[...], wout_ref[GMLP_WIDTH:GMLP_WIDTH + GDN_WIDTH, :]))
    e = _rms(_dot(_bf(p_ref[...]), wple_ref[...]), pnw_ref[...])
    gate = jax.nn.sigmoid(_dot(_bf(_rms(h1, gnw_ref[...])), wg_ref[...]))
    h2 = h1 + gate * e
    if apply_final_norm:
        h2 = _rms(h2, fnw_ref[...])
    o_ref[...] = h2


def _out_ple(layer, h, ya, yb, p, wout, wple, pnw, gnw, wg, fnw, apply_final_norm):
    t = h.shape[0]
    tm = TILE_OUT
    spec = functools.partial(_layer_spec, layer)
    tok = lambda width: pl.BlockSpec((tm, width), lambda i: (i, 0))
    return pl.pallas_call(
        functools.partial(_out_ple_kernel, apply_final_norm),
        out_shape=jax.ShapeDtypeStruct((t, D_MODEL), F32),
        grid=(t // tm,),
        in_specs=[
            tok(D_MODEL), tok(GMLP_WIDTH), tok(GDN_WIDTH),
            pl.BlockSpec((None, tm, PLE_DIM), lambda i: (layer, i, 0)),
            spec((GMLP_WIDTH + GDN_WIDTH, D_MODEL)),
            spec((PLE_DIM, D_MODEL)),
            spec((1, D_MODEL)),
            spec((1, D_MODEL)),
            spec((D_MODEL, D_MODEL)),
            pl.BlockSpec((1, D_MODEL), lambda i: (0, 0)),
        ],
        out_specs=tok(D_MODEL),
        compiler_params=pltpu.CompilerParams(
            dimension_semantics=("arbitrary",), vmem_limit_bytes=VMEM_LIMIT_BYTES),
        name="out_ple",
    )(h, ya, yb, p, wout, wple, pnw, gnw, wg, fnw)


def kernel(x, p, norm_w, w_in, ln_v_g, ln_v_b, w_spatial, b_spatial, conv_w, A_log, dt_bias,
           gdn_norm_w, w_out, w_ple, ple_norm_w, ple_gate_norm_w, w_ple_gate, final_norm_w):
    batch, seq, d = x.shape
    depth = w_in.shape[0]
    assert d == D_MODEL and seq % TILE_MIX == 0 and (batch * seq) % TILE_OUT == 0
    assert w_in.shape[2] == MAIN_COLS + 2 * GDN_HEADS
    t = batch * seq
    h = x.reshape(t, d)
    p2 = p.reshape(depth, t, PLE_DIM)
    w_in_t = jnp.swapaxes(w_in, 1, 2)
    w_main = _bf(w_in_t[:, 0:MAIN_COLS, :])
    wabt = _bf(w_in_t[:, MAIN_COLS:MAIN_COLS + GDN_HEADS, :])
    wab = _bf(w_in[:, :, MAIN_COLS:])
    wsp_b, wout_b, wple_b, wg_b = _bf(w_spatial), _bf(w_out), _bf(w_ple), _bf(w_ple_gate)
    bspt = jnp.swapaxes(b_spatial, 1, 2)
    row3 = lambda a: a[:, None, :]
    col3 = lambda a: a[:, :, None]
    for i in range(depth):
        ya, yb = _mix(i, h, row3(norm_w), w_main, wab, wabt, row3(ln_v_g), row3(ln_v_b), wsp_b, bspt,
                      conv_w, row3(A_log), row3(dt_bias), col3(A_log), col3(dt_bias), row3(gdn_norm_w), seq)
        h = _out_ple(i, h, ya, yb, p2, wout_b, wple_b, row3(ple_norm_w), row3(ple_gate_norm_w), wg_b,
                     final_norm_w[None], apply_final_norm=(i == depth - 1))
    return h.reshape(batch, seq, d)
```

```python
import functools

import jax
import jax.numpy as jnp
from jax import lax
from jax.experimental import pallas as pl
from jax.experimental.pallas import tpu as pltpu

D_MODEL = 1024
GMLP_WIDTH = 1024
GMLP_GROUPS = 8
GROUP_DIM = 128
SPATIAL_BLOCK = 128
SPATIAL_CHUNK = 64
GDN_WIDTH = 1024
GDN_HEADS = 8
HEAD_DIM = 128
CONV_K = 4
PLE_DIM = 256
EPS = 1e-6
MAIN_COLS = 3 * GMLP_WIDTH + 4 * GDN_WIDTH

GDN_CHUNK = 128
TILE_MIX = 256
TILE_OUT = 1024
STRIP_COLS = 256
BF16_ROWS = 16
CONV_PAD = 8
VMEM_LIMIT_BYTES = 56 * 1024 * 1024

F32 = jnp.float32
BF16 = jnp.bfloat16
LOG2E = 1.4426950408889634
GELU_K1 = -2.0 * 0.7978845608028654 * LOG2E
GELU_K3 = GELU_K1 * 0.044715


def _bf(x):
    return x.astype(BF16)


def _dot(a, b):
    return jnp.dot(a, b, preferred_element_type=F32)


def _dot_nt(a, b):
    return lax.dot_general(a, b, (((1,), (1,)), ((), ())), preferred_element_type=F32)


def _dot_tn(a, b):
    return lax.dot_general(a, b, (((0,), (0,)), ((), ())), preferred_element_type=F32)


def _rms(x, w):
    return x * lax.rsqrt(jnp.mean(x * x, axis=-1, keepdims=True) + EPS) * w


def _silu(x):
    return x / (1.0 + jnp.exp2(x * -LOG2E))


def _gelu_tanh(x):
    return x / (1.0 + jnp.exp2(x * (GELU_K1 + GELU_K3 * (x * x))))


def _softplus(x):
    return jnp.maximum(x, 0.0) + jnp.log1p(jnp.exp(-jnp.abs(x)))


def _head_cols(base, hd):
    return slice(base + hd * HEAD_DIM, base + (hd + 1) * HEAD_DIM)


def _gmlp_branch(xn, w_ref, lng_ref, lnb_ref, wsp_ref, bspt_ref, ya_ref):
    tm = xn.shape[0]
    strips = range(0, GMLP_WIDTH, STRIP_COLS)

    def proj(base, c0):
        return _dot_nt(xn, w_ref[base + c0:base + c0 + STRIP_COLS, :])

    v = jnp.concatenate([_gelu_tanh(proj(GMLP_WIDTH, c0)) for c0 in strips], axis=1)
    mu = jnp.mean(v, axis=-1, keepdims=True)
    vc = v - mu
    var = jnp.mean(vc * vc, axis=-1, keepdims=True)
    vln = _bf(vc * lax.rsqrt(var + EPS) * lng_ref[...] + lnb_ref[...])
    uz = jnp.concatenate([_gelu_tanh(proj(0, c0)) * _silu(proj(2 * GMLP_WIDTH, c0)) for c0 in strips], axis=1)
    ri = lax.broadcasted_iota(jnp.int32, (SPATIAL_BLOCK, SPATIAL_BLOCK), 0) // SPATIAL_CHUNK
    ci = lax.broadcasted_iota(jnp.int32, (SPATIAL_BLOCK, SPATIAL_BLOCK), 1) // SPATIAL_CHUNK
    chunk_causal = ci <= ri
    for g in range(GMLP_GROUPS):
        gs = slice(g * GROUP_DIM, (g + 1) * GROUP_DIM)
        wm = jnp.where(chunk_causal, wsp_ref[g], jnp.zeros_like(wsp_ref[g]))
        bias = bspt_ref[:, g:g + 1]
        for r in range(tm // SPATIAL_BLOCK):
            rs = slice(r * SPATIAL_BLOCK, (r + 1) * SPATIAL_BLOCK)
            mixed = _dot(wm, vln[rs, gs]) + bias
            ya_ref[rs, gs] = (uz[rs, gs] * mixed).astype(ya_ref.dtype)


def _gdn_prepare(xn, w_ref, wab_ref, cw_ref, alog_c_ref, dtb_c_ref, alog_r_ref, dtb_r_ref,
                 cs_ref, qkv_ref, zb_ref, col_ref, row_ref):
    tm = xn.shape[0]
    sub = lax.broadcasted_iota(jnp.int32, (CONV_PAD, STRIP_COLS), 0)
    for c0 in range(0, 3 * GDN_WIDTH, STRIP_COLS):
        cols = slice(c0, c0 + STRIP_COLS)
        x = _dot_nt(xn, w_ref[3 * GMLP_WIDTH + c0:3 * GMLP_WIDTH + c0 + STRIP_COLS, :])
        prev = cs_ref[:, cols]
        acc = x * cw_ref[CONV_K - 1:CONV_K, cols]
        for shift in range(1, CONV_K):
            xs = pltpu.roll(x, shift, axis=0)
            head = jnp.where(sub < shift, pltpu.roll(prev, shift, axis=0), xs[0:CONV_PAD])
            shifted = jnp.concatenate([head, xs[CONV_PAD:]], axis=0)
            acc = acc + shifted * cw_ref[CONV_K - 1 - shift:CONV_K - shift, cols]
        cs_ref[:, cols] = x[tm - CONV_PAD:tm]
        act = _silu(acc)
        if c0 >= 2 * GDN_WIDTH:
            qkv_ref[:, cols] = _bf(act)
        else:
            scale = HEAD_DIM ** -0.5 if c0 < GDN_WIDTH else 1.0
            for hd in range(STRIP_COLS // HEAD_DIM):
                t = act[:, _head_cols(0, hd)]
                t = t * (lax.rsqrt(jnp.sum(t * t, axis=-1, keepdims=True) + EPS) * scale)
                qkv_ref[:, _head_cols(c0, hd)] = _bf(t)

    for c0 in range(0, GDN_WIDTH, STRIP_COLS):
        zb_ref[:, c0:c0 + STRIP_COLS] = _bf(_silu(_dot_nt(
            xn, w_ref[3 * GMLP_WIDTH + 3 * GDN_WIDTH + c0:3 * GMLP_WIDTH + 3 * GDN_WIDTH + c0 + STRIP_COLS, :])))

    ab = _dot_nt(xn, wab_ref[...])
    g_col = -jnp.exp(alog_c_ref[...]) * _softplus(ab[:, 0:GDN_HEADS] + dtb_c_ref[...])
    beta = jax.nn.sigmoid(ab[:, GDN_HEADS:2 * GDN_HEADS])
    a_row = _dot_nt(wab_ref[...], xn)[0:GDN_HEADS]
    g_row = -jnp.exp(alog_r_ref[...]) * _softplus(a_row + dtb_r_ref[...])
    ii = lax.broadcasted_iota(jnp.int32, (GDN_CHUNK, GDN_CHUNK), 0)
    jj = lax.broadcasted_iota(jnp.int32, (GDN_CHUNK, GDN_CHUNK), 1)
    tri_lower = (jj <= ii).astype(F32)
    tri_upper = (ii <= jj).astype(F32)
    for r in range(tm // GDN_CHUNK):
        rs = slice(r * GDN_CHUNK, (r + 1) * GDN_CHUNK)
        col_ref[rs, 0:GDN_HEADS] = jnp.dot(tri_lower, g_col[rs, :], precision=lax.Precision.HIGHEST,
                                           preferred_element_type=F32)
        col_ref[rs, GDN_HEADS:2 * GDN_HEADS] = beta[rs, :]
        row_ref[:, rs] = jnp.dot(g_row[:, rs], tri_upper, precision=lax.Precision.HIGHEST,
                                 preferred_element_type=F32)


def _gdn_chunks(qkv_ref, zb_ref, col_ref, row_ref, gnw_ref, state_ref, yb_ref):
    c = GDN_CHUNK
    n_chunks = qkv_ref.shape[0] // c
    ii = lax.broadcasted_iota(jnp.int32, (c, c), 0)
    jj = lax.broadcasted_iota(jnp.int32, (c, c), 1)
    causal = jj <= ii
    strict = jj < ii
    eye = (ii == jj).astype(F32)
    units = [(j, hd) for j in range(n_chunks) for hd in range(GDN_HEADS)]
    n_units = range(len(units))

    def rows(j):
        return slice(j * c, (j + 1) * c)

    dcols, betas, e_dec, e_rem, e_last = [], [], [], [], []
    for j in range(n_chunks):
        col = col_ref[rows(j), :]
        d = col[:, 0:GDN_HEADS]
        dlast = d[c - 1:c, :]
        dcols.append(d)
        betas.append(col[:, GDN_HEADS:2 * GDN_HEADS])
        e_dec.append(jnp.exp(d))
        e_rem.append(jnp.exp(dlast - d))
        e_last.append(jnp.exp(dlast))

    def colv(per_chunk, u):
        j, hd = units[u]
        return per_chunk[j][:, hd:hd + 1]

    q = [qkv_ref[rows(j), _head_cols(0, hd)] for j, hd in units]
    k = [qkv_ref[rows(j), _head_cols(GDN_WIDTH, hd)] for j, hd in units]
    kf = [k[u].astype(F32) for u in n_units]
    kb = [kf[u] * colv(betas, u) for u in n_units]
    gram = [_dot_nt(jnp.concatenate([_bf(kb[u]), q[u]], axis=0), k[u]) for u in n_units]
    lmat = [jnp.exp(jnp.where(causal, colv(dcols, u) - row_ref[hd:hd + 1, rows(j)], -jnp.inf))
            for u, (j, hd) in enumerate(units)]
    a = [jnp.where(strict, gram[u][0:c] * lmat[u], 0.0) for u in n_units]
    qk = [_bf(gram[u][c:2 * c] * lmat[u]) for u in n_units]
    a_b = [_bf(a[u]) for u in n_units]
    tb = [_bf(eye) - jnp.where((ii // 2) == (jj // 2), a_b[u], jnp.zeros_like(a_b[u])) for u in n_units]
    b = 2
    while b < c:
        off = _bf((((ii // (2 * b)) == (jj // (2 * b))) & ((ii // b) != (jj // b))).astype(F32))
        if b % BF16_ROWS == 0:
            lower = [slice(s0 + b, s0 + 2 * b) for s0 in range(0, c, 2 * b)]
            m = [_dot(jnp.concatenate([tb[u][rs] for rs in lower], axis=0), a_b[u] * off) for u in n_units]
            upd = [_bf(_dot(_bf(m[u]), tb[u])) for u in n_units]
            tb = [jnp.concatenate(
                [piece for n, rs in enumerate(lower)
                 for piece in (tb[u][rs.start - b:rs.start], tb[u][rs] - upd[u][n * b:(n + 1) * b])], axis=0)
                 for u in n_units]
        else:
            m = [_dot(tb[u], a_b[u] * off) for u in n_units]
            tb = [tb[u] - _bf(_dot(_bf(m[u]), tb[u])) for u in n_units]
        b *= 2
    r = [jnp.concatenate([qkv_ref[rows(j), _head_cols(2 * GDN_WIDTH, hd)].astype(F32) * colv(betas, u),
                          kb[u] * colv(e_dec, u)], axis=1) for u, (j, hd) in enumerate(units)]
    uw = [_dot(tb[u], _bf(r[u])) for u in n_units]
    lhs = [jnp.concatenate([_bf(uw[u][:, HEAD_DIM:2 * HEAD_DIM]),
                            _bf(q[u].astype(F32) * colv(e_dec, u))], axis=0) for u in n_units]
    kd = [_bf(kf[u] * colv(e_rem, u)) for u in n_units]
    s = [state_ref[hd] for hd in range(GDN_HEADS)]
    for j in range(n_chunks):
        us = [j * GDN_HEADS + hd for hd in range(GDN_HEADS)]
        ws_qs = [_dot(lhs[u], _bf(s[hd])) for hd, u in enumerate(us)]
        v_new = [_bf(uw[u][:, 0:HEAD_DIM] - ws_qs[hd][0:c]) for hd, u in enumerate(us)]
        o = [ws_qs[hd][c:2 * c] + _dot(qk[u], v_new[hd]) for hd, u in enumerate(us)]
        s = [s[hd] * colv(e_last, u) + _dot_tn(kd[u], v_new[hd]) for hd, u in enumerate(us)]
        for hd in range(GDN_HEADS):
            on = o[hd] * lax.rsqrt(jnp.mean(o[hd] * o[hd], axis=-1, keepdims=True) + EPS) * gnw_ref[...]
            yb_ref[rows(j), _head_cols(0, hd)] = (
                on * zb_ref[rows(j), _head_cols(0, hd)].astype(F32)).astype(yb_ref.dtype)
    for hd in range(GDN_HEADS):
        state_ref[hd] = s[hd]


def _mix_kernel(tiles_per_seq,
                h_ref, nw_ref, w_ref, wab_ref, lng_ref, lnb_ref, wsp_ref, bspt_ref,
                cw_ref, alog_c_ref, dtb_c_ref, alog_r_ref, dtb_r_ref, gnw_ref,
                ya_ref, yb_ref,
                cs_ref, state_ref, qkv_ref, zb_ref, col_ref, row_ref):
    @pl.when(pl.program_id(0) % tiles_per_seq == 0)
    def _():
        cs_ref[...] = jnp.zeros(cs_ref.shape, F32)
        state_ref[...] = jnp.zeros(state_ref.shape, F32)

    xn = _bf(_rms(h_ref[...], nw_ref[...]))
    _gmlp_branch(xn, w_ref, lng_ref, lnb_ref, wsp_ref, bspt_ref, ya_ref)
    _gdn_prepare(xn, w_ref, wab_ref, cw_ref, alog_c_ref, dtb_c_ref, alog_r_ref, dtb_r_ref,
                 cs_ref, qkv_ref, zb_ref, col_ref, row_ref)
    _gdn_chunks(qkv_ref, zb_ref, col_ref, row_ref, gnw_ref, state_ref, yb_ref)


def _layer_spec(layer, shape):
    return pl.BlockSpec((None,) + shape, lambda i: (layer,) + (0,) * len(shape),
                        pipeline_mode=pl.Buffered(1))


def _mix(layer, h, nw, w_in_t, lng, lnb, wsp, bspt, cw, alog_c, dtb_c, alog_r, dtb_r, gnw, seq):
    t = h.shape[0]
    tm = TILE_MIX
    spec = functools.partial(_layer_spec, layer)
    return pl.pallas_call(
        functools.partial(_mix_kernel, seq // tm),
        out_shape=(
            jax.ShapeDtypeStruct((t, GMLP_WIDTH), BF16),
            jax.ShapeDtypeStruct((t, GDN_WIDTH), BF16),
        ),
        grid=(t // tm,),
        in_specs=[
            pl.BlockSpec((tm, D_MODEL), lambda i: (i, 0)),
            spec((1, D_MODEL)),
            spec((MAIN_COLS, D_MODEL)),
            pl.BlockSpec((None, 2 * GDN_HEADS, D_MODEL),
                         lambda i: (layer, MAIN_COLS // (2 * GDN_HEADS), 0), pipeline_mode=pl.Buffered(1)),
            spec((1, GMLP_WIDTH)),
            spec((1, GMLP_WIDTH)),
            spec((GMLP_GROUPS, SPATIAL_BLOCK, SPATIAL_BLOCK)),
            spec((SPATIAL_BLOCK, GMLP_GROUPS)),
            spec((CONV_K, 3 * GDN_WIDTH)),
            spec((1, GDN_HEADS)),
            spec((1, GDN_HEADS)),
            spec((GDN_HEADS, 1)),
            spec((GDN_HEADS, 1)),
            spec((1, HEAD_DIM)),
        ],
        out_specs=(
            pl.BlockSpec((tm, GMLP_WIDTH), lambda i: (i, 0)),
            pl.BlockSpec((tm, GDN_WIDTH), lambda i: (i, 0)),
        ),
        scratch_shapes=[
            pltpu.VMEM((CONV_PAD, 3 * GDN_WIDTH), F32),
            pltpu.VMEM((GDN_HEADS, HEAD_DIM, HEAD_DIM), F32),
            pltpu.VMEM((tm, 3 * GDN_WIDTH), BF16),
            pltpu.VMEM((tm, GDN_WIDTH), BF16),
            pltpu.VMEM((tm, 2 * GDN_HEADS), F32),
            pltpu.VMEM((GDN_HEADS, tm), F32),
        ],
        compiler_params=pltpu.CompilerParams(
            dimension_semantics=("arbitrary",), vmem_limit_bytes=VMEM_LIMIT_BYTES),
        name="mix",
    )(h, nw, w_in_t, w_in_t, lng, lnb, wsp, bspt, cw, alog_c, dtb_c, alog_r, dtb_r, gnw)


def _out_ple_kernel(apply_final_norm,
                    h_ref, ya_ref, yb_ref, p_ref, wout_ref, wple_ref, pnw_ref, gnw_ref, wg_ref, fnw_ref,
                    o_ref):
    h1 = (h_ref[...] + _dot(ya_ref[...], wout_ref[0:GMLP_WIDTH, :])
          + _dot(yb_ref[...], wout_ref[GMLP_WIDTH:GMLP_WIDTH + GDN_WIDTH, :]))
    e = _rms(_dot(_bf(p_ref[...]), wple_ref[...]), pnw_ref[...])
    gate = jax.nn.sigmoid(_dot(_bf(_rms(h1, gnw_ref[...])), wg_ref[...]))
    h2 = h1 + gate * e
    if apply_final_norm:
        h2 = _rms(h2, fnw_ref[...])
    o_ref[...] = h2


def _out_ple(layer, h, ya, yb, p, wout, wple, pnw, gnw, wg, fnw, apply_final_norm):
    t = h.shape[0]
    tm = TILE_OUT
    spec = functools.partial(_layer_spec, layer)
    tok = lambda width: pl.BlockSpec((tm, width), lambda i: (i, 0))
    return pl.pallas_call(
        functools.partial(_out_ple_kernel, apply_final_norm),
        out_shape=jax.ShapeDtypeStruct((t, D_MODEL), F32),
        grid=(t // tm,),
        in_specs=[
            tok(D_MODEL), tok(GMLP_WIDTH), tok(GDN_WIDTH),
            pl.BlockSpec((None, tm, PLE_DIM), lambda i: (layer, i, 0)),
            spec((GMLP_WIDTH + GDN_WIDTH, D_MODEL)),
            spec((PLE_DIM, D_MODEL)),
            spec((1, D_MODEL)),
            spec((1, D_MODEL)),
            spec((D_MODEL, D_MODEL)),
            pl.BlockSpec((1, D_MODEL), lambda i: (0, 0)),
        ],
        out_specs=tok(D_MODEL),
        compiler_params=pltpu.CompilerParams(
            dimension_semantics=("arbitrary",), vmem_limit_bytes=VMEM_LIMIT_BYTES),
        name="out_ple",
    )(h, ya, yb, p, wout, wple, pnw, gnw, wg, fnw)


def kernel(x, p, norm_w, w_in, ln_v_g, ln_v_b, w_spatial, b_spatial, conv_w, A_log, dt_bias,
           gdn_norm_w, w_out, w_ple, ple_norm_w, ple_gate_norm_w, w_ple_gate, final_norm_w):
    batch, seq, d = x.shape
    depth = w_in.shape[0]
    assert d == D_MODEL and seq % TILE_MIX == 0 and (batch * seq) % TILE_OUT == 0
    assert w_in.shape[2] == MAIN_COLS + 2 * GDN_HEADS
    t = batch * seq
    h = x.reshape(t, d)
    p2 = p.reshape(depth, t, PLE_DIM)
    w_in_t = _bf(jnp.swapaxes(w_in, 1, 2))
    wsp_b, wout_b, wple_b, wg_b = _bf(w_spatial), _bf(w_out), _bf(w_ple), _bf(w_ple_gate)
    bspt = jnp.swapaxes(b_spatial, 1, 2)
    row3 = lambda a: a[:, None, :]
    col3 = lambda a: a[:, :, None]
    for i in range(depth):
        ya, yb = _mix(i, h, row3(norm_w), w_in_t, row3(ln_v_g), row3(ln_v_b), wsp_b, bspt,
                      conv_w, row3(A_log), row3(dt_bias), col3(A_log), col3(dt_bias), row3(gdn_norm_w), seq)
        h = _out_ple(i, h, ya, yb, p2, wout_b, wple_b, row3(ple_norm_w), row3(ple_gate_norm_w), wg_b,
                     final_norm_w[None], apply_final_norm=(i == depth - 1))
    return h.reshape(batch, seq, d)
```

```python
import functools

import jax
import jax.numpy as jnp
from jax import lax
from jax.experimental import pallas as pl
from jax.experimental.pallas import tpu as pltpu

D_MODEL = 1024
GMLP_WIDTH = 1024
GMLP_GROUPS = 8
GROUP_DIM = 128
SPATIAL_BLOCK = 128
SPATIAL_CHUNK = 64
GDN_WIDTH = 1024
GDN_HEADS = 8
HEAD_DIM = 128
CONV_K = 4
PLE_DIM = 256
EPS = 1e-6
MAIN_COLS = 3 * GMLP_WIDTH + 4 * GDN_WIDTH

GDN_CHUNK = 128
TILE_MIX = 256
TILE_OUT = 512
STRIP_COLS = 256
BF16_ROWS = 16
CONV_PAD = 8
VMEM_LIMIT_BYTES = 56 * 1024 * 1024

F32 = jnp.float32
BF16 = jnp.bfloat16
LOG2E = 1.4426950408889634
GELU_K1 = -2.0 * 0.7978845608028654 * LOG2E
GELU_K3 = GELU_K1 * 0.044715


def _bf(x):
    return x.astype(BF16)


def _dot(a, b):
    return jnp.dot(a, b, preferred_element_type=F32)


def _dot_nt(a, b):
    return lax.dot_general(a, b, (((1,), (1,)), ((), ())), preferred_element_type=F32)


def _dot_tn(a, b):
    return lax.dot_general(a, b, (((0,), (0,)), ((), ())), preferred_element_type=F32)


def _rms(x, w):
    return x * lax.rsqrt(jnp.mean(x * x, axis=-1, keepdims=True) + EPS) * w


def _silu(x):
    return x / (1.0 + jnp.exp2(x * -LOG2E))


def _gelu_tanh(x):
    return x / (1.0 + jnp.exp2(x * (GELU_K1 + GELU_K3 * (x * x))))


def _softplus(x):
    return jnp.maximum(x, 0.0) + jnp.log1p(jnp.exp(-jnp.abs(x)))


def _head_cols(base, hd):
    return slice(base + hd * HEAD_DIM, base + (hd + 1) * HEAD_DIM)


def _gmlp_branch(xn, w_ref, lng_ref, lnb_ref, wsp_ref, bspt_ref, ya_ref):
    tm = xn.shape[0]
    strips = range(0, GMLP_WIDTH, STRIP_COLS)

    def proj(base, c0):
        return _dot(xn, w_ref[:, base + c0:base + c0 + STRIP_COLS])

    v = jnp.concatenate([_gelu_tanh(proj(GMLP_WIDTH, c0)) for c0 in strips], axis=1)
    mu = jnp.mean(v, axis=-1, keepdims=True)
    vc = v - mu
    var = jnp.mean(vc * vc, axis=-1, keepdims=True)
    vln = _bf(vc * lax.rsqrt(var + EPS) * lng_ref[...] + lnb_ref[...])
    uz = jnp.concatenate([_gelu_tanh(proj(0, c0)) * _silu(proj(2 * GMLP_WIDTH, c0)) for c0 in strips], axis=1)
    ri = lax.broadcasted_iota(jnp.int32, (SPATIAL_BLOCK, SPATIAL_BLOCK), 0) // SPATIAL_CHUNK
    ci = lax.broadcasted_iota(jnp.int32, (SPATIAL_BLOCK, SPATIAL_BLOCK), 1) // SPATIAL_CHUNK
    chunk_causal = ci <= ri
    for g in range(GMLP_GROUPS):
        gs = slice(g * GROUP_DIM, (g + 1) * GROUP_DIM)
        wm = jnp.where(chunk_causal, wsp_ref[g], jnp.zeros_like(wsp_ref[g]))
        bias = bspt_ref[:, g:g + 1]
        for r in range(tm // SPATIAL_BLOCK):
            rs = slice(r * SPATIAL_BLOCK, (r + 1) * SPATIAL_BLOCK)
            mixed = _dot(wm, vln[rs, gs]) + bias
            ya_ref[rs, gs] = (uz[rs, gs] * mixed).astype(ya_ref.dtype)


def _gdn_prepare(xn, w_ref, wab_ref, wabt_ref, cw_ref, alog_c_ref, dtb_c_ref, alog_r_ref, dtb_r_ref,
                 cs_ref, qkv_ref, zb_ref, col_ref, row_ref):
    tm = xn.shape[0]
    sub = lax.broadcasted_iota(jnp.int32, (CONV_PAD, STRIP_COLS), 0)
    for c0 in range(0, 3 * GDN_WIDTH, STRIP_COLS):
        cols = slice(c0, c0 + STRIP_COLS)
        x = _dot(xn, w_ref[:, 3 * GMLP_WIDTH + c0:3 * GMLP_WIDTH + c0 + STRIP_COLS])
        prev = cs_ref[:, cols]
        acc = x * cw_ref[CONV_K - 1:CONV_K, cols]
        for shift in range(1, CONV_K):
            xs = pltpu.roll(x, shift, axis=0)
            head = jnp.where(sub < shift, pltpu.roll(prev, shift, axis=0), xs[0:CONV_PAD])
            shifted = jnp.concatenate([head, xs[CONV_PAD:]], axis=0)
            acc = acc + shifted * cw_ref[CONV_K - 1 - shift:CONV_K - shift, cols]
        cs_ref[:, cols] = x[tm - CONV_PAD:tm]
        act = _silu(acc)
        if c0 >= 2 * GDN_WIDTH:
            qkv_ref[:, cols] = _bf(act)
        else:
            scale = HEAD_DIM ** -0.5 if c0 < GDN_WIDTH else 1.0
            for hd in range(STRIP_COLS // HEAD_DIM):
                t = act[:, _head_cols(0, hd)]
                t = t * (lax.rsqrt(jnp.sum(t * t, axis=-1, keepdims=True) + EPS) * scale)
                qkv_ref[:, _head_cols(c0, hd)] = _bf(t)

    for c0 in range(0, GDN_WIDTH, STRIP_COLS):
        zb_ref[:, c0:c0 + STRIP_COLS] = _bf(_silu(_dot(
            xn, w_ref[:, 3 * GMLP_WIDTH + 3 * GDN_WIDTH + c0:3 * GMLP_WIDTH + 3 * GDN_WIDTH + c0 + STRIP_COLS])))

    ab = _dot(xn, wab_ref[...])
    g_col = -jnp.exp(alog_c_ref[...]) * _softplus(ab[:, 0:GDN_HEADS] + dtb_c_ref[...])
    beta = jax.nn.sigmoid(ab[:, GDN_HEADS:2 * GDN_HEADS])
    a_row = _dot_nt(wabt_ref[...], xn)
    g_row = -jnp.exp(alog_r_ref[...]) * _softplus(a_row + dtb_r_ref[...])
    ii = lax.broadcasted_iota(jnp.int32, (GDN_CHUNK, GDN_CHUNK), 0)
    jj = lax.broadcasted_iota(jnp.int32, (GDN_CHUNK, GDN_CHUNK), 1)
    tri_lower = (jj <= ii).astype(F32)
    tri_upper = (ii <= jj).astype(F32)
    for r in range(tm // GDN_CHUNK):
        rs = slice(r * GDN_CHUNK, (r + 1) * GDN_CHUNK)
        col_ref[rs, 0:GDN_HEADS] = jnp.dot(tri_lower, g_col[rs, :], precision=lax.Precision.HIGHEST,
                                           preferred_element_type=F32)
        col_ref[rs, GDN_HEADS:2 * GDN_HEADS] = beta[rs, :]
        row_ref[:, rs] = jnp.dot(g_row[:, rs], tri_upper, precision=lax.Precision.HIGHEST,
                                 preferred_element_type=F32)


def _gdn_chunks(qkv_ref, zb_ref, col_ref, row_ref, gnw_ref, state_ref, yb_ref):
    c = GDN_CHUNK
    n_chunks = qkv_ref.shape[0] // c
    ii = lax.broadcasted_iota(jnp.int32, (c, c), 0)
    jj = lax.broadcasted_iota(jnp.int32, (c, c), 1)
    causal = jj <= ii
    strict = jj < ii
    eye = (ii == jj).astype(F32)
    units = [(j, hd) for j in range(n_chunks) for hd in range(GDN_HEADS)]
    n_units = range(len(units))

    def rows(j):
        return slice(j * c, (j + 1) * c)

    dcols, betas, e_dec, e_rem, e_last = [], [], [], [], []
    for j in range(n_chunks):
        col = col_ref[rows(j), :]
        d = col[:, 0:GDN_HEADS]
        dlast = d[c - 1:c, :]
        dcols.append(d)
        betas.append(col[:, GDN_HEADS:2 * GDN_HEADS])
        e_dec.append(jnp.exp(d))
        e_rem.append(jnp.exp(dlast - d))
        e_last.append(jnp.exp(dlast))

    def colv(per_chunk, u):
        j, hd = units[u]
        return per_chunk[j][:, hd:hd + 1]

    q = [qkv_ref[rows(j), _head_cols(0, hd)] for j, hd in units]
    k = [qkv_ref[rows(j), _head_cols(GDN_WIDTH, hd)] for j, hd in units]
    kf = [k[u].astype(F32) for u in n_units]
    kb = [kf[u] * colv(betas, u) for u in n_units]
    gram = [_dot_nt(jnp.concatenate([_bf(kb[u]), q[u]], axis=0), k[u]) for u in n_units]
    lmat = [jnp.exp(jnp.where(causal, colv(dcols, u) - row_ref[hd:hd + 1, rows(j)], -jnp.inf))
            for u, (j, hd) in enumerate(units)]
    a = [jnp.where(strict, gram[u][0:c] * lmat[u], 0.0) for u in n_units]
    qk = [_bf(gram[u][c:2 * c] * lmat[u]) for u in n_units]
    a_b = [_bf(a[u]) for u in n_units]
    tb = [_bf(eye) - jnp.where((ii // 2) == (jj // 2), a_b[u], jnp.zeros_like(a_b[u])) for u in n_units]
    b = 2
    while b < c:
        off = _bf((((ii // (2 * b)) == (jj // (2 * b))) & ((ii // b) != (jj // b))).astype(F32))
        if b % BF16_ROWS == 0:
            lower = [slice(s0 + b, s0 + 2 * b) for s0 in range(0, c, 2 * b)]
            m = [_dot(jnp.concatenate([tb[u][rs] for rs in lower], axis=0), a_b[u] * off) for u in n_units]
            upd = [_bf(_dot(_bf(m[u]), tb[u])) for u in n_units]
            tb = [jnp.concatenate(
                [piece for n, rs in enumerate(lower)
                 for piece in (tb[u][rs.start - b:rs.start], tb[u][rs] - upd[u][n * b:(n + 1) * b])], axis=0)
                 for u in n_units]
        else:
            m = [_dot(tb[u], a_b[u] * off) for u in n_units]
            tb = [tb[u] - _bf(_dot(_bf(m[u]), tb[u])) for u in n_units]
        b *= 2
    r = [jnp.concatenate([qkv_ref[rows(j), _head_cols(2 * GDN_WIDTH, hd)].astype(F32) * colv(betas, u),
                          kb[u] * colv(e_dec, u)], axis=1) for u, (j, hd) in enumerate(units)]
    uw = [_dot(tb[u], _bf(r[u])) for u in n_units]
    w_b = [_bf(uw[u][:, HEAD_DIM:2 * HEAD_DIM]) for u in n_units]
    q_qk = [jnp.concatenate([_bf(q[u].astype(F32) * colv(e_dec, u)), qk[u]], axis=1) for u in n_units]
    kd = [_bf(kf[u] * colv(e_rem, u)) for u in n_units]
    s = [state_ref[hd] for hd in range(GDN_HEADS)]
    for j in range(n_chunks):
        us = [j * GDN_HEADS + hd for hd in range(GDN_HEADS)]
        s_b = [_bf(s[hd]) for hd in range(GDN_HEADS)]
        v_new = [_bf(uw[u][:, 0:HEAD_DIM] - _dot(w_b[u], s_b[hd])) for hd, u in enumerate(us)]
        o = [_dot(q_qk[u], jnp.concatenate([s_b[hd], v_new[hd]], axis=0)) for hd, u in enumerate(us)]
        s = [s[hd] * colv(e_last, u) + _dot_tn(kd[u], v_new[hd]) for hd, u in enumerate(us)]
        for hd in range(GDN_HEADS):
            on = o[hd] * lax.rsqrt(jnp.mean(o[hd] * o[hd], axis=-1, keepdims=True) + EPS) * gnw_ref[...]
            yb_ref[rows(j), _head_cols(0, hd)] = (
                on * zb_ref[rows(j), _head_cols(0, hd)].astype(F32)).astype(yb_ref.dtype)
    for hd in range(GDN_HEADS):
        state_ref[hd] = s[hd]


def _mix_kernel(tiles_per_seq,
                h_ref, nw_ref, w_ref, wab_ref, wabt_ref, lng_ref, lnb_ref, wsp_ref, bspt_ref,
                cw_ref, alog_c_ref, dtb_c_ref, alog_r_ref, dtb_r_ref, gnw_ref,
                ya_ref, yb_ref,
                cs_ref, state_ref, qkv_ref, zb_ref, col_ref, row_ref):
    @pl.when(pl.program_id(0) % tiles_per_seq == 0)
    def _():
        cs_ref[...] = jnp.zeros(cs_ref.shape, F32)
        state_ref[...] = jnp.zeros(state_ref.shape, F32)

    xn = _bf(_rms(h_ref[...], nw_ref[...]))
    _gmlp_branch(xn, w_ref, lng_ref, lnb_ref, wsp_ref, bspt_ref, ya_ref)
    _gdn_prepare(xn, w_ref, wab_ref, wabt_ref, cw_ref, alog_c_ref, dtb_c_ref, alog_r_ref, dtb_r_ref,
                 cs_ref, qkv_ref, zb_ref, col_ref, row_ref)
    _gdn_chunks(qkv_ref, zb_ref, col_ref, row_ref, gnw_ref, state_ref, yb_ref)


def _layer_spec(layer, shape):
    return pl.BlockSpec((None,) + shape, lambda i: (layer,) + (0,) * len(shape),
                        pipeline_mode=pl.Buffered(1))


def _mix(layer, h, nw, w_main, wab, wabt, lng, lnb, wsp, bspt, cw, alog_c, dtb_c, alog_r, dtb_r, gnw, seq):
    t = h.shape[0]
    tm = TILE_MIX
    spec = functools.partial(_layer_spec, layer)
    return pl.pallas_call(
        functools.partial(_mix_kernel, seq // tm),
        out_shape=(
            jax.ShapeDtypeStruct((t, GMLP_WIDTH), BF16),
            jax.ShapeDtypeStruct((t, GDN_WIDTH), BF16),
        ),
        grid=(t // tm,),
        in_specs=[
            pl.BlockSpec((tm, D_MODEL), lambda i: (i, 0)),
            spec((1, D_MODEL)),
            spec((D_MODEL, MAIN_COLS)),
            spec((D_MODEL, 2 * GDN_HEADS)),
            spec((GDN_HEADS, D_MODEL)),
            spec((1, GMLP_WIDTH)),
            spec((1, GMLP_WIDTH)),
            spec((GMLP_GROUPS, SPATIAL_BLOCK, SPATIAL_BLOCK)),
            spec((SPATIAL_BLOCK, GMLP_GROUPS)),
            spec((CONV_K, 3 * GDN_WIDTH)),
            spec((1, GDN_HEADS)),
            spec((1, GDN_HEADS)),
            spec((GDN_HEADS, 1)),
            spec((GDN_HEADS, 1)),
            spec((1, HEAD_DIM)),
        ],
        out_specs=(
            pl.BlockSpec((tm, GMLP_WIDTH), lambda i: (i, 0)),
            pl.BlockSpec((tm, GDN_WIDTH), lambda i: (i, 0)),
        ),
        scratch_shapes=[
            pltpu.VMEM((CONV_PAD, 3 * GDN_WIDTH), F32),
            pltpu.VMEM((GDN_HEADS, HEAD_DIM, HEAD_DIM), F32),
            pltpu.VMEM((tm, 3 * GDN_WIDTH), BF16),
            pltpu.VMEM((tm, GDN_WIDTH), BF16),
            pltpu.VMEM((tm, 2 * GDN_HEADS), F32),
            pltpu.VMEM((GDN_HEADS, tm), F32),
        ],
        compiler_params=pltpu.CompilerParams(
            dimension_semantics=("arbitrary",), vmem_limit_bytes=VMEM_LIMIT_BYTES),
        name="mix",
    )(h, nw, w_main, wab, wabt, lng, lnb, wsp, bspt, cw, alog_c, dtb_c, alog_r, dtb_r, gnw)


def _out_ple_kernel(apply_final_norm,
                    h_ref, ya_ref, yb_ref, p_ref, wout_ref, wple_ref, pnw_ref, gnw_ref, wg_ref, fnw_ref,
                    o_ref):
    h1 = (h_ref[...] + _dot(ya_ref[...], wout_ref[0:GMLP_WIDTH, :])
          + _dot(yb_ref[...], wout_ref[GMLP_WIDTH:GMLP_WIDTH + GDN_WIDTH, :]))
    e = _rms(_dot(_bf(p_ref[...]), wple_ref[...]), pnw_ref[...])
    gate = jax.nn.sigmoid(_dot(_bf(_rms(h1, gnw_ref[...])), wg_ref[...]))
    h2 = h1 + gate * e
    if apply_final_norm:
        h2 = _rms(h2, fnw_ref[...])
    o_ref[...] = h2


def _out_ple(layer, h, ya, yb, p, wout, wple, pnw, gnw, wg, fnw, apply_final_norm):
    t = h.shape[0]
    tm = TILE_OUT
    spec = functools.partial(_layer_spec, layer)
    tok = lambda width: pl.BlockSpec((tm, width), lambda i: (i, 0))
    return pl.pallas_call(
        functools.partial(_out_ple_kernel, apply_final_norm),
        out_shape=jax.ShapeDtypeStruct((t, D_MODEL), F32),
        grid=(t // tm,),
        in_specs=[
            tok(D_MODEL), tok(GMLP_WIDTH), tok(GDN_WIDTH),
            pl.BlockSpec((None, tm, PLE_DIM), lambda i: (layer, i, 0)),
            spec((GMLP_WIDTH + GDN_WIDTH, D_MODEL)),
            spec((PLE_DIM, D_MODEL)),
            spec((1, D_MODEL)),
            spec((1, D_MODEL)),
            spec((D_MODEL, D_MODEL)),
            pl.BlockSpec((1, D_MODEL), lambda i: (0, 0)),
        ],
        out_specs=tok(D_MODEL),
        compiler_params=pltpu.CompilerParams(
            dimension_semantics=("arbitrary",), vmem_limit_bytes=VMEM_LIMIT_BYTES),
        name="out_ple",
    )(h, ya, yb, p, wout, wple, pnw, gnw, wg, fnw)


def kernel(x, p, norm_w, w_in, ln_v_g, ln_v_b, w_spatial, b_spatial, conv_w, A_log, dt_bias,
           gdn_norm_w, w_out, w_ple, ple_norm_w, ple_gate_norm_w, w_ple_gate, final_norm_w):
    batch, seq, d = x.shape
    depth = w_in.shape[0]
    assert d == D_MODEL and seq % TILE_MIX == 0 and (batch * seq) % TILE_OUT == 0
    assert w_in.shape[2] == MAIN_COLS + 2 * GDN_HEADS
    t = batch * seq
    h = x.reshape(t, d)
    p2 = p.reshape(depth, t, PLE_DIM)
    w_in_b = _bf(w_in)
    wab = w_in_b[:, :, MAIN_COLS:]
    wabt = jnp.swapaxes(wab[:, :, 0:GDN_HEADS], 1, 2)
    wsp_b, wout_b, wple_b, wg_b = _bf(w_spatial), _bf(w_out), _bf(w_ple), _bf(w_ple_gate)
    bspt = jnp.swapaxes(b_spatial, 1, 2)
    row3 = lambda a: a[:, None, :]
    col3 = lambda a: a[:, :, None]
    for i in range(depth):
        ya, yb = _mix(i, h, row3(norm_w), w_in_b, wab, wabt, row3(ln_v_g), row3(ln_v_b), wsp_b, bspt,
                      conv_w, row3(A_log), row3(dt_bias), col3(A_log), col3(dt_bias), row3(gdn_norm_w), seq)
        h = _out_ple(i, h, ya, yb, p2, wout_b, wple_b, row3(ple_norm_w), row3(ple_gate_norm_w), wg_b,
                     final_norm_w[None], apply_final_norm=(i == depth - 1))
    return h.reshape(batch, seq, d)
```

```python
import functools

import jax
import jax.numpy as jnp
from jax import lax
from jax.experimental import pallas as pl
from jax.experimental.pallas import tpu as pltpu

D_MODEL = 1024
GMLP_WIDTH = 1024
GMLP_GROUPS = 8
GROUP_DIM = 128
SPATIAL_BLOCK = 128
SPATIAL_CHUNK = 64
GDN_WIDTH = 1024
GDN_HEADS = 8
HEAD_DIM = 128
CONV_K = 4
PLE_DIM = 256
EPS = 1e-6
MAIN_COLS = 3 * GMLP_WIDTH + 4 * GDN_WIDTH

GDN_CHUNK = 128
TILE_MIX = 512
TILE_OUT = 512
GDN_GROUP_ROWS = 256
STRIP_COLS = 256
BF16_ROWS = 16
CONV_PAD = 8
VMEM_LIMIT_BYTES = 56 * 1024 * 1024

F32 = jnp.float32
BF16 = jnp.bfloat16
LOG2E = 1.4426950408889634
GELU_K1 = -2.0 * 0.7978845608028654 * LOG2E
GELU_K3 = GELU_K1 * 0.044715


def _bf(x):
    return x.astype(BF16)


def _dot(a, b):
    return jnp.dot(a, b, preferred_element_type=F32)


def _dot_nt(a, b):
    return lax.dot_general(a, b, (((1,), (1,)), ((), ())), preferred_element_type=F32)


def _dot_tn(a, b):
    return lax.dot_general(a, b, (((0,), (0,)), ((), ())), preferred_element_type=F32)


def _rms(x, w):
    return x * lax.rsqrt(jnp.mean(x * x, axis=-1, keepdims=True) + EPS) * w


def _silu(x):
    return x / (1.0 + jnp.exp2(x * -LOG2E))


def _gelu_tanh(x):
    return x / (1.0 + jnp.exp2(x * (GELU_K1 + GELU_K3 * (x * x))))


def _softplus(x):
    return jnp.maximum(x, 0.0) + jnp.log1p(jnp.exp(-jnp.abs(x)))


def _head_cols(base, hd):
    return slice(base + hd * HEAD_DIM, base + (hd + 1) * HEAD_DIM)


def _gmlp_branch(xn, w_ref, lng_ref, lnb_ref, wsp_ref, bspt_ref, ya_ref):
    tm = xn.shape[0]
    strips = range(0, GMLP_WIDTH, STRIP_COLS)

    def proj(base, c0):
        return _dot(xn, w_ref[:, base + c0:base + c0 + STRIP_COLS])

    v = jnp.concatenate([_gelu_tanh(proj(GMLP_WIDTH, c0)) for c0 in strips], axis=1)
    mu = jnp.mean(v, axis=-1, keepdims=True)
    vc = v - mu
    var = jnp.mean(vc * vc, axis=-1, keepdims=True)
    vln = _bf(vc * lax.rsqrt(var + EPS) * lng_ref[...] + lnb_ref[...])
    uz = jnp.concatenate([_gelu_tanh(proj(0, c0)) * _silu(proj(2 * GMLP_WIDTH, c0)) for c0 in strips], axis=1)
    ri = lax.broadcasted_iota(jnp.int32, (SPATIAL_BLOCK, SPATIAL_BLOCK), 0) // SPATIAL_CHUNK
    ci = lax.broadcasted_iota(jnp.int32, (SPATIAL_BLOCK, SPATIAL_BLOCK), 1) // SPATIAL_CHUNK
    chunk_causal = ci <= ri
    for g in range(GMLP_GROUPS):
        gs = slice(g * GROUP_DIM, (g + 1) * GROUP_DIM)
        wm = jnp.where(chunk_causal, wsp_ref[g], jnp.zeros_like(wsp_ref[g]))
        bias = bspt_ref[:, g:g + 1]
        for r in range(tm // SPATIAL_BLOCK):
            rs = slice(r * SPATIAL_BLOCK, (r + 1) * SPATIAL_BLOCK)
            mixed = _dot(wm, vln[rs, gs]) + bias
            ya_ref[rs, gs] = (uz[rs, gs] * mixed).astype(ya_ref.dtype)


def _gdn_prepare(xn, w_ref, wab_ref, wabt_ref, cw_ref, alog_c_ref, dtb_c_ref, alog_r_ref, dtb_r_ref,
                 cs_ref, qkv_ref, zb_ref, col_ref, row_ref):
    tm = xn.shape[0]
    sub = lax.broadcasted_iota(jnp.int32, (CONV_PAD, STRIP_COLS), 0)
    for c0 in range(0, 3 * GDN_WIDTH, STRIP_COLS):
        cols = slice(c0, c0 + STRIP_COLS)
        x = _dot(xn, w_ref[:, 3 * GMLP_WIDTH + c0:3 * GMLP_WIDTH + c0 + STRIP_COLS])
        prev = cs_ref[:, cols]
        acc = x * cw_ref[CONV_K - 1:CONV_K, cols]
        for shift in range(1, CONV_K):
            xs = pltpu.roll(x, shift, axis=0)
            head = jnp.where(sub < shift, pltpu.roll(prev, shift, axis=0), xs[0:CONV_PAD])
            shifted = jnp.concatenate([head, xs[CONV_PAD:]], axis=0)
            acc = acc + shifted * cw_ref[CONV_K - 1 - shift:CONV_K - shift, cols]
        cs_ref[:, cols] = x[tm - CONV_PAD:tm]
        act = _silu(acc)
        if c0 >= 2 * GDN_WIDTH:
            qkv_ref[:, cols] = _bf(act)
        else:
            scale = HEAD_DIM ** -0.5 if c0 < GDN_WIDTH else 1.0
            for hd in range(STRIP_COLS // HEAD_DIM):
                t = act[:, _head_cols(0, hd)]
                t = t * (lax.rsqrt(jnp.sum(t * t, axis=-1, keepdims=True) + EPS) * scale)
                qkv_ref[:, _head_cols(c0, hd)] = _bf(t)

    for c0 in range(0, GDN_WIDTH, STRIP_COLS):
        zb_ref[:, c0:c0 + STRIP_COLS] = _bf(_silu(_dot(
            xn, w_ref[:, 3 * GMLP_WIDTH + 3 * GDN_WIDTH + c0:3 * GMLP_WIDTH + 3 * GDN_WIDTH + c0 + STRIP_COLS])))

    ab = _dot(xn, wab_ref[...])
    g_col = -jnp.exp(alog_c_ref[...]) * _softplus(ab[:, 0:GDN_HEADS] + dtb_c_ref[...])
    beta = jax.nn.sigmoid(ab[:, GDN_HEADS:2 * GDN_HEADS])
    a_row = _dot_nt(wabt_ref[...], xn)
    g_row = -jnp.exp(alog_r_ref[...]) * _softplus(a_row + dtb_r_ref[...])
    ii = lax.broadcasted_iota(jnp.int32, (GDN_CHUNK, GDN_CHUNK), 0)
    jj = lax.broadcasted_iota(jnp.int32, (GDN_CHUNK, GDN_CHUNK), 1)
    tri_lower = (jj <= ii).astype(F32)
    tri_upper = (ii <= jj).astype(F32)
    for r in range(tm // GDN_CHUNK):
        rs = slice(r * GDN_CHUNK, (r + 1) * GDN_CHUNK)
        col_ref[rs, 0:GDN_HEADS] = jnp.dot(tri_lower, g_col[rs, :], precision=lax.Precision.HIGHEST,
                                           preferred_element_type=F32)
        col_ref[rs, GDN_HEADS:2 * GDN_HEADS] = beta[rs, :]
        row_ref[:, rs] = jnp.dot(g_row[:, rs], tri_upper, precision=lax.Precision.HIGHEST,
                                 preferred_element_type=F32)


def _gdn_chunks(qkv_ref, zb_ref, col_ref, row_ref, gnw_ref, state_ref, yb_ref):
    c = GDN_CHUNK
    n_chunks = qkv_ref.shape[0] // c
    ii = lax.broadcasted_iota(jnp.int32, (c, c), 0)
    jj = lax.broadcasted_iota(jnp.int32, (c, c), 1)
    causal = jj <= ii
    strict = jj < ii
    eye = (ii == jj).astype(F32)
    units = [(j, hd) for j in range(n_chunks) for hd in range(GDN_HEADS)]
    n_units = range(len(units))

    def rows(j):
        return slice(j * c, (j + 1) * c)

    dcols, betas, e_dec, e_rem, e_last = [], [], [], [], []
    for j in range(n_chunks):
        col = col_ref[rows(j), :]
        d = col[:, 0:GDN_HEADS]
        dlast = d[c - 1:c, :]
        dcols.append(d)
        betas.append(col[:, GDN_HEADS:2 * GDN_HEADS])
        e_dec.append(jnp.exp(d))
        e_rem.append(jnp.exp(dlast - d))
        e_last.append(jnp.exp(dlast))

    def colv(per_chunk, u):
        j, hd = units[u]
        return per_chunk[j][:, hd:hd + 1]

    q = [qkv_ref[rows(j), _head_cols(0, hd)] for j, hd in units]
    k = [qkv_ref[rows(j), _head_cols(GDN_WIDTH, hd)] for j, hd in units]
    kf = [k[u].astype(F32) for u in n_units]
    kb = [kf[u] * colv(betas, u) for u in n_units]
    gram = [_dot_nt(jnp.concatenate([_bf(kb[u]), q[u]], axis=0), k[u]) for u in n_units]
    lmat = [jnp.exp(jnp.where(causal, colv(dcols, u) - row_ref[hd:hd + 1, rows(j)], -jnp.inf))
            for u, (j, hd) in enumerate(units)]
    a = [jnp.where(strict, gram[u][0:c] * lmat[u], 0.0) for u in n_units]
    qk = [_bf(gram[u][c:2 * c] * lmat[u]) for u in n_units]
    a_b = [_bf(a[u]) for u in n_units]
    tb = [_bf(eye) - jnp.where((ii // 2) == (jj // 2), a_b[u], jnp.zeros_like(a_b[u])) for u in n_units]
    b = 2
    while b < c:
        off = _bf((((ii // (2 * b)) == (jj // (2 * b))) & ((ii // b) != (jj // b))).astype(F32))
        if b % BF16_ROWS == 0:
            lower = [slice(s0 + b, s0 + 2 * b) for s0 in range(0, c, 2 * b)]
            m = [_dot(jnp.concatenate([tb[u][rs] for rs in lower], axis=0), a_b[u] * off) for u in n_units]
            upd = [_bf(_dot(_bf(m[u]), tb[u])) for u in n_units]
            tb = [jnp.concatenate(
                [piece for n, rs in enumerate(lower)
                 for piece in (tb[u][rs.start - b:rs.start], tb[u][rs] - upd[u][n * b:(n + 1) * b])], axis=0)
                 for u in n_units]
        else:
            m = [_dot(tb[u], a_b[u] * off) for u in n_units]
            tb = [tb[u] - _bf(_dot(_bf(m[u]), tb[u])) for u in n_units]
        b *= 2
    r = [jnp.concatenate([qkv_ref[rows(j), _head_cols(2 * GDN_WIDTH, hd)].astype(F32) * colv(betas, u),
                          kb[u] * colv(e_dec, u)], axis=1) for u, (j, hd) in enumerate(units)]
    uw = [_dot(tb[u], _bf(r[u])) for u in n_units]
    w_b = [_bf(uw[u][:, HEAD_DIM:2 * HEAD_DIM]) for u in n_units]
    q_qk = [jnp.concatenate([_bf(q[u].astype(F32) * colv(e_dec, u)), qk[u]], axis=1) for u in n_units]
    kd = [_bf(kf[u] * colv(e_rem, u)) for u in n_units]
    s = [state_ref[hd] for hd in range(GDN_HEADS)]
    for j in range(n_chunks):
        us = [j * GDN_HEADS + hd for hd in range(GDN_HEADS)]
        s_b = [_bf(s[hd]) for hd in range(GDN_HEADS)]
        v_new = [_bf(uw[u][:, 0:HEAD_DIM] - _dot(w_b[u], s_b[hd])) for hd, u in enumerate(us)]
        o = [_dot(q_qk[u], jnp.concatenate([s_b[hd], v_new[hd]], axis=0)) for hd, u in enumerate(us)]
        s = [s[hd] * colv(e_last, u) + _dot_tn(kd[u], v_new[hd]) for hd, u in enumerate(us)]
        for hd in range(GDN_HEADS):
            on = o[hd] * lax.rsqrt(jnp.mean(o[hd] * o[hd], axis=-1, keepdims=True) + EPS) * gnw_ref[...]
            yb_ref[rows(j), _head_cols(0, hd)] = (
                on * zb_ref[rows(j), _head_cols(0, hd)].astype(F32)).astype(yb_ref.dtype)
    for hd in range(GDN_HEADS):
        state_ref[hd] = s[hd]


def _mix_kernel(tiles_per_seq,
                h_ref, nw_ref, w_ref, wab_ref, wabt_ref, lng_ref, lnb_ref, wsp_ref, bspt_ref,
                cw_ref, alog_c_ref, dtb_c_ref, alog_r_ref, dtb_r_ref, gnw_ref,
                ya_ref, yb_ref,
                cs_ref, state_ref, qkv_ref, zb_ref, col_ref, row_ref):
    @pl.when(pl.program_id(0) % tiles_per_seq == 0)
    def _():
        cs_ref[...] = jnp.zeros(cs_ref.shape, F32)
        state_ref[...] = jnp.zeros(state_ref.shape, F32)

    xn = _bf(_rms(h_ref[...], nw_ref[...]))
    _gmlp_branch(xn, w_ref, lng_ref, lnb_ref, wsp_ref, bspt_ref, ya_ref)
    _gdn_prepare(xn, w_ref, wab_ref, wabt_ref, cw_ref, alog_c_ref, dtb_c_ref, alog_r_ref, dtb_r_ref,
                 cs_ref, qkv_ref, zb_ref, col_ref, row_ref)
    for r0 in range(0, h_ref.shape[0], GDN_GROUP_ROWS):
        rs = slice(r0, r0 + GDN_GROUP_ROWS)
        _gdn_chunks(qkv_ref.at[rs], zb_ref.at[rs], col_ref.at[rs], row_ref.at[:, rs], gnw_ref, state_ref,
                    yb_ref.at[rs])


def _layer_spec(layer, shape):
    return pl.BlockSpec((None,) + shape, lambda i: (layer,) + (0,) * len(shape),
                        pipeline_mode=pl.Buffered(1))


def _mix(layer, h, nw, w_main, wab, wabt, lng, lnb, wsp, bspt, cw, alog_c, dtb_c, alog_r, dtb_r, gnw, seq):
    t = h.shape[0]
    tm = TILE_MIX
    spec = functools.partial(_layer_spec, layer)
    return pl.pallas_call(
        functools.partial(_mix_kernel, seq // tm),
        out_shape=(
            jax.ShapeDtypeStruct((t, GMLP_WIDTH), BF16),
            jax.ShapeDtypeStruct((t, GDN_WIDTH), BF16),
        ),
        grid=(t // tm,),
        in_specs=[
            pl.BlockSpec((tm, D_MODEL), lambda i: (i, 0)),
            spec((1, D_MODEL)),
            spec((D_MODEL, MAIN_COLS)),
            spec((D_MODEL, 2 * GDN_HEADS)),
            spec((GDN_HEADS, D_MODEL)),
            spec((1, GMLP_WIDTH)),
            spec((1, GMLP_WIDTH)),
            spec((GMLP_GROUPS, SPATIAL_BLOCK, SPATIAL_BLOCK)),
            spec((SPATIAL_BLOCK, GMLP_GROUPS)),
            spec((CONV_K, 3 * GDN_WIDTH)),
            spec((1, GDN_HEADS)),
            spec((1, GDN_HEADS)),
            spec((GDN_HEADS, 1)),
            spec((GDN_HEADS, 1)),
            spec((1, HEAD_DIM)),
        ],
        out_specs=(
            pl.BlockSpec((tm, GMLP_WIDTH), lambda i: (i, 0)),
            pl.BlockSpec((tm, GDN_WIDTH), lambda i: (i, 0)),
        ),
        scratch_shapes=[
            pltpu.VMEM((CONV_PAD, 3 * GDN_WIDTH), F32),
            pltpu.VMEM((GDN_HEADS, HEAD_DIM, HEAD_DIM), F32),
            pltpu.VMEM((tm, 3 * GDN_WIDTH), BF16),
            pltpu.VMEM((tm, GDN_WIDTH), BF16),
            pltpu.VMEM((tm, 2 * GDN_HEADS), F32),
            pltpu.VMEM((GDN_HEADS, tm), F32),
        ],
        compiler_params=pltpu.CompilerParams(
            dimension_semantics=("arbitrary",), vmem_limit_bytes=VMEM_LIMIT_BYTES),
        name="mix",
    )(h, nw, w_main, wab, wabt, lng, lnb, wsp, bspt, cw, alog_c, dtb_c, alog_r, dtb_r, gnw)


def _out_ple_kernel(apply_final_norm,
                    h_ref, ya_ref, yb_ref, p_ref, wout_ref, wple_ref, pnw_ref, gnw_ref, wg_ref, fnw_ref,
                    o_ref):
    h1 = (h_ref[...] + _dot(ya_ref[...], wout_ref[0:GMLP_WIDTH, :])
          + _dot(yb_ref[...], wout_ref[GMLP_WIDTH:GMLP_WIDTH + GDN_WIDTH, :]))
    e = _rms(_dot(_bf(p_ref[...]), wple_ref[...]), pnw_ref[...])
    gate = jax.nn.sigmoid(_dot(_bf(_rms(h1, gnw_ref[...])), wg_ref[...]))
    h2 = h1 + gate * e
    if apply_final_norm:
        h2 = _rms(h2, fnw_ref[...])
    o_ref[...] = h2


def _out_ple(layer, h, ya, yb, p, wout, wple, pnw, gnw, wg, fnw, apply_final_norm):
    t = h.shape[0]
    tm = TILE_OUT
    spec = functools.partial(_layer_spec, layer)
    tok = lambda width: pl.BlockSpec((tm, width), lambda i: (i, 0))
    return pl.pallas_call(
        functools.partial(_out_ple_kernel, apply_final_norm),
        out_shape=jax.ShapeDtypeStruct((t, D_MODEL), F32),
        grid=(t // tm,),
        in_specs=[
            tok(D_MODEL), tok(GMLP_WIDTH), tok(GDN_WIDTH),
            pl.BlockSpec((None, tm, PLE_DIM), lambda i: (layer, i, 0)),
            spec((GMLP_WIDTH + GDN_WIDTH, D_MODEL)),
            spec((PLE_DIM, D_MODEL)),
            spec((1, D_MODEL)),
            spec((1, D_MODEL)),
            spec((D_MODEL, D_MODEL)),
            pl.BlockSpec((1, D_MODEL), lambda i: (0, 0)),
        ],
        out_specs=tok(D_MODEL),
        compiler_params=pltpu.CompilerParams(
            dimension_semantics=("arbitrary",), vmem_limit_bytes=VMEM_LIMIT_BYTES),
        name="out_ple",
    )(h, ya, yb, p, wout, wple, pnw, gnw, wg, fnw)


def kernel(x, p, norm_w, w_in, ln_v_g, ln_v_b, w_spatial, b_spatial, conv_w, A_log, dt_bias,
           gdn_norm_w, w_out, w_ple, ple_norm_w, ple_gate_norm_w, w_ple_gate, final_norm_w):
    batch, seq, d = x.shape
    depth = w_in.shape[0]
    assert d == D_MODEL and seq % TILE_MIX == 0 and (batch * seq) % TILE_OUT == 0
    assert w_in.shape[2] == MAIN_COLS + 2 * GDN_HEADS
    t = batch * seq
    h = x.reshape(t, d)
    p2 = p.reshape(depth, t, PLE_DIM)
    w_in_b = _bf(w_in)
    wab = w_in_b[:, :, MAIN_COLS:]
    wabt = jnp.swapaxes(wab[:, :, 0:GDN_HEADS], 1, 2)
    wsp_b, wout_b, wple_b, wg_b = _bf(w_spatial), _bf(w_out), _bf(w_ple), _bf(w_ple_gate)
    bspt = jnp.swapaxes(b_spatial, 1, 2)
    row3 = lambda a: a[:, None, :]
    col3 = lambda a: a[:, :, None]
    for i in range(depth):
        ya, yb = _mix(i, h, row3(norm_w), w_in_b, wab, wabt, row3(ln_v_g), row3(ln_v_b), wsp_b, bspt,
                      conv_w, row3(A_log), row3(dt_bias), col3(A_log), col3(dt_bias), row3(gdn_norm_w), seq)
        h = _out_ple(i, h, ya, yb, p2, wout_b, wple_b, row3(ple_norm_w), row3(ple_gate_norm_w), wg_b,
                     final_norm_w[None], apply_final_norm=(i == depth - 1))
    return h.reshape(batch, seq, d)
```

```python
import functools

import jax
import jax.numpy as jnp
from jax import lax
from jax.experimental import pallas as pl
from jax.experimental.pallas import tpu as pltpu

D_MODEL = 1024
GMLP_WIDTH = 1024
GMLP_GROUPS = 8
GROUP_DIM = 128
SPATIAL_BLOCK = 128
SPATIAL_CHUNK = 64
GDN_WIDTH = 1024
GDN_HEADS = 8
HEAD_DIM = 128
CONV_K = 4
PLE_DIM = 256
EPS = 1e-6
MAIN_COLS = 3 * GMLP_WIDTH + 4 * GDN_WIDTH

GDN_CHUNK = 128
TILE_MIX = 512
TILE_OUT = 512
GDN_GROUP_ROWS = 256
STRIP_COLS = 256
BF16_ROWS = 16
CONV_PAD = 8
VMEM_LIMIT_BYTES = 56 * 1024 * 1024

F32 = jnp.float32
BF16 = jnp.bfloat16
LOG2E = 1.4426950408889634
GELU_K1 = -2.0 * 0.7978845608028654 * LOG2E
GELU_K3 = GELU_K1 * 0.044715


def _bf(x):
    return x.astype(BF16)


def _dot(a, b):
    return jnp.dot(a, b, preferred_element_type=F32)


def _dot_nt(a, b):
    return lax.dot_general(a, b, (((1,), (1,)), ((), ())), preferred_element_type=F32)


def _dot_tn(a, b):
    return lax.dot_general(a, b, (((0,), (0,)), ((), ())), preferred_element_type=F32)


def _rms(x, w):
    return x * lax.rsqrt(jnp.mean(x * x, axis=-1, keepdims=True) + EPS) * w


def _silu(x):
    return x / (1.0 + jnp.exp2(x * -LOG2E))


def _gelu_tanh(x):
    return x / (1.0 + jnp.exp2(x * (GELU_K1 + GELU_K3 * (x * x))))


def _softplus(x):
    return jnp.maximum(x, 0.0) + jnp.log1p(jnp.exp(-jnp.abs(x)))


def _head_cols(base, hd):
    return slice(base + hd * HEAD_DIM, base + (hd + 1) * HEAD_DIM)


def _gmlp_branch(xn, w_ref, lng_ref, lnb_ref, wsp_ref, bspt_ref, ya_ref):
    tm = xn.shape[0]
    strips = range(0, GMLP_WIDTH, STRIP_COLS)

    def proj(base, c0):
        return _dot_nt(xn, w_ref[base + c0:base + c0 + STRIP_COLS, :])

    v = jnp.concatenate([_gelu_tanh(proj(GMLP_WIDTH, c0)) for c0 in strips], axis=1)
    mu = jnp.mean(v, axis=-1, keepdims=True)
    vc = v - mu
    var = jnp.mean(vc * vc, axis=-1, keepdims=True)
    vln = _bf(vc * lax.rsqrt(var + EPS) * lng_ref[...] + lnb_ref[...])
    uz = jnp.concatenate([_gelu_tanh(proj(0, c0)) * _silu(proj(2 * GMLP_WIDTH, c0)) for c0 in strips], axis=1)
    ri = lax.broadcasted_iota(jnp.int32, (SPATIAL_BLOCK, SPATIAL_BLOCK), 0) // SPATIAL_CHUNK
    ci = lax.broadcasted_iota(jnp.int32, (SPATIAL_BLOCK, SPATIAL_BLOCK), 1) // SPATIAL_CHUNK
    chunk_causal = ci <= ri
    for g in range(GMLP_GROUPS):
        gs = slice(g * GROUP_DIM, (g + 1) * GROUP_DIM)
        wm = jnp.where(chunk_causal, wsp_ref[g], jnp.zeros_like(wsp_ref[g]))
        bias = bspt_ref[:, g:g + 1]
        for r in range(tm // SPATIAL_BLOCK):
            rs = slice(r * SPATIAL_BLOCK, (r + 1) * SPATIAL_BLOCK)
            mixed = _dot(wm, vln[rs, gs]) + bias
            ya_ref[rs, gs] = (uz[rs, gs] * mixed).astype(ya_ref.dtype)


def _gdn_prepare(xn, w_ref, wab_ref, wabt_ref, cw_ref, alog_c_ref, dtb_c_ref, alog_r_ref, dtb_r_ref,
                 cs_ref, qkv_ref, zb_ref, col_ref, row_ref):
    tm = xn.shape[0]
    sub = lax.broadcasted_iota(jnp.int32, (CONV_PAD, STRIP_COLS), 0)
    for c0 in range(0, 3 * GDN_WIDTH, STRIP_COLS):
        cols = slice(c0, c0 + STRIP_COLS)
        x = _dot_nt(xn, w_ref[3 * GMLP_WIDTH + c0:3 * GMLP_WIDTH + c0 + STRIP_COLS, :])
        prev = cs_ref[:, cols]
        acc = x * cw_ref[CONV_K - 1:CONV_K, cols]
        for shift in range(1, CONV_K):
            xs = pltpu.roll(x, shift, axis=0)
            head = jnp.where(sub < shift, pltpu.roll(prev, shift, axis=0), xs[0:CONV_PAD])
            shifted = jnp.concatenate([head, xs[CONV_PAD:]], axis=0)
            acc = acc + shifted * cw_ref[CONV_K - 1 - shift:CONV_K - shift, cols]
        cs_ref[:, cols] = x[tm - CONV_PAD:tm]
        act = _silu(acc)
        if c0 >= 2 * GDN_WIDTH:
            qkv_ref[:, cols] = _bf(act)
        else:
            scale = HEAD_DIM ** -0.5 if c0 < GDN_WIDTH else 1.0
            for hd in range(STRIP_COLS // HEAD_DIM):
                t = act[:, _head_cols(0, hd)]
                t = t * (lax.rsqrt(jnp.sum(t * t, axis=-1, keepdims=True) + EPS) * scale)
                qkv_ref[:, _head_cols(c0, hd)] = _bf(t)

    for c0 in range(0, GDN_WIDTH, STRIP_COLS):
        zb_ref[:, c0:c0 + STRIP_COLS] = _bf(_silu(_dot_nt(
            xn, w_ref[3 * GMLP_WIDTH + 3 * GDN_WIDTH + c0:3 * GMLP_WIDTH + 3 * GDN_WIDTH + c0 + STRIP_COLS, :])))

    ab = _dot(xn, wab_ref[...])
    g_col = -jnp.exp(alog_c_ref[...]) * _softplus(ab[:, 0:GDN_HEADS] + dtb_c_ref[...])
    beta = jax.nn.sigmoid(ab[:, GDN_HEADS:2 * GDN_HEADS])
    a_row = _dot_nt(wabt_ref[...], xn)
    g_row = -jnp.exp(alog_r_ref[...]) * _softplus(a_row + dtb_r_ref[...])
    ii = lax.broadcasted_iota(jnp.int32, (GDN_CHUNK, GDN_CHUNK), 0)
    jj = lax.broadcasted_iota(jnp.int32, (GDN_CHUNK, GDN_CHUNK), 1)
    tri_lower = (jj <= ii).astype(F32)
    tri_upper = (ii <= jj).astype(F32)
    for r in range(tm // GDN_CHUNK):
        rs = slice(r * GDN_CHUNK, (r + 1) * GDN_CHUNK)
        col_ref[rs, 0:GDN_HEADS] = jnp.dot(tri_lower, g_col[rs, :], precision=lax.Precision.HIGHEST,
                                           preferred_element_type=F32)
        col_ref[rs, GDN_HEADS:2 * GDN_HEADS] = beta[rs, :]
        row_ref[:, rs] = jnp.dot(g_row[:, rs], tri_upper, precision=lax.Precision.HIGHEST,
                                 preferred_element_type=F32)


def _gdn_chunks(qkv_ref, zb_ref, col_ref, row_ref, gnw_ref, state_ref, yb_ref):
    c = GDN_CHUNK
    n_chunks = qkv_ref.shape[0] // c
    ii = lax.broadcasted_iota(jnp.int32, (c, c), 0)
    jj = lax.broadcasted_iota(jnp.int32, (c, c), 1)
    causal = jj <= ii
    strict = jj < ii
    eye = (ii == jj).astype(F32)
    units = [(j, hd) for j in range(n_chunks) for hd in range(GDN_HEADS)]
    n_units = range(len(units))

    def rows(j):
        return slice(j * c, (j + 1) * c)

    dcols, betas, e_dec, e_rem, e_last = [], [], [], [], []
    for j in range(n_chunks):
        col = col_ref[rows(j), :]
        d = col[:, 0:GDN_HEADS]
        dlast = d[c - 1:c, :]
        dcols.append(d)
        betas.append(col[:, GDN_HEADS:2 * GDN_HEADS])
        e_dec.append(jnp.exp(d))
        e_rem.append(jnp.exp(dlast - d))
        e_last.append(jnp.exp(dlast))

    def colv(per_chunk, u):
        j, hd = units[u]
        return per_chunk[j][:, hd:hd + 1]

    q = [qkv_ref[rows(j), _head_cols(0, hd)] for j, hd in units]
    k = [qkv_ref[rows(j), _head_cols(GDN_WIDTH, hd)] for j, hd in units]
    kf = [k[u].astype(F32) for u in n_units]
    kb = [kf[u] * colv(betas, u) for u in n_units]
    gram = [_dot_nt(jnp.concatenate([_bf(kb[u]), q[u]], axis=0), k[u]) for u in n_units]
    lmat = [jnp.exp(jnp.where(causal, colv(dcols, u) - row_ref[hd:hd + 1, rows(j)], -jnp.inf))
            for u, (j, hd) in enumerate(units)]
    a = [jnp.where(strict, gram[u][0:c] * lmat[u], 0.0) for u in n_units]
    qk = [_bf(gram[u][c:2 * c] * lmat[u]) for u in n_units]
    a_b = [_bf(a[u]) for u in n_units]
    tb = [_bf(eye) - jnp.where((ii // 2) == (jj // 2), a_b[u], jnp.zeros_like(a_b[u])) for u in n_units]
    b = 2
    while b < c:
        off = _bf((((ii // (2 * b)) == (jj // (2 * b))) & ((ii // b) != (jj // b))).astype(F32))
        if b % BF16_ROWS == 0:
            lower = [slice(s0 + b, s0 + 2 * b) for s0 in range(0, c, 2 * b)]
            m = [_dot(jnp.concatenate([tb[u][rs] for rs in lower], axis=0), a_b[u] * off) for u in n_units]
            upd = [_bf(_dot(_bf(m[u]), tb[u])) for u in n_units]
            tb = [jnp.concatenate(
                [piece for n, rs in enumerate(lower)
                 for piece in (tb[u][rs.start - b:rs.start], tb[u][rs] - upd[u][n * b:(n + 1) * b])], axis=0)
                 for u in n_units]
        else:
            m = [_dot(tb[u], a_b[u] * off) for u in n_units]
            tb = [tb[u] - _bf(_dot(_bf(m[u]), tb[u])) for u in n_units]
        b *= 2
    r = [jnp.concatenate([qkv_ref[rows(j), _head_cols(2 * GDN_WIDTH, hd)].astype(F32) * colv(betas, u),
                          kb[u] * colv(e_dec, u)], axis=1) for u, (j, hd) in enumerate(units)]
    uw = [_dot(tb[u], _bf(r[u])) for u in n_units]
    w_b = [_bf(uw[u][:, HEAD_DIM:2 * HEAD_DIM]) for u in n_units]
    q_qk = [jnp.concatenate([_bf(q[u].astype(F32) * colv(e_dec, u)), qk[u]], axis=1) for u in n_units]
    kd = [_bf(kf[u] * colv(e_rem, u)) for u in n_units]
    s = [state_ref[hd] for hd in range(GDN_HEADS)]
    for j in range(n_chunks):
        us = [j * GDN_HEADS + hd for hd in range(GDN_HEADS)]
        s_b = [_bf(s[hd]) for hd in range(GDN_HEADS)]
        v_new = [_bf(uw[u][:, 0:HEAD_DIM] - _dot(w_b[u], s_b[hd])) for hd, u in enumerate(us)]
        o = [_dot(q_qk[u], jnp.concatenate([s_b[hd], v_new[hd]], axis=0)) for hd, u in enumerate(us)]
        s = [s[hd] * colv(e_last, u) + _dot_tn(kd[u], v_new[hd]) for hd, u in enumerate(us)]
        for hd in range(GDN_HEADS):
            on = o[hd] * lax.rsqrt(jnp.mean(o[hd] * o[hd], axis=-1, keepdims=True) + EPS) * gnw_ref[...]
            yb_ref[rows(j), _head_cols(0, hd)] = (
                on * zb_ref[rows(j), _head_cols(0, hd)].astype(F32)).astype(yb_ref.dtype)
    for hd in range(GDN_HEADS):
        state_ref[hd] = s[hd]


def _mix_kernel(tiles_per_seq,
                h_ref, nw_ref, w_ref, wab_ref, wabt_ref, lng_ref, lnb_ref, wsp_ref, bspt_ref,
                cw_ref, alog_c_ref, dtb_c_ref, alog_r_ref, dtb_r_ref, gnw_ref,
                ya_ref, yb_ref,
                cs_ref, state_ref, qkv_ref, zb_ref, col_ref, row_ref):
    @pl.when(pl.program_id(0) % tiles_per_seq == 0)
    def _():
        cs_ref[...] = jnp.zeros(cs_ref.shape, F32)
        state_ref[...] = jnp.zeros(state_ref.shape, F32)

    xn = _bf(_rms(h_ref[...], nw_ref[...]))
    _gmlp_branch(xn, w_ref, lng_ref, lnb_ref, wsp_ref, bspt_ref, ya_ref)
    _gdn_prepare(xn, w_ref, wab_ref, wabt_ref, cw_ref, alog_c_ref, dtb_c_ref, alog_r_ref, dtb_r_ref,
                 cs_ref, qkv_ref, zb_ref, col_ref, row_ref)
    for r0 in range(0, h_ref.shape[0], GDN_GROUP_ROWS):
        rs = slice(r0, r0 + GDN_GROUP_ROWS)
        _gdn_chunks(qkv_ref.at[rs], zb_ref.at[rs], col_ref.at[rs], row_ref.at[:, rs], gnw_ref, state_ref,
                    yb_ref.at[rs])


def _layer_spec(layer, shape):
    return pl.BlockSpec((None,) + shape, lambda i: (layer,) + (0,) * len(shape),
                        pipeline_mode=pl.Buffered(1))


def _mix(layer, h, nw, w_main, wab, wabt, lng, lnb, wsp, bspt, cw, alog_c, dtb_c, alog_r, dtb_r, gnw, seq):
    t = h.shape[0]
    tm = TILE_MIX
    spec = functools.partial(_layer_spec, layer)
    return pl.pallas_call(
        functools.partial(_mix_kernel, seq // tm),
        out_shape=(
            jax.ShapeDtypeStruct((t, GMLP_WIDTH), BF16),
            jax.ShapeDtypeStruct((t, GDN_WIDTH), BF16),
        ),
        grid=(t // tm,),
        in_specs=[
            pl.BlockSpec((tm, D_MODEL), lambda i: (i, 0)),
            spec((1, D_MODEL)),
            spec((MAIN_COLS, D_MODEL)),
            spec((D_MODEL, 2 * GDN_HEADS)),
            spec((GDN_HEADS, D_MODEL)),
            spec((1, GMLP_WIDTH)),
            spec((1, GMLP_WIDTH)),
            spec((GMLP_GROUPS, SPATIAL_BLOCK, SPATIAL_BLOCK)),
            spec((SPATIAL_BLOCK, GMLP_GROUPS)),
            spec((CONV_K, 3 * GDN_WIDTH)),
            spec((1, GDN_HEADS)),
            spec((1, GDN_HEADS)),
            spec((GDN_HEADS, 1)),
            spec((GDN_HEADS, 1)),
            spec((1, HEAD_DIM)),
        ],
        out_specs=(
            pl.BlockSpec((tm, GMLP_WIDTH), lambda i: (i, 0)),
            pl.BlockSpec((tm, GDN_WIDTH), lambda i: (i, 0)),
        ),
        scratch_shapes=[
            pltpu.VMEM((CONV_PAD, 3 * GDN_WIDTH), F32),
            pltpu.VMEM((GDN_HEADS, HEAD_DIM, HEAD_DIM), F32),
            pltpu.VMEM((tm, 3 * GDN_WIDTH), BF16),
            pltpu.VMEM((tm, GDN_WIDTH), BF16),
            pltpu.VMEM((tm, 2 * GDN_HEADS), F32),
            pltpu.VMEM((GDN_HEADS, tm), F32),
        ],
        compiler_params=pltpu.CompilerParams(
            dimension_semantics=("arbitrary",), vmem_limit_bytes=VMEM_LIMIT_BYTES),
        name="mix",
    )(h, nw, w_main, wab, wabt, lng, lnb, wsp, bspt, cw, alog_c, dtb_c, alog_r, dtb_r, gnw)


def _out_ple_kernel(apply_final_norm,
                    h_ref, ya_ref, yb_ref, p_ref, wout_ref, wple_ref, pnw_ref, gnw_ref, wg_ref, fnw_ref,
                    o_ref):
    h1 = (h_ref[...] + _dot(ya_ref[...], wout_ref[0:GMLP_WIDTH, :])
          + _dot(yb_ref[...], wout_ref[GMLP_WIDTH:GMLP_WIDTH + GDN_WIDTH, :]))
    e = _rms(_dot(_bf(p_ref[...]), wple_ref[...]), pnw_ref[...])
    gate = jax.nn.sigmoid(_dot(_bf(_rms(h1, gnw_ref[...])), wg_ref[...]))
    h2 = h1 + gate * e
    if apply_final_norm:
        h2 = _rms(h2, fnw_ref[...])
    o_ref[...] = h2


def _out_ple(layer, h, ya, yb, p, wout, wple, pnw, gnw, wg, fnw, apply_final_norm):
    t = h.shape[0]
    tm = TILE_OUT
    spec = functools.partial(_layer_spec, layer)
    tok = lambda width: pl.BlockSpec((tm, width), lambda i: (i, 0))
    return pl.pallas_call(
        functools.partial(_out_ple_kernel, apply_final_norm),
        out_shape=jax.ShapeDtypeStruct((t, D_MODEL), F32),
        grid=(t // tm,),
        in_specs=[
            tok(D_MODEL), tok(GMLP_WIDTH), tok(GDN_WIDTH),
            pl.BlockSpec((None, tm, PLE_DIM), lambda i: (layer, i, 0)),
            spec((GMLP_WIDTH + GDN_WIDTH, D_MODEL)),
            spec((PLE_DIM, D_MODEL)),
            spec((1, D_MODEL)),
            spec((1, D_MODEL)),
            spec((D_MODEL, D_MODEL)),
            pl.BlockSpec((1, D_MODEL), lambda i: (0, 0)),
        ],
        out_specs=tok(D_MODEL),
        compiler_params=pltpu.CompilerParams(
            dimension_semantics=("arbitrary",), vmem_limit_bytes=VMEM_LIMIT_BYTES),
        name="out_ple",
    )(h, ya, yb, p, wout, wple, pnw, gnw, wg, fnw)


def kernel(x, p, norm_w, w_in, ln_v_g, ln_v_b, w_spatial, b_spatial, conv_w, A_log, dt_bias,
           gdn_norm_w, w_out, w_ple, ple_norm_w, ple_gate_norm_w, w_ple_gate, final_norm_w):
    batch, seq, d = x.shape
    depth = w_in.shape[0]
    assert d == D_MODEL and seq % TILE_MIX == 0 and (batch * seq) % TILE_OUT == 0
    assert w_in.shape[2] == MAIN_COLS + 2 * GDN_HEADS
    t = batch * seq
    h = x.reshape(t, d)
    p2 = p.reshape(depth, t, PLE_DIM)
    w_in_t = _bf(jnp.swapaxes(w_in, 1, 2))
    wabt = w_in_t[:, MAIN_COLS:MAIN_COLS + GDN_HEADS, :]
    wab = jnp.swapaxes(w_in_t[:, MAIN_COLS:, :], 1, 2)
    wsp_b, wout_b, wple_b, wg_b = _bf(w_spatial), _bf(w_out), _bf(w_ple), _bf(w_ple_gate)
    bspt = jnp.swapaxes(b_spatial, 1, 2)
    row3 = lambda a: a[:, None, :]
    col3 = lambda a: a[:, :, None]
    for i in range(depth):
        ya, yb = _mix(i, h, row3(norm_w), w_in_t, wab, wabt, row3(ln_v_g), row3(ln_v_b), wsp_b, bspt,
                      conv_w, row3(A_log), row3(dt_bias), col3(A_log), col3(dt_bias), row3(gdn_norm_w), seq)
        h = _out_ple(i, h, ya, yb, p2, wout_b, wple_b, row3(ple_norm_w), row3(ple_gate_norm_w), wg_b,
                     final_norm_w[None], apply_final_norm=(i == depth - 1))
    return h.reshape(batch, seq, d)
```

```python
import functools

import jax
import jax.numpy as jnp
from jax import lax
from jax.experimental import pallas as pl
from jax.experimental.pallas import tpu as pltpu

D_MODEL = 1024
GMLP_WIDTH = 1024
GMLP_GROUPS = 8
GROUP_DIM = 128
SPATIAL_BLOCK = 128
SPATIAL_CHUNK = 64
GDN_WIDTH = 1024
GDN_HEADS = 8
HEAD_DIM = 128
CONV_K = 4
PLE_DIM = 256
EPS = 1e-6
MAIN_COLS = 3 * GMLP_WIDTH + 4 * GDN_WIDTH

GDN_CHUNK = 128
TILE_MIX = 1024
TILE_OUT = 512
GDN_GROUP_ROWS = 256
STRIP_COLS = 256
BF16_ROWS = 16
CONV_PAD = 8
VMEM_LIMIT_BYTES = 56 * 1024 * 1024

F32 = jnp.float32
BF16 = jnp.bfloat16
LOG2E = 1.4426950408889634
GELU_K1 = -2.0 * 0.7978845608028654 * LOG2E
GELU_K3 = GELU_K1 * 0.044715


def _bf(x):
    return x.astype(BF16)


def _dot(a, b):
    return jnp.dot(a, b, preferred_element_type=F32)


def _dot_nt(a, b):
    return lax.dot_general(a, b, (((1,), (1,)), ((), ())), preferred_element_type=F32)


def _dot_tn(a, b):
    return lax.dot_general(a, b, (((0,), (0,)), ((), ())), preferred_element_type=F32)


def _rms(x, w):
    return x * lax.rsqrt(jnp.mean(x * x, axis=-1, keepdims=True) + EPS) * w


def _silu(x):
    return x / (1.0 + jnp.exp2(x * -LOG2E))


def _gelu_tanh(x):
    return x / (1.0 + jnp.exp2(x * (GELU_K1 + GELU_K3 * (x * x))))


def _softplus(x):
    return jnp.maximum(x, 0.0) + jnp.log1p(jnp.exp(-jnp.abs(x)))


def _head_cols(base, hd):
    return slice(base + hd * HEAD_DIM, base + (hd + 1) * HEAD_DIM)


def _gmlp_branch(xn, w_ref, lng_ref, lnb_ref, wsp_ref, bspt_ref, ya_ref):
    tm = xn.shape[0]
    strips = range(0, GMLP_WIDTH, STRIP_COLS)

    def proj(base, c0):
        return _dot_nt(xn, w_ref[base + c0:base + c0 + STRIP_COLS, :])

    v = jnp.concatenate([_gelu_tanh(proj(GMLP_WIDTH, c0)) for c0 in strips], axis=1)
    mu = jnp.mean(v, axis=-1, keepdims=True)
    vc = v - mu
    var = jnp.mean(vc * vc, axis=-1, keepdims=True)
    vln = _bf(vc * lax.rsqrt(var + EPS) * lng_ref[...] + lnb_ref[...])
    uz = jnp.concatenate([_gelu_tanh(proj(0, c0)) * _silu(proj(2 * GMLP_WIDTH, c0)) for c0 in strips], axis=1)
    ri = lax.broadcasted_iota(jnp.int32, (SPATIAL_BLOCK, SPATIAL_BLOCK), 0) // SPATIAL_CHUNK
    ci = lax.broadcasted_iota(jnp.int32, (SPATIAL_BLOCK, SPATIAL_BLOCK), 1) // SPATIAL_CHUNK
    chunk_causal = ci <= ri
    for g in range(GMLP_GROUPS):
        gs = slice(g * GROUP_DIM, (g + 1) * GROUP_DIM)
        wm = jnp.where(chunk_causal, wsp_ref[g], jnp.zeros_like(wsp_ref[g]))
        bias = bspt_ref[:, g:g + 1]
        for r in range(tm // SPATIAL_BLOCK):
            rs = slice(r * SPATIAL_BLOCK, (r + 1) * SPATIAL_BLOCK)
            mixed = _dot(wm, vln[rs, gs]) + bias
            ya_ref[rs, gs] = (uz[rs, gs] * mixed).astype(ya_ref.dtype)


def _gdn_prepare(xn, w_ref, wab_ref, wabt_ref, cw_ref, alog_c_ref, dtb_c_ref, alog_r_ref, dtb_r_ref,
                 cs_ref, qkv_ref, zb_ref, col_ref, row_ref):
    tm = xn.shape[0]
    sub = lax.broadcasted_iota(jnp.int32, (CONV_PAD, STRIP_COLS), 0)
    for c0 in range(0, 3 * GDN_WIDTH, STRIP_COLS):
        cols = slice(c0, c0 + STRIP_COLS)
        x = _dot_nt(xn, w_ref[3 * GMLP_WIDTH + c0:3 * GMLP_WIDTH + c0 + STRIP_COLS, :])
        prev = cs_ref[:, cols]
        acc = x * cw_ref[CONV_K - 1:CONV_K, cols]
        for shift in range(1, CONV_K):
            xs = pltpu.roll(x, shift, axis=0)
            head = jnp.where(sub < shift, pltpu.roll(prev, shift, axis=0), xs[0:CONV_PAD])
            shifted = jnp.concatenate([head, xs[CONV_PAD:]], axis=0)
            acc = acc + shifted * cw_ref[CONV_K - 1 - shift:CONV_K - shift, cols]
        cs_ref[:, cols] = x[tm - CONV_PAD:tm]
        act = _silu(acc)
        if c0 >= 2 * GDN_WIDTH:
            qkv_ref[:, cols] = _bf(act)
        else:
            scale = HEAD_DIM ** -0.5 if c0 < GDN_WIDTH else 1.0
            for hd in range(STRIP_COLS // HEAD_DIM):
                t = act[:, _head_cols(0, hd)]
                t = t * (lax.rsqrt(jnp.sum(t * t, axis=-1, keepdims=True) + EPS) * scale)
                qkv_ref[:, _head_cols(c0, hd)] = _bf(t)

    for c0 in range(0, GDN_WIDTH, STRIP_COLS):
        zb_ref[:, c0:c0 + STRIP_COLS] = _bf(_silu(_dot_nt(
            xn, w_ref[3 * GMLP_WIDTH + 3 * GDN_WIDTH + c0:3 * GMLP_WIDTH + 3 * GDN_WIDTH + c0 + STRIP_COLS, :])))

    ab = _dot(xn, wab_ref[...])
    g_col = -jnp.exp(alog_c_ref[...]) * _softplus(ab[:, 0:GDN_HEADS] + dtb_c_ref[...])
    beta = jax.nn.sigmoid(ab[:, GDN_HEADS:2 * GDN_HEADS])
    a_row = _dot_nt(wabt_ref[...], xn)
    g_row = -jnp.exp(alog_r_ref[...]) * _softplus(a_row + dtb_r_ref[...])
    ii = lax.broadcasted_iota(jnp.int32, (GDN_CHUNK, GDN_CHUNK), 0)
    jj = lax.broadcasted_iota(jnp.int32, (GDN_CHUNK, GDN_CHUNK), 1)
    tri_lower = (jj <= ii).astype(F32)
    tri_upper = (ii <= jj).astype(F32)
    for r in range(tm // GDN_CHUNK):
        rs = slice(r * GDN_CHUNK, (r + 1) * GDN_CHUNK)
        col_ref[rs, 0:GDN_HEADS] = jnp.dot(tri_lower, g_col[rs, :], precision=lax.Precision.HIGHEST,
                                           preferred_element_type=F32)
        col_ref[rs, GDN_HEADS:2 * GDN_HEADS] = beta[rs, :]
        row_ref[:, rs] = jnp.dot(g_row[:, rs], tri_upper, precision=lax.Precision.HIGHEST,
                                 preferred_element_type=F32)


def _gdn_chunks(qkv_ref, zb_ref, col_ref, row_ref, gnw_ref, state_ref, yb_ref):
    c = GDN_CHUNK
    n_chunks = qkv_ref.shape[0] // c
    ii = lax.broadcasted_iota(jnp.int32, (c, c), 0)
    jj = lax.broadcasted_iota(jnp.int32, (c, c), 1)
    causal = jj <= ii
    strict = jj < ii
    eye = (ii == jj).astype(F32)
    units = [(j, hd) for j in range(n_chunks) for hd in range(GDN_HEADS)]
    n_units = range(len(units))

    def rows(j):
        return slice(j * c, (j + 1) * c)

    dcols, betas, e_dec, e_rem, e_last = [], [], [], [], []
    for j in range(n_chunks):
        col = col_ref[rows(j), :]
        d = col[:, 0:GDN_HEADS]
        dlast = d[c - 1:c, :]
        dcols.append(d)
        betas.append(col[:, GDN_HEADS:2 * GDN_HEADS])
        e_dec.append(jnp.exp(d))
        e_rem.append(jnp.exp(dlast - d))
        e_last.append(jnp.exp(dlast))

    def colv(per_chunk, u):
        j, hd = units[u]
        return per_chunk[j][:, hd:hd + 1]

    q = [qkv_ref[rows(j), _head_cols(0, hd)] for j, hd in units]
    k = [qkv_ref[rows(j), _head_cols(GDN_WIDTH, hd)] for j, hd in units]
    kf = [k[u].astype(F32) for u in n_units]
    kb = [kf[u] * colv(betas, u) for u in n_units]
    gram = [_dot_nt(jnp.concatenate([_bf(kb[u]), q[u]], axis=0), k[u]) for u in n_units]
    lmat = [jnp.exp(jnp.where(causal, colv(dcols, u) - row_ref[hd:hd + 1, rows(j)], -jnp.inf))
            for u, (j, hd) in enumerate(units)]
    a = [jnp.where(strict, gram[u][0:c] * lmat[u], 0.0) for u in n_units]
    qk = [_bf(gram[u][c:2 * c] * lmat[u]) for u in n_units]
    a_b = [_bf(a[u]) for u in n_units]
    tb = [_bf(eye) - jnp.where((ii // 2) == (jj // 2), a_b[u], jnp.zeros_like(a_b[u])) for u in n_units]
    b = 2
    while b < c:
        off = _bf((((ii // (2 * b)) == (jj // (2 * b))) & ((ii // b) != (jj // b))).astype(F32))
        if b % BF16_ROWS == 0:
            lower = [slice(s0 + b, s0 + 2 * b) for s0 in range(0, c, 2 * b)]
            m = [_dot(jnp.concatenate([tb[u][rs] for rs in lower], axis=0), a_b[u] * off) for u in n_units]
            upd = [_bf(_dot(_bf(m[u]), tb[u])) for u in n_units]
            tb = [jnp.concatenate(
                [piece for n, rs in enumerate(lower)
                 for piece in (tb[u][rs.start - b:rs.start], tb[u][rs] - upd[u][n * b:(n + 1) * b])], axis=0)
                 for u in n_units]
        else:
            m = [_dot(tb[u], a_b[u] * off) for u in n_units]
            tb = [tb[u] - _bf(_dot(_bf(m[u]), tb[u])) for u in n_units]
        b *= 2
    r = [jnp.concatenate([qkv_ref[rows(j), _head_cols(2 * GDN_WIDTH, hd)].astype(F32) * colv(betas, u),
                          kb[u] * colv(e_dec, u)], axis=1) for u, (j, hd) in enumerate(units)]
    uw = [_dot(tb[u], _bf(r[u])) for u in n_units]
    w_b = [_bf(uw[u][:, HEAD_DIM:2 * HEAD_DIM]) for u in n_units]
    q_qk = [jnp.concatenate([_bf(q[u].astype(F32) * colv(e_dec, u)), qk[u]], axis=1) for u in n_units]
    kd = [_bf(kf[u] * colv(e_rem, u)) for u in n_units]
    s = [state_ref[hd] for hd in range(GDN_HEADS)]
    for j in range(n_chunks):
        us = [j * GDN_HEADS + hd for hd in range(GDN_HEADS)]
        s_b = [_bf(s[hd]) for hd in range(GDN_HEADS)]
        v_new = [_bf(uw[u][:, 0:HEAD_DIM] - _dot(w_b[u], s_b[hd])) for hd, u in enumerate(us)]
        o = [_dot(q_qk[u], jnp.concatenate([s_b[hd], v_new[hd]], axis=0)) for hd, u in enumerate(us)]
        s = [s[hd] * colv(e_last, u) + _dot_tn(kd[u], v_new[hd]) for hd, u in enumerate(us)]
        for hd in range(GDN_HEADS):
            on = o[hd] * lax.rsqrt(jnp.mean(o[hd] * o[hd], axis=-1, keepdims=True) + EPS) * gnw_ref[...]
            yb_ref[rows(j), _head_cols(0, hd)] = (
                on * zb_ref[rows(j), _head_cols(0, hd)].astype(F32)).astype(yb_ref.dtype)
    for hd in range(GDN_HEADS):
        state_ref[hd] = s[hd]


def _mix_kernel(tiles_per_seq,
                h_ref, nw_ref, w_ref, wab_ref, wabt_ref, lng_ref, lnb_ref, wsp_ref, bspt_ref,
                cw_ref, alog_c_ref, dtb_c_ref, alog_r_ref, dtb_r_ref, gnw_ref,
                ya_ref, yb_ref,
                cs_ref, state_ref, qkv_ref, zb_ref, col_ref, row_ref):
    @pl.when(pl.program_id(0) % tiles_per_seq == 0)
    def _():
        cs_ref[...] = jnp.zeros(cs_ref.shape, F32)
        state_ref[...] = jnp.zeros(state_ref.shape, F32)

    xn = _bf(_rms(h_ref[...], nw_ref[...]))
    _gmlp_branch(xn, w_ref, lng_ref, lnb_ref, wsp_ref, bspt_ref, ya_ref)
    _gdn_prepare(xn, w_ref, wab_ref, wabt_ref, cw_ref, alog_c_ref, dtb_c_ref, alog_r_ref, dtb_r_ref,
                 cs_ref, qkv_ref, zb_ref, col_ref, row_ref)
    for r0 in range(0, h_ref.shape[0], GDN_GROUP_ROWS):
        rs = slice(r0, r0 + GDN_GROUP_ROWS)
        _gdn_chunks(qkv_ref.at[rs], zb_ref.at[rs], col_ref.at[rs], row_ref.at[:, rs], gnw_ref, state_ref,
                    yb_ref.at[rs])


def _layer_spec(layer, shape):
    return pl.BlockSpec((None,) + shape, lambda i: (layer,) + (0,) * len(shape),
                        pipeline_mode=pl.Buffered(1))


def _mix(layer, h, nw, w_main, wab, wabt, lng, lnb, wsp, bspt, cw, alog_c, dtb_c, alog_r, dtb_r, gnw, seq):
    t = h.shape[0]
    tm = TILE_MIX
    spec = functools.partial(_layer_spec, layer)
    return pl.pallas_call(
        functools.partial(_mix_kernel, seq // tm),
        out_shape=(
            jax.ShapeDtypeStruct((t, GMLP_WIDTH), BF16),
            jax.ShapeDtypeStruct((t, GDN_WIDTH), BF16),
        ),
        grid=(t // tm,),
        in_specs=[
            pl.BlockSpec((tm, D_MODEL), lambda i: (i, 0)),
            spec((1, D_MODEL)),
            spec((MAIN_COLS, D_MODEL)),
            spec((D_MODEL, 2 * GDN_HEADS)),
            spec((GDN_HEADS, D_MODEL)),
            spec((1, GMLP_WIDTH)),
            spec((1, GMLP_WIDTH)),
            spec((GMLP_GROUPS, SPATIAL_BLOCK, SPATIAL_BLOCK)),
            spec((SPATIAL_BLOCK, GMLP_GROUPS)),
            spec((CONV_K, 3 * GDN_WIDTH)),
            spec((1, GDN_HEADS)),
            spec((1, GDN_HEADS)),
            spec((GDN_HEADS, 1)),
            spec((GDN_HEADS, 1)),
            spec((1, HEAD_DIM)),
        ],
        out_specs=(
            pl.BlockSpec((tm, GMLP_WIDTH), lambda i: (i, 0)),
            pl.BlockSpec((tm, GDN_WIDTH), lambda i: (i, 0)),
        ),
        scratch_shapes=[
            pltpu.VMEM((CONV_PAD, 3 * GDN_WIDTH), F32),
            pltpu.VMEM((GDN_HEADS, HEAD_DIM, HEAD_DIM), F32),
            pltpu.VMEM((tm, 3 * GDN_WIDTH), BF16),
            pltpu.VMEM((tm, GDN_WIDTH), BF16),
            pltpu.VMEM((tm, 2 * GDN_HEADS), F32),
            pltpu.VMEM((GDN_HEADS, tm), F32),
        ],
        compiler_params=pltpu.CompilerParams(
            dimension_semantics=("arbitrary",), vmem_limit_bytes=VMEM_LIMIT_BYTES),
        name="mix",
    )(h, nw, w_main, wab, wabt, lng, lnb, wsp, bspt, cw, alog_c, dtb_c, alog_r, dtb_r, gnw)


def _out_ple_kernel(apply_final_norm,
                    h_ref, ya_ref, yb_ref, p_ref, wout_ref, wple_ref, pnw_ref, gnw_ref, wg_ref, fnw_ref,
                    o_ref):
    h1 = (h_ref[...] + _dot(ya_ref[...], wout_ref[0:GMLP_WIDTH, :])
          + _dot(yb_ref[...], wout_ref[GMLP_WIDTH:GMLP_WIDTH + GDN_WIDTH, :]))
    e = _rms(_dot(_bf(p_ref[...]), wple_ref[...]), pnw_ref[...])
    gate = jax.nn.sigmoid(_dot(_bf(_rms(h1, gnw_ref[...])), wg_ref[...]))
    h2 = h1 + gate * e
    if apply_final_norm:
        h2 = _rms(h2, fnw_ref[...])
    o_ref[...] = h2


def _out_ple(layer, h, ya, yb, p, wout, wple, pnw, gnw, wg, fnw, apply_final_norm):
    t = h.shape[0]
    tm = TILE_OUT
    spec = functools.partial(_layer_spec, layer)
    tok = lambda width: pl.BlockSpec((tm, width), lambda i: (i, 0))
    return pl.pallas_call(
        functools.partial(_out_ple_kernel, apply_final_norm),
        out_shape=jax.ShapeDtypeStruct((t, D_MODEL), F32),
        grid=(t // tm,),
        in_specs=[
            tok(D_MODEL), tok(GMLP_WIDTH), tok(GDN_WIDTH),
            pl.BlockSpec((None, tm, PLE_DIM), lambda i: (layer, i, 0)),
            spec((GMLP_WIDTH + GDN_WIDTH, D_MODEL)),
            spec((PLE_DIM, D_MODEL)),
            spec((1, D_MODEL)),
            spec((1, D_MODEL)),
            spec((D_MODEL, D_MODEL)),
            pl.BlockSpec((1, D_MODEL), lambda i: (0, 0)),
        ],
        out_specs=tok(D_MODEL),
        compiler_params=pltpu.CompilerParams(
            dimension_semantics=("arbitrary",), vmem_limit_bytes=VMEM_LIMIT_BYTES),
        name="out_ple",
    )(h, ya, yb, p, wout, wple, pnw, gnw, wg, fnw)


def kernel(x, p, norm_w, w_in, ln_v_g, ln_v_b, w_spatial, b_spatial, conv_w, A_log, dt_bias,
           gdn_norm_w, w_out, w_ple, ple_norm_w, ple_gate_norm_w, w_ple_gate, final_norm_w):
    batch, seq, d = x.shape
    depth = w_in.shape[0]
    assert d == D_MODEL and seq % TILE_MIX == 0 and (batch * seq) % TILE_OUT == 0
    assert w_in.shape[2] == MAIN_COLS + 2 * GDN_HEADS
    t = batch * seq
    h = x.reshape(t, d)
    p2 = p.reshape(depth, t, PLE_DIM)
    w_in_t = _bf(jnp.swapaxes(w_in, 1, 2))
    wabt = w_in_t[:, MAIN_COLS:MAIN_COLS + GDN_HEADS, :]
    wab = jnp.swapaxes(w_in_t[:, MAIN_COLS:, :], 1, 2)
    wsp_b, wout_b, wple_b, wg_b = _bf(w_spatial), _bf(w_out), _bf(w_ple), _bf(w_ple_gate)
    bspt = jnp.swapaxes(b_spatial, 1, 2)
    row3 = lambda a: a[:, None, :]
    col3 = lambda a: a[:, :, None]
    for i in range(depth):
        ya, yb = _mix(i, h, row3(norm_w), w_in_t, wab, wabt, row3(ln_v_g), row3(ln_v_b), wsp_b, bspt,
                      conv_w, row3(A_log), row3(dt_bias), col3(A_log), col3(dt_bias), row3(gdn_norm_w), seq)
        h = _out_ple(i, h, ya, yb, p2, wout_b, wple_b, row3(ple_norm_w), row3(ple_gate_norm_w), wg_b,
                     final_norm_w[None], apply_final_norm=(i == depth - 1))
    return h.reshape(batch, seq, d)
```

```python
import functools

import jax
import jax.numpy as jnp
from jax import lax
from jax.experimental import pallas as pl
from jax.experimental.pallas import tpu as pltpu

D_MODEL = 1024
GMLP_WIDTH = 1024
GMLP_GROUPS = 8
GROUP_DIM = 128
SPATIAL_BLOCK = 128
SPATIAL_CHUNK = 64
GDN_WIDTH = 1024
GDN_HEADS = 8
HEAD_DIM = 128
CONV_K = 4
PLE_DIM = 256
EPS = 1e-6
MAIN_COLS = 3 * GMLP_WIDTH + 4 * GDN_WIDTH

GDN_CHUNK = 128
TILE_MIX = 512
TILE_OUT = 512
GDN_GROUP_ROWS = 256
STRIP_COLS = 256
BF16_ROWS = 16
CONV_ROWS = 64
CONV_PAD = 8
VMEM_LIMIT_BYTES = 56 * 1024 * 1024

F32 = jnp.float32
BF16 = jnp.bfloat16
LOG2E = 1.4426950408889634
GELU_K1 = -2.0 * 0.7978845608028654 * LOG2E
GELU_K3 = GELU_K1 * 0.044715


def _bf(x):
    return x.astype(BF16)


def _dot(a, b):
    return jnp.dot(a, b, preferred_element_type=F32)


def _dot_nt(a, b):
    return lax.dot_general(a, b, (((1,), (1,)), ((), ())), preferred_element_type=F32)


def _dot_tn(a, b):
    return lax.dot_general(a, b, (((0,), (0,)), ((), ())), preferred_element_type=F32)


def _rms(x, w):
    return x * lax.rsqrt(jnp.mean(x * x, axis=-1, keepdims=True) + EPS) * w


def _silu(x):
    return x / (1.0 + jnp.exp2(x * -LOG2E))


def _gelu_tanh(x):
    return x / (1.0 + jnp.exp2(x * (GELU_K1 + GELU_K3 * (x * x))))


def _softplus(x):
    return jnp.maximum(x, 0.0) + jnp.log1p(jnp.exp(-jnp.abs(x)))


def _head_cols(base, hd):
    return slice(base + hd * HEAD_DIM, base + (hd + 1) * HEAD_DIM)


def _gmlp_branch(xn, w_ref, lng_ref, lnb_ref, wsp_ref, bspt_ref, ya_ref):
    tm = xn.shape[0]
    strips = range(0, GMLP_WIDTH, STRIP_COLS)

    def proj(base, c0):
        return _dot_nt(xn, w_ref[base + c0:base + c0 + STRIP_COLS, :])

    v = jnp.concatenate([_gelu_tanh(proj(GMLP_WIDTH, c0)) for c0 in strips], axis=1)
    mu = jnp.mean(v, axis=-1, keepdims=True)
    vc = v - mu
    var = jnp.mean(vc * vc, axis=-1, keepdims=True)
    vln = _bf(vc * lax.rsqrt(var + EPS) * lng_ref[...] + lnb_ref[...])
    uz = jnp.concatenate([_gelu_tanh(proj(0, c0)) * _silu(proj(2 * GMLP_WIDTH, c0)) for c0 in strips], axis=1)
    ri = lax.broadcasted_iota(jnp.int32, (SPATIAL_BLOCK, SPATIAL_BLOCK), 0) // SPATIAL_CHUNK
    ci = lax.broadcasted_iota(jnp.int32, (SPATIAL_BLOCK, SPATIAL_BLOCK), 1) // SPATIAL_CHUNK
    chunk_causal = ci <= ri
    for g in range(GMLP_GROUPS):
        gs = slice(g * GROUP_DIM, (g + 1) * GROUP_DIM)
        wm = jnp.where(chunk_causal, wsp_ref[g], jnp.zeros_like(wsp_ref[g]))
        bias = bspt_ref[:, g:g + 1]
        for r in range(tm // SPATIAL_BLOCK):
            rs = slice(r * SPATIAL_BLOCK, (r + 1) * SPATIAL_BLOCK)
            mixed = _dot(wm, vln[rs, gs]) + bias
            ya_ref[rs, gs] = (uz[rs, gs] * mixed).astype(ya_ref.dtype)


def _gdn_prepare(xn, w_ref, wab_ref, wabt_ref, cw_ref, alog_c_ref, dtb_c_ref, alog_r_ref, dtb_r_ref,
                 cs_ref, qkv_ref, zb_ref, col_ref, row_ref):
    tm = xn.shape[0]
    sub = lax.broadcasted_iota(jnp.int32, (CONV_PAD, STRIP_COLS), 0)
    for c0 in range(0, 3 * GDN_WIDTH, STRIP_COLS):
        cols = slice(c0, c0 + STRIP_COLS)
        x = _dot_nt(xn, w_ref[3 * GMLP_WIDTH + c0:3 * GMLP_WIDTH + c0 + STRIP_COLS, :])
        prev = cs_ref[:, cols]
        for r0 in range(0, tm, CONV_ROWS):
            rows = slice(r0, r0 + CONV_ROWS)
            xc = x[rows]
            acc = xc * cw_ref[CONV_K - 1:CONV_K, cols]
            for shift in range(1, CONV_K):
                xs = pltpu.roll(xc, shift, axis=0)
                head = jnp.where(sub < shift, pltpu.roll(prev, shift, axis=0), xs[0:CONV_PAD])
                shifted = jnp.concatenate([head, xs[CONV_PAD:]], axis=0)
                acc = acc + shifted * cw_ref[CONV_K - 1 - shift:CONV_K - shift, cols]
            prev = xc[CONV_ROWS - CONV_PAD:CONV_ROWS]
            act = _silu(acc)
            if c0 >= 2 * GDN_WIDTH:
                qkv_ref[rows, cols] = _bf(act)
            else:
                scale = HEAD_DIM ** -0.5 if c0 < GDN_WIDTH else 1.0
                for hd in range(STRIP_COLS // HEAD_DIM):
                    t = act[:, _head_cols(0, hd)]
                    t = t * (lax.rsqrt(jnp.sum(t * t, axis=-1, keepdims=True) + EPS) * scale)
                    qkv_ref[rows, _head_cols(c0, hd)] = _bf(t)
        cs_ref[:, cols] = prev

    for c0 in range(0, GDN_WIDTH, STRIP_COLS):
        zb_ref[:, c0:c0 + STRIP_COLS] = _bf(_silu(_dot_nt(
            xn, w_ref[3 * GMLP_WIDTH + 3 * GDN_WIDTH + c0:3 * GMLP_WIDTH + 3 * GDN_WIDTH + c0 + STRIP_COLS, :])))

    ab = _dot(xn, wab_ref[...])
    g_col = -jnp.exp(alog_c_ref[...]) * _softplus(ab[:, 0:GDN_HEADS] + dtb_c_ref[...])
    beta = jax.nn.sigmoid(ab[:, GDN_HEADS:2 * GDN_HEADS])
    a_row = _dot_nt(wabt_ref[...], xn)
    g_row = -jnp.exp(alog_r_ref[...]) * _softplus(a_row + dtb_r_ref[...])
    ii = lax.broadcasted_iota(jnp.int32, (GDN_CHUNK, GDN_CHUNK), 0)
    jj = lax.broadcasted_iota(jnp.int32, (GDN_CHUNK, GDN_CHUNK), 1)
    tri_lower = (jj <= ii).astype(F32)
    tri_upper = (ii <= jj).astype(F32)
    for r in range(tm // GDN_CHUNK):
        rs = slice(r * GDN_CHUNK, (r + 1) * GDN_CHUNK)
        col_ref[rs, 0:GDN_HEADS] = jnp.dot(tri_lower, g_col[rs, :], precision=lax.Precision.HIGHEST,
                                           preferred_element_type=F32)
        col_ref[rs, GDN_HEADS:2 * GDN_HEADS] = beta[rs, :]
        row_ref[:, rs] = jnp.dot(g_row[:, rs], tri_upper, precision=lax.Precision.HIGHEST,
                                 preferred_element_type=F32)


def _gdn_chunks(qkv_ref, zb_ref, col_ref, row_ref, gnw_ref, state_ref, yb_ref):
    c = GDN_CHUNK
    n_chunks = qkv_ref.shape[0] // c
    ii = lax.broadcasted_iota(jnp.int32, (c, c), 0)
    jj = lax.broadcasted_iota(jnp.int32, (c, c), 1)
    causal = jj <= ii
    strict = jj < ii
    eye = (ii == jj).astype(F32)
    units = [(j, hd) for j in range(n_chunks) for hd in range(GDN_HEADS)]
    n_units = range(len(units))

    def rows(j):
        return slice(j * c, (j + 1) * c)

    dcols, betas, e_dec, e_rem, e_last = [], [], [], [], []
    for j in range(n_chunks):
        col = col_ref[rows(j), :]
        d = col[:, 0:GDN_HEADS]
        dlast = d[c - 1:c, :]
        dcols.append(d)
        betas.append(col[:, GDN_HEADS:2 * GDN_HEADS])
        e_dec.append(jnp.exp(d))
        e_rem.append(jnp.exp(dlast - d))
        e_last.append(jnp.exp(dlast))

    def colv(per_chunk, u):
        j, hd = units[u]
        return per_chunk[j][:, hd:hd + 1]

    q = [qkv_ref[rows(j), _head_cols(0, hd)] for j, hd in units]
    k = [qkv_ref[rows(j), _head_cols(GDN_WIDTH, hd)] for j, hd in units]
    kf = [k[u].astype(F32) for u in n_units]
    kb = [kf[u] * colv(betas, u) for u in n_units]
    gram = [_dot_nt(jnp.concatenate([_bf(kb[u]), q[u]], axis=0), k[u]) for u in n_units]
    lmat = [jnp.exp(jnp.where(causal, colv(dcols, u) - row_ref[hd:hd + 1, rows(j)], -jnp.inf))
            for u, (j, hd) in enumerate(units)]
    a = [jnp.where(strict, gram[u][0:c] * lmat[u], 0.0) for u in n_units]
    qk = [_bf(gram[u][c:2 * c] * lmat[u]) for u in n_units]
    a_b = [_bf(a[u]) for u in n_units]
    tb = [_bf(eye) - jnp.where((ii // 2) == (jj // 2), a_b[u], jnp.zeros_like(a_b[u])) for u in n_units]
    b = 2
    while b < c:
        off = _bf((((ii // (2 * b)) == (jj // (2 * b))) & ((ii // b) != (jj // b))).astype(F32))
        if b % BF16_ROWS == 0:
            lower = [slice(s0 + b, s0 + 2 * b) for s0 in range(0, c, 2 * b)]
            m = [_dot(jnp.concatenate([tb[u][rs] for rs in lower], axis=0), a_b[u] * off) for u in n_units]
            upd = [_bf(_dot(_bf(m[u]), tb[u])) for u in n_units]
            tb = [jnp.concatenate(
                [piece for n, rs in enumerate(lower)
                 for piece in (tb[u][rs.start - b:rs.start], tb[u][rs] - upd[u][n * b:(n + 1) * b])], axis=0)
                 for u in n_units]
        else:
            m = [_dot(tb[u], a_b[u] * off) for u in n_units]
            tb = [tb[u] - _bf(_dot(_bf(m[u]), tb[u])) for u in n_units]
        b *= 2
    r = [jnp.concatenate([qkv_ref[rows(j), _head_cols(2 * GDN_WIDTH, hd)].astype(F32) * colv(betas, u),
                          kb[u] * colv(e_dec, u)], axis=1) for u, (j, hd) in enumerate(units)]
    uw = [_dot(tb[u], _bf(r[u])) for u in n_units]
    w_b = [_bf(uw[u][:, HEAD_DIM:2 * HEAD_DIM]) for u in n_units]
    q_qk = [jnp.concatenate([_bf(q[u].astype(F32) * colv(e_dec, u)), qk[u]], axis=1) for u in n_units]
    kd = [_bf(kf[u] * colv(e_rem, u)) for u in n_units]
    s = [state_ref[hd] for hd in range(GDN_HEADS)]
    for j in range(n_chunks):
        us = [j * GDN_HEADS + hd for hd in range(GDN_HEADS)]
        s_b = [_bf(s[hd]) for hd in range(GDN_HEADS)]
        v_new = [_bf(uw[u][:, 0:HEAD_DIM] - _dot(w_b[u], s_b[hd])) for hd, u in enumerate(us)]
        o = [_dot(q_qk[u], jnp.concatenate([s_b[hd], v_new[hd]], axis=0)) for hd, u in enumerate(us)]
        s = [s[hd] * colv(e_last, u) + _dot_tn(kd[u], v_new[hd]) for hd, u in enumerate(us)]
        for hd in range(GDN_HEADS):
            on = o[hd] * lax.rsqrt(jnp.mean(o[hd] * o[hd], axis=-1, keepdims=True) + EPS) * gnw_ref[...]
            yb_ref[rows(j), _head_cols(0, hd)] = (
                on * zb_ref[rows(j), _head_cols(0, hd)].astype(F32)).astype(yb_ref.dtype)
    for hd in range(GDN_HEADS):
        state_ref[hd] = s[hd]


def _mix_kernel(tiles_per_seq,
                h_ref, nw_ref, w_ref, wab_ref, wabt_ref, lng_ref, lnb_ref, wsp_ref, bspt_ref,
                cw_ref, alog_c_ref, dtb_c_ref, alog_r_ref, dtb_r_ref, gnw_ref,
                ya_ref, yb_ref,
                cs_ref, state_ref, qkv_ref, zb_ref, col_ref, row_ref):
    @pl.when(pl.program_id(0) % tiles_per_seq == 0)
    def _():
        cs_ref[...] = jnp.zeros(cs_ref.shape, F32)
        state_ref[...] = jnp.zeros(state_ref.shape, F32)

    xn = _bf(_rms(h_ref[...], nw_ref[...]))
    _gmlp_branch(xn, w_ref, lng_ref, lnb_ref, wsp_ref, bspt_ref, ya_ref)
    _gdn_prepare(xn, w_ref, wab_ref, wabt_ref, cw_ref, alog_c_ref, dtb_c_ref, alog_r_ref, dtb_r_ref,
                 cs_ref, qkv_ref, zb_ref, col_ref, row_ref)
    for r0 in range(0, h_ref.shape[0], GDN_GROUP_ROWS):
        rs = slice(r0, r0 + GDN_GROUP_ROWS)
        _gdn_chunks(qkv_ref.at[rs], zb_ref.at[rs], col_ref.at[rs], row_ref.at[:, rs], gnw_ref, state_ref,
                    yb_ref.at[rs])


def _layer_spec(layer, shape):
    return pl.BlockSpec((None,) + shape, lambda i: (layer,) + (0,) * len(shape),
                        pipeline_mode=pl.Buffered(1))


def _mix(layer, h, nw, w_main, wab, wabt, lng, lnb, wsp, bspt, cw, alog_c, dtb_c, alog_r, dtb_r, gnw, seq):
    t = h.shape[0]
    tm = TILE_MIX
    spec = functools.partial(_layer_spec, layer)
    return pl.pallas_call(
        functools.partial(_mix_kernel, seq // tm),
        out_shape=(
            jax.ShapeDtypeStruct((t, GMLP_WIDTH), BF16),
            jax.ShapeDtypeStruct((t, GDN_WIDTH), BF16),
        ),
        grid=(t // tm,),
        in_specs=[
            pl.BlockSpec((tm, D_MODEL), lambda i: (i, 0)),
            spec((1, D_MODEL)),
            spec((MAIN_COLS, D_MODEL)),
            spec((D_MODEL, 2 * GDN_HEADS)),
            spec((GDN_HEADS, D_MODEL)),
            spec((1, GMLP_WIDTH)),
            spec((1, GMLP_WIDTH)),
            spec((GMLP_GROUPS, SPATIAL_BLOCK, SPATIAL_BLOCK)),
            spec((SPATIAL_BLOCK, GMLP_GROUPS)),
            spec((CONV_K, 3 * GDN_WIDTH)),
            spec((1, GDN_HEADS)),
            spec((1, GDN_HEADS)),
            spec((GDN_HEADS, 1)),
            spec((GDN_HEADS, 1)),
            spec((1, HEAD_DIM)),
        ],
        out_specs=(
            pl.BlockSpec((tm, GMLP_WIDTH), lambda i: (i, 0)),
            pl.BlockSpec((tm, GDN_WIDTH), lambda i: (i, 0)),
        ),
        scratch_shapes=[
            pltpu.VMEM((CONV_PAD, 3 * GDN_WIDTH), F32),
            pltpu.VMEM((GDN_HEADS, HEAD_DIM, HEAD_DIM), F32),
            pltpu.VMEM((tm, 3 * GDN_WIDTH), BF16),
            pltpu.VMEM((tm, GDN_WIDTH), BF16),
            pltpu.VMEM((tm, 2 * GDN_HEADS), F32),
            pltpu.VMEM((GDN_HEADS, tm), F32),
        ],
        compiler_params=pltpu.CompilerParams(
            dimension_semantics=("arbitrary",), vmem_limit_bytes=VMEM_LIMIT_BYTES),
        name="mix",
    )(h, nw, w_main, wab, wabt, lng, lnb, wsp, bspt, cw, alog_c, dtb_c, alog_r, dtb_r, gnw)


def _out_ple_kernel(apply_final_norm,
                    h_ref, ya_ref, yb_ref, p_ref, wout_ref, wple_ref, pnw_ref, gnw_ref, wg_ref, fnw_ref,
                    o_ref):
    h1 = (h_ref[...] + _dot(ya_ref[...], wout_ref[0:GMLP_WIDTH, :])
          + _dot(yb_ref[...], wout_ref[GMLP_WIDTH:GMLP_WIDTH + GDN_WIDTH, :]))
    e = _rms(_dot(_bf(p_ref[...]), wple_ref[...]), pnw_ref[...])
    gate = jax.nn.sigmoid(_dot(_bf(_rms(h1, gnw_ref[...])), wg_ref[...]))
    h2 = h1 + gate * e
    if apply_final_norm:
        h2 = _rms(h2, fnw_ref[...])
    o_ref[...] = h2


def _out_ple(layer, h, ya, yb, p, wout, wple, pnw, gnw, wg, fnw, apply_final_norm):
    t = h.shape[0]
    tm = TILE_OUT
    spec = functools.partial(_layer_spec, layer)
    tok = lambda width: pl.BlockSpec((tm, width), lambda i: (i, 0))
    return pl.pallas_call(
        functools.partial(_out_ple_kernel, apply_final_norm),
        out_shape=jax.ShapeDtypeStruct((t, D_MODEL), F32),
        grid=(t // tm,),
        in_specs=[
            tok(D_MODEL), tok(GMLP_WIDTH), tok(GDN_WIDTH),
            pl.BlockSpec((None, tm, PLE_DIM), lambda i: (layer, i, 0)),
            spec((GMLP_WIDTH + GDN_WIDTH, D_MODEL)),
            spec((PLE_DIM, D_MODEL)),
            spec((1, D_MODEL)),
            spec((1, D_MODEL)),
            spec((D_MODEL, D_MODEL)),
            pl.BlockSpec((1, D_MODEL), lambda i: (0, 0)),
        ],
        out_specs=tok(D_MODEL),
        compiler_params=pltpu.CompilerParams(
            dimension_semantics=("arbitrary",), vmem_limit_bytes=VMEM_LIMIT_BYTES),
        name="out_ple",
    )(h, ya, yb, p, wout, wple, pnw, gnw, wg, fnw)


def kernel(x, p, norm_w, w_in, ln_v_g, ln_v_b, w_spatial, b_spatial, conv_w, A_log, dt_bias,
           gdn_norm_w, w_out, w_ple, ple_norm_w, ple_gate_norm_w, w_ple_gate, final_norm_w):
    batch, seq, d = x.shape
    depth = w_in.shape[0]
    assert d == D_MODEL and seq % TILE_MIX == 0 and (batch * seq) % TILE_OUT == 0
    assert w_in.shape[2] == MAIN_COLS + 2 * GDN_HEADS
    t = batch * seq
    h = x.reshape(t, d)
    p2 = p.reshape(depth, t, PLE_DIM)
    w_in_t = _bf(jnp.swapaxes(w_in, 1, 2))
    wabt = w_in_t[:, MAIN_COLS:MAIN_COLS + GDN_HEADS, :]
    wab = jnp.swapaxes(w_in_t[:, MAIN_COLS:, :], 1, 2)
    wsp_b, wout_b, wple_b, wg_b = _bf(w_spatial), _bf(w_out), _bf(w_ple), _bf(w_ple_gate)
    bspt = jnp.swapaxes(b_spatial, 1, 2)
    row3 = lambda a: a[:, None, :]
    col3 = lambda a: a[:, :, None]
    for i in range(depth):
        ya, yb = _mix(i, h, row3(norm_w), w_in_t, wab, wabt, row3(ln_v_g), row3(ln_v_b), wsp_b, bspt,
                      conv_w, row3(A_log), row3(dt_bias), col3(A_log), col3(dt_bias), row3(gdn_norm_w), seq)
        h = _out_ple(i, h, ya, yb, p2, wout_b, wple_b, row3(ple_norm_w), row3(ple_gate_norm_w), wg_b,
                     final_norm_w[None], apply_final_norm=(i == depth - 1))
    return h.reshape(batch, seq, d)
```

```python
import functools

import jax
import jax.numpy as jnp
from jax import lax
from jax.experimental import pallas as pl
from jax.experimental.pallas import tpu as pltpu

D_MODEL = 1024
GMLP_WIDTH = 1024
GMLP_GROUPS = 8
GROUP_DIM = 128
SPATIAL_BLOCK = 128
SPATIAL_CHUNK = 64
GDN_WIDTH = 1024
GDN_HEADS = 8
HEAD_DIM = 128
CONV_K = 4
PLE_DIM = 256
EPS = 1e-6
MAIN_COLS = 3 * GMLP_WIDTH + 4 * GDN_WIDTH

GDN_CHUNK = 128
TILE_MIX = 512
TILE_OUT = 512
GDN_GROUP_ROWS = 256
STRIP_COLS = 256
BF16_ROWS = 16
CONV_PAD = 8
VMEM_LIMIT_BYTES = 56 * 1024 * 1024

F32 = jnp.float32
BF16 = jnp.bfloat16
LOG2E = 1.4426950408889634
GELU_K1 = -2.0 * 0.7978845608028654 * LOG2E
GELU_K3 = GELU_K1 * 0.044715


def _bf(x):
    return x.astype(BF16)


def _dot(a, b):
    return jnp.dot(a, b, preferred_element_type=F32)


def _dot_nt(a, b):
    return lax.dot_general(a, b, (((1,), (1,)), ((), ())), preferred_element_type=F32)


def _dot_tn(a, b):
    return lax.dot_general(a, b, (((0,), (0,)), ((), ())), preferred_element_type=F32)


def _rms(x, w):
    return x * lax.rsqrt(jnp.mean(x * x, axis=-1, keepdims=True) + EPS) * w


def _silu(x):
    return x / (1.0 + jnp.exp2(x * -LOG2E))


def _gelu_tanh(x):
    return x / (1.0 + jnp.exp2(x * (GELU_K1 + GELU_K3 * (x * x))))


def _softplus(x):
    return jnp.maximum(x, 0.0) + jnp.log1p(jnp.exp(-jnp.abs(x)))


def _head_cols(base, hd):
    return slice(base + hd * HEAD_DIM, base + (hd + 1) * HEAD_DIM)


def _gmlp_branch(xn, w_ref, lng_ref, lnb_ref, wsp_ref, bspt_ref, ya_ref):
    tm = xn.shape[0]
    strips = range(0, GMLP_WIDTH, STRIP_COLS)

    def proj(base, c0):
        return _dot_nt(xn, w_ref[base + c0:base + c0 + STRIP_COLS, :])

    v = jnp.concatenate([_gelu_tanh(proj(GMLP_WIDTH, c0)) for c0 in strips], axis=1)
    mu = jnp.mean(v, axis=-1, keepdims=True)
    vc = v - mu
    var = jnp.mean(vc * vc, axis=-1, keepdims=True)
    vln = _bf(vc * lax.rsqrt(var + EPS) * lng_ref[...] + lnb_ref[...])
    uz = jnp.concatenate([_gelu_tanh(proj(0, c0)) * _silu(proj(2 * GMLP_WIDTH, c0)) for c0 in strips], axis=1)
    ri = lax.broadcasted_iota(jnp.int32, (SPATIAL_BLOCK, SPATIAL_BLOCK), 0) // SPATIAL_CHUNK
    ci = lax.broadcasted_iota(jnp.int32, (SPATIAL_BLOCK, SPATIAL_BLOCK), 1) // SPATIAL_CHUNK
    chunk_causal = ci <= ri
    for g in range(GMLP_GROUPS):
        gs = slice(g * GROUP_DIM, (g + 1) * GROUP_DIM)
        wm = jnp.where(chunk_causal, wsp_ref[g], jnp.zeros_like(wsp_ref[g]))
        bias = bspt_ref[:, g:g + 1]
        for r in range(tm // SPATIAL_BLOCK):
            rs = slice(r * SPATIAL_BLOCK, (r + 1) * SPATIAL_BLOCK)
            mixed = _dot(wm, vln[rs, gs]) + bias
            ya_ref[rs, gs] = (uz[rs, gs] * mixed).astype(ya_ref.dtype)


def _gdn_prepare(xn, w_ref, wab_ref, wabt_ref, wbt_ref, cw_ref, alog_c_ref, dtb_c_ref, alog_r_ref, dtb_r_ref,
                 cs_ref, qkv_ref, zb_ref, col_ref, row_ref, brow_ref):
    tm = xn.shape[0]
    sub = lax.broadcasted_iota(jnp.int32, (CONV_PAD, STRIP_COLS), 0)
    for c0 in range(0, 3 * GDN_WIDTH, STRIP_COLS):
        cols = slice(c0, c0 + STRIP_COLS)
        x = _dot_nt(xn, w_ref[3 * GMLP_WIDTH + c0:3 * GMLP_WIDTH + c0 + STRIP_COLS, :])
        prev = cs_ref[:, cols]
        acc = x * cw_ref[CONV_K - 1:CONV_K, cols]
        for shift in range(1, CONV_K):
            xs = pltpu.roll(x, shift, axis=0)
            head = jnp.where(sub < shift, pltpu.roll(prev, shift, axis=0), xs[0:CONV_PAD])
            shifted = jnp.concatenate([head, xs[CONV_PAD:]], axis=0)
            acc = acc + shifted * cw_ref[CONV_K - 1 - shift:CONV_K - shift, cols]
        cs_ref[:, cols] = x[tm - CONV_PAD:tm]
        act = _silu(acc)
        if c0 >= 2 * GDN_WIDTH:
            qkv_ref[:, cols] = _bf(act)
        else:
            scale = HEAD_DIM ** -0.5 if c0 < GDN_WIDTH else 1.0
            for hd in range(STRIP_COLS // HEAD_DIM):
                t = act[:, _head_cols(0, hd)]
                t = t * (lax.rsqrt(jnp.sum(t * t, axis=-1, keepdims=True) + EPS) * scale)
                qkv_ref[:, _head_cols(c0, hd)] = _bf(t)

    for c0 in range(0, GDN_WIDTH, STRIP_COLS):
        zb_ref[:, c0:c0 + STRIP_COLS] = _bf(_silu(_dot_nt(
            xn, w_ref[3 * GMLP_WIDTH + 3 * GDN_WIDTH + c0:3 * GMLP_WIDTH + 3 * GDN_WIDTH + c0 + STRIP_COLS, :])))

    ab = _dot(xn, wab_ref[...])
    g_col = -jnp.exp(alog_c_ref[...]) * _softplus(ab[:, 0:GDN_HEADS] + dtb_c_ref[...])
    beta = jax.nn.sigmoid(ab[:, GDN_HEADS:2 * GDN_HEADS])
    a_row = _dot_nt(wabt_ref[...], xn)
    brow_ref[...] = jax.nn.sigmoid(_dot_nt(wbt_ref[...], xn))
    g_row = -jnp.exp(alog_r_ref[...]) * _softplus(a_row + dtb_r_ref[...])
    ii = lax.broadcasted_iota(jnp.int32, (GDN_CHUNK, GDN_CHUNK), 0)
    jj = lax.broadcasted_iota(jnp.int32, (GDN_CHUNK, GDN_CHUNK), 1)
    tri_lower = (jj <= ii).astype(F32)
    tri_upper = (ii <= jj).astype(F32)
    for r in range(tm // GDN_CHUNK):
        rs = slice(r * GDN_CHUNK, (r + 1) * GDN_CHUNK)
        col_ref[rs, 0:GDN_HEADS] = jnp.dot(tri_lower, g_col[rs, :], precision=lax.Precision.HIGHEST,
                                           preferred_element_type=F32)
        col_ref[rs, GDN_HEADS:2 * GDN_HEADS] = beta[rs, :]
        row_ref[:, rs] = jnp.dot(g_row[:, rs], tri_upper, precision=lax.Precision.HIGHEST,
                                 preferred_element_type=F32)


def _gdn_chunks(qkv_ref, zb_ref, col_ref, row_ref, brow_ref, gnw_ref, state_ref, yb_ref):
    c = GDN_CHUNK
    n_chunks = qkv_ref.shape[0] // c
    ii = lax.broadcasted_iota(jnp.int32, (c, c), 0)
    jj = lax.broadcasted_iota(jnp.int32, (c, c), 1)
    causal = jj <= ii
    strict = jj < ii
    eye = (ii == jj).astype(F32)
    units = [(j, hd) for j in range(n_chunks) for hd in range(GDN_HEADS)]
    n_units = range(len(units))

    def rows(j):
        return slice(j * c, (j + 1) * c)

    dcols, betas, e_dec, e_rem, e_last = [], [], [], [], []
    for j in range(n_chunks):
        col = col_ref[rows(j), :]
        d = col[:, 0:GDN_HEADS]
        dlast = d[c - 1:c, :]
        dcols.append(d)
        betas.append(col[:, GDN_HEADS:2 * GDN_HEADS])
        e_dec.append(jnp.exp(d))
        e_rem.append(jnp.exp(dlast - d))
        e_last.append(jnp.exp(dlast))

    def colv(per_chunk, u):
        j, hd = units[u]
        return per_chunk[j][:, hd:hd + 1]

    q = [qkv_ref[rows(j), _head_cols(0, hd)] for j, hd in units]
    k = [qkv_ref[rows(j), _head_cols(GDN_WIDTH, hd)] for j, hd in units]
    kf = [k[u].astype(F32) for u in n_units]
    kb = [kf[u] * colv(betas, u) for u in n_units]
    gram = [_dot_nt(jnp.concatenate([_bf(kb[u]), q[u]], axis=0), k[u]) for u in n_units]
    lmat = [jnp.exp(jnp.where(causal, colv(dcols, u) - row_ref[hd:hd + 1, rows(j)], -jnp.inf))
            for u, (j, hd) in enumerate(units)]
    a = [jnp.where(strict, gram[u][0:c] * lmat[u], 0.0) for u in n_units]
    qk = [_bf(gram[u][c:2 * c] * lmat[u]) for u in n_units]
    a_b = [_bf(a[u]) for u in n_units]
    tb = [_bf(eye) - jnp.where((ii // 2) == (jj // 2), a_b[u], jnp.zeros_like(a_b[u])) for u in n_units]
    b = 2
    while b < c:
        off = _bf((((ii // (2 * b)) == (jj // (2 * b))) & ((ii // b) != (jj // b))).astype(F32))
        if b % BF16_ROWS == 0:
            lower = [slice(s0 + b, s0 + 2 * b) for s0 in range(0, c, 2 * b)]
            m = [_dot(jnp.concatenate([tb[u][rs] for rs in lower], axis=0), a_b[u] * off) for u in n_units]
            upd = [_bf(_dot(_bf(m[u]), tb[u])) for u in n_units]
            tb = [jnp.concatenate(
                [piece for n, rs in enumerate(lower)
                 for piece in (tb[u][rs.start - b:rs.start], tb[u][rs] - upd[u][n * b:(n + 1) * b])], axis=0)
                 for u in n_units]
        else:
            m = [_dot(tb[u], a_b[u] * off) for u in n_units]
            tb = [tb[u] - _bf(_dot(_bf(m[u]), tb[u])) for u in n_units]
        b *= 2
    beta_r = [brow_ref[hd:hd + 1, rows(j)] for j, hd in units]
    bdec_r = [beta_r[u] * jnp.exp(row_ref[hd:hd + 1, rows(j)]) for u, (j, hd) in enumerate(units)]
    u_val = [_dot(tb[u] * _bf(beta_r[u]), qkv_ref[rows(j), _head_cols(2 * GDN_WIDTH, hd)])
             for u, (j, hd) in enumerate(units)]
    w_b = [_bf(_dot(tb[u] * _bf(bdec_r[u]), k[u])) for u in n_units]
    q_qk = [jnp.concatenate([_bf(q[u].astype(F32) * colv(e_dec, u)), qk[u]], axis=1) for u in n_units]
    kd = [_bf(kf[u] * colv(e_rem, u)) for u in n_units]
    s = [state_ref[hd] for hd in range(GDN_HEADS)]
    for j in range(n_chunks):
        us = [j * GDN_HEADS + hd for hd in range(GDN_HEADS)]
        s_b = [_bf(s[hd]) for hd in range(GDN_HEADS)]
        v_new = [_bf(u_val[u] - _dot(w_b[u], s_b[hd])) for hd, u in enumerate(us)]
        o = [_dot(q_qk[u], jnp.concatenate([s_b[hd], v_new[hd]], axis=0)) for hd, u in enumerate(us)]
        s = [s[hd] * colv(e_last, u) + _dot_tn(kd[u], v_new[hd]) for hd, u in enumerate(us)]
        for hd in range(GDN_HEADS):
            on = o[hd] * lax.rsqrt(jnp.mean(o[hd] * o[hd], axis=-1, keepdims=True) + EPS) * gnw_ref[...]
            yb_ref[rows(j), _head_cols(0, hd)] = (
                on * zb_ref[rows(j), _head_cols(0, hd)].astype(F32)).astype(yb_ref.dtype)
    for hd in range(GDN_HEADS):
        state_ref[hd] = s[hd]


def _mix_kernel(tiles_per_seq,
                h_ref, nw_ref, w_ref, wab_ref, wabt_ref, wbt_ref, lng_ref, lnb_ref, wsp_ref, bspt_ref,
                cw_ref, alog_c_ref, dtb_c_ref, alog_r_ref, dtb_r_ref, gnw_ref,
                ya_ref, yb_ref,
                cs_ref, state_ref, qkv_ref, zb_ref, col_ref, row_ref, brow_ref):
    @pl.when(pl.program_id(0) % tiles_per_seq == 0)
    def _():
        cs_ref[...] = jnp.zeros(cs_ref.shape, F32)
        state_ref[...] = jnp.zeros(state_ref.shape, F32)

    xn = _bf(_rms(h_ref[...], nw_ref[...]))
    _gmlp_branch(xn, w_ref, lng_ref, lnb_ref, wsp_ref, bspt_ref, ya_ref)
    _gdn_prepare(xn, w_ref, wab_ref, wabt_ref, wbt_ref, cw_ref, alog_c_ref, dtb_c_ref, alog_r_ref, dtb_r_ref,
                 cs_ref, qkv_ref, zb_ref, col_ref, row_ref, brow_ref)
    for r0 in range(0, h_ref.shape[0], GDN_GROUP_ROWS):
        rs = slice(r0, r0 + GDN_GROUP_ROWS)
        _gdn_chunks(qkv_ref.at[rs], zb_ref.at[rs], col_ref.at[rs], row_ref.at[:, rs], brow_ref.at[:, rs], gnw_ref, state_ref,
                    yb_ref.at[rs])


def _layer_spec(layer, shape):
    return pl.BlockSpec((None,) + shape, lambda i: (layer,) + (0,) * len(shape),
                        pipeline_mode=pl.Buffered(1))


def _mix(layer, h, nw, w_main, wab, wabt, wbt, lng, lnb, wsp, bspt, cw, alog_c, dtb_c, alog_r, dtb_r, gnw, seq):
    t = h.shape[0]
    tm = TILE_MIX
    spec = functools.partial(_layer_spec, layer)
    return pl.pallas_call(
        functools.partial(_mix_kernel, seq // tm),
        out_shape=(
            jax.ShapeDtypeStruct((t, GMLP_WIDTH), BF16),
            jax.ShapeDtypeStruct((t, GDN_WIDTH), BF16),
        ),
        grid=(t // tm,),
        in_specs=[
            pl.BlockSpec((tm, D_MODEL), lambda i: (i, 0)),
            spec((1, D_MODEL)),
            spec((MAIN_COLS, D_MODEL)),
            spec((D_MODEL, 2 * GDN_HEADS)),
            spec((GDN_HEADS, D_MODEL)),
            spec((GDN_HEADS, D_MODEL)),
            spec((1, GMLP_WIDTH)),
            spec((1, GMLP_WIDTH)),
            spec((GMLP_GROUPS, SPATIAL_BLOCK, SPATIAL_BLOCK)),
            spec((SPATIAL_BLOCK, GMLP_GROUPS)),
            spec((CONV_K, 3 * GDN_WIDTH)),
            spec((1, GDN_HEADS)),
            spec((1, GDN_HEADS)),
            spec((GDN_HEADS, 1)),
            spec((GDN_HEADS, 1)),
            spec((1, HEAD_DIM)),
        ],
        out_specs=(
            pl.BlockSpec((tm, GMLP_WIDTH), lambda i: (i, 0)),
            pl.BlockSpec((tm, GDN_WIDTH), lambda i: (i, 0)),
        ),
        scratch_shapes=[
            pltpu.VMEM((CONV_PAD, 3 * GDN_WIDTH), F32),
            pltpu.VMEM((GDN_HEADS, HEAD_DIM, HEAD_DIM), F32),
            pltpu.VMEM((tm, 3 * GDN_WIDTH), BF16),
            pltpu.VMEM((tm, GDN_WIDTH), BF16),
            pltpu.VMEM((tm, 2 * GDN_HEADS), F32),
            pltpu.VMEM((GDN_HEADS, tm), F32),
            pltpu.VMEM((GDN_HEADS, tm), F32),
        ],
        compiler_params=pltpu.CompilerParams(
            dimension_semantics=("arbitrary",), vmem_limit_bytes=VMEM_LIMIT_BYTES),
        name="mix",
    )(h, nw, w_main, wab, wabt, wbt, lng, lnb, wsp, bspt, cw, alog_c, dtb_c, alog_r, dtb_r, gnw)


def _out_ple_kernel(apply_final_norm,
                    h_ref, ya_ref, yb_ref, p_ref, wout_ref, wple_ref, pnw_ref, gnw_ref, wg_ref, fnw_ref,
                    o_ref):
    h1 = (h_ref[...] + _dot(ya_ref[...], wout_ref[0:GMLP_WIDTH, :])
          + _dot(yb_ref[...], wout_ref[GMLP_WIDTH:GMLP_WIDTH + GDN_WIDTH, :]))
    e = _rms(_dot(_bf(p_ref[...]), wple_ref[...]), pnw_ref[...])
    gate = jax.nn.sigmoid(_dot(_bf(_rms(h1, gnw_ref[...])), wg_ref[...]))
    h2 = h1 + gate * e
    if apply_final_norm:
        h2 = _rms(h2, fnw_ref[...])
    o_ref[...] = h2


def _out_ple(layer, h, ya, yb, p, wout, wple, pnw, gnw, wg, fnw, apply_final_norm):
    t = h.shape[0]
    tm = TILE_OUT
    spec = functools.partial(_layer_spec, layer)
    tok = lambda width: pl.BlockSpec((tm, width), lambda i: (i, 0))
    return pl.pallas_call(
        functools.partial(_out_ple_kernel, apply_final_norm),
        out_shape=jax.ShapeDtypeStruct((t, D_MODEL), F32),
        grid=(t // tm,),
        in_specs=[
            tok(D_MODEL), tok(GMLP_WIDTH), tok(GDN_WIDTH),
            pl.BlockSpec((None, tm, PLE_DIM), lambda i: (layer, i, 0)),
            spec((GMLP_WIDTH + GDN_WIDTH, D_MODEL)),
            spec((PLE_DIM, D_MODEL)),
            spec((1, D_MODEL)),
            spec((1, D_MODEL)),
            spec((D_MODEL, D_MODEL)),
            pl.BlockSpec((1, D_MODEL), lambda i: (0, 0)),
        ],
        out_specs=tok(D_MODEL),
        compiler_params=pltpu.CompilerParams(
            dimension_semantics=("arbitrary",), vmem_limit_bytes=VMEM_LIMIT_BYTES),
        name="out_ple",
    )(h, ya, yb, p, wout, wple, pnw, gnw, wg, fnw)


def kernel(x, p, norm_w, w_in, ln_v_g, ln_v_b, w_spatial, b_spatial, conv_w, A_log, dt_bias,
           gdn_norm_w, w_out, w_ple, ple_norm_w, ple_gate_norm_w, w_ple_gate, final_norm_w):
    batch, seq, d = x.shape
    depth = w_in.shape[0]
    assert d == D_MODEL and seq % TILE_MIX == 0 and (batch * seq) % TILE_OUT == 0
    assert w_in.shape[2] == MAIN_COLS + 2 * GDN_HEADS
    t = batch * seq
    h = x.reshape(t, d)
    p2 = p.reshape(depth, t, PLE_DIM)
    w_in_t = _bf(jnp.swapaxes(w_in, 1, 2))
    wabt = w_in_t[:, MAIN_COLS:MAIN_COLS + GDN_HEADS, :]
    wbt = w_in_t[:, MAIN_COLS + GDN_HEADS:, :]
    wab = jnp.swapaxes(w_in_t[:, MAIN_COLS:, :], 1, 2)
    wsp_b, wout_b, wple_b, wg_b = _bf(w_spatial), _bf(w_out), _bf(w_ple), _bf(w_ple_gate)
    bspt = jnp.swapaxes(b_spatial, 1, 2)
    row3 = lambda a: a[:, None, :]
    col3 = lambda a: a[:, :, None]
    for i in range(depth):
        ya, yb = _mix(i, h, row3(norm_w), w_in_t, wab, wabt, wbt, row3(ln_v_g), row3(ln_v_b), wsp_b, bspt,
                      conv_w, row3(A_log), row3(dt_bias), col3(A_log), col3(dt_bias), row3(gdn_norm_w), seq)
        h = _out_ple(i, h, ya, yb, p2, wout_b, wple_b, row3(ple_norm_w), row3(ple_gate_norm_w), wg_b,
                     final_norm_w[None], apply_final_norm=(i == depth - 1))
    return h.reshape(batch, seq, d)
```

```python
import functools

import jax
import jax.numpy as jnp
from jax import lax
from jax.experimental import pallas as pl
from jax.experimental.pallas import tpu as pltpu

D_MODEL = 1024
GMLP_WIDTH = 1024
GMLP_GROUPS = 8
GROUP_DIM = 128
SPATIAL_BLOCK = 128
SPATIAL_CHUNK = 64
GDN_WIDTH = 1024
GDN_HEADS = 8
HEAD_DIM = 128
CONV_K = 4
PLE_DIM = 256
EPS = 1e-6
MAIN_COLS = 3 * GMLP_WIDTH + 4 * GDN_WIDTH

GDN_CHUNK = 128
TILE_MIX = 512
TILE_OUT = 512
GDN_GROUP_ROWS = 256
STRIP_COLS = 512
BF16_ROWS = 16
CONV_PAD = 8
VMEM_LIMIT_BYTES = 56 * 1024 * 1024

F32 = jnp.float32
BF16 = jnp.bfloat16
LOG2E = 1.4426950408889634
GELU_K1 = -2.0 * 0.7978845608028654 * LOG2E
GELU_K3 = GELU_K1 * 0.044715


def _bf(x):
    return x.astype(BF16)


def _dot(a, b):
    return jnp.dot(a, b, preferred_element_type=F32)


def _dot_nt(a, b):
    return lax.dot_general(a, b, (((1,), (1,)), ((), ())), preferred_element_type=F32)


def _dot_tn(a, b):
    return lax.dot_general(a, b, (((0,), (0,)), ((), ())), preferred_element_type=F32)


def _rms(x, w):
    return x * lax.rsqrt(jnp.mean(x * x, axis=-1, keepdims=True) + EPS) * w


def _silu(x):
    return x / (1.0 + jnp.exp2(x * -LOG2E))


def _gelu_tanh(x):
    return x / (1.0 + jnp.exp2(x * (GELU_K1 + GELU_K3 * (x * x))))


def _softplus(x):
    return jnp.maximum(x, 0.0) + jnp.log1p(jnp.exp(-jnp.abs(x)))


def _head_cols(base, hd):
    return slice(base + hd * HEAD_DIM, base + (hd + 1) * HEAD_DIM)


def _gmlp_branch(xn, w_ref, lng_ref, lnb_ref, wsp_ref, bspt_ref, ya_ref):
    tm = xn.shape[0]
    strips = range(0, GMLP_WIDTH, STRIP_COLS)

    def proj(base, c0):
        return _dot_nt(xn, w_ref[base + c0:base + c0 + STRIP_COLS, :])

    v = jnp.concatenate([_gelu_tanh(proj(GMLP_WIDTH, c0)) for c0 in strips], axis=1)
    mu = jnp.mean(v, axis=-1, keepdims=True)
    vc = v - mu
    var = jnp.mean(vc * vc, axis=-1, keepdims=True)
    vln = _bf(vc * lax.rsqrt(var + EPS) * lng_ref[...] + lnb_ref[...])
    uz = jnp.concatenate([_gelu_tanh(proj(0, c0)) * _silu(proj(2 * GMLP_WIDTH, c0)) for c0 in strips], axis=1)
    ri = lax.broadcasted_iota(jnp.int32, (SPATIAL_BLOCK, SPATIAL_BLOCK), 0) // SPATIAL_CHUNK
    ci = lax.broadcasted_iota(jnp.int32, (SPATIAL_BLOCK, SPATIAL_BLOCK), 1) // SPATIAL_CHUNK
    chunk_causal = ci <= ri
    for g in range(GMLP_GROUPS):
        gs = slice(g * GROUP_DIM, (g + 1) * GROUP_DIM)
        wm = jnp.where(chunk_causal, wsp_ref[g], jnp.zeros_like(wsp_ref[g]))
        bias = bspt_ref[:, g:g + 1]
        for r in range(tm // SPATIAL_BLOCK):
            rs = slice(r * SPATIAL_BLOCK, (r + 1) * SPATIAL_BLOCK)
            mixed = _dot(wm, vln[rs, gs]) + bias
            ya_ref[rs, gs] = (uz[rs, gs] * mixed).astype(ya_ref.dtype)


def _gdn_prepare(xn, w_ref, wab_ref, wabt_ref, cw_ref, alog_c_ref, dtb_c_ref, alog_r_ref, dtb_r_ref,
                 cs_ref, qkv_ref, zb_ref, col_ref, row_ref):
    tm = xn.shape[0]
    sub = lax.broadcasted_iota(jnp.int32, (CONV_PAD, STRIP_COLS), 0)
    for c0 in range(0, 3 * GDN_WIDTH, STRIP_COLS):
        cols = slice(c0, c0 + STRIP_COLS)
        x = _dot_nt(xn, w_ref[3 * GMLP_WIDTH + c0:3 * GMLP_WIDTH + c0 + STRIP_COLS, :])
        prev = cs_ref[:, cols]
        acc = x * cw_ref[CONV_K - 1:CONV_K, cols]
        for shift in range(1, CONV_K):
            xs = pltpu.roll(x, shift, axis=0)
            head = jnp.where(sub < shift, pltpu.roll(prev, shift, axis=0), xs[0:CONV_PAD])
            shifted = jnp.concatenate([head, xs[CONV_PAD:]], axis=0)
            acc = acc + shifted * cw_ref[CONV_K - 1 - shift:CONV_K - shift, cols]
        cs_ref[:, cols] = x[tm - CONV_PAD:tm]
        act = _silu(acc)
        if c0 >= 2 * GDN_WIDTH:
            qkv_ref[:, cols] = _bf(act)
        else:
            scale = HEAD_DIM ** -0.5 if c0 < GDN_WIDTH else 1.0
            for hd in range(STRIP_COLS // HEAD_DIM):
                t = act[:, _head_cols(0, hd)]
                t = t * (lax.rsqrt(jnp.sum(t * t, axis=-1, keepdims=True) + EPS) * scale)
                qkv_ref[:, _head_cols(c0, hd)] = _bf(t)

    for c0 in range(0, GDN_WIDTH, STRIP_COLS):
        zb_ref[:, c0:c0 + STRIP_COLS] = _bf(_silu(_dot_nt(
            xn, w_ref[3 * GMLP_WIDTH + 3 * GDN_WIDTH + c0:3 * GMLP_WIDTH + 3 * GDN_WIDTH + c0 + STRIP_COLS, :])))

    ab = _dot(xn, wab_ref[...])
    g_col = -jnp.exp(alog_c_ref[...]) * _softplus(ab[:, 0:GDN_HEADS] + dtb_c_ref[...])
    beta = jax.nn.sigmoid(ab[:, GDN_HEADS:2 * GDN_HEADS])
    a_row = _dot_nt(wabt_ref[...], xn)
    g_row = -jnp.exp(alog_r_ref[...]) * _softplus(a_row + dtb_r_ref[...])
    ii = lax.broadcasted_iota(jnp.int32, (GDN_CHUNK, GDN_CHUNK), 0)
    jj = lax.broadcasted_iota(jnp.int32, (GDN_CHUNK, GDN_CHUNK), 1)
    tri_lower = (jj <= ii).astype(F32)
    tri_upper = (ii <= jj).astype(F32)
    for r in range(tm // GDN_CHUNK):
        rs = slice(r * GDN_CHUNK, (r + 1) * GDN_CHUNK)
        col_ref[rs, 0:GDN_HEADS] = jnp.dot(tri_lower, g_col[rs, :], precision=lax.Precision.HIGHEST,
                                           preferred_element_type=F32)
        col_ref[rs, GDN_HEADS:2 * GDN_HEADS] = beta[rs, :]
        row_ref[:, rs] = jnp.dot(g_row[:, rs], tri_upper, precision=lax.Precision.HIGHEST,
                                 preferred_element_type=F32)


def _gdn_chunks(qkv_ref, zb_ref, col_ref, row_ref, gnw_ref, state_ref, yb_ref):
    c = GDN_CHUNK
    n_chunks = qkv_ref.shape[0] // c
    ii = lax.broadcasted_iota(jnp.int32, (c, c), 0)
    jj = lax.broadcasted_iota(jnp.int32, (c, c), 1)
    causal = jj <= ii
    strict = jj < ii
    eye = (ii == jj).astype(F32)
    units = [(j, hd) for j in range(n_chunks) for hd in range(GDN_HEADS)]
    n_units = range(len(units))

    def rows(j):
        return slice(j * c, (j + 1) * c)

    dcols, betas, e_dec, e_rem, e_last = [], [], [], [], []
    for j in range(n_chunks):
        col = col_ref[rows(j), :]
        d = col[:, 0:GDN_HEADS]
        dlast = d[c - 1:c, :]
        dcols.append(d)
        betas.append(col[:, GDN_HEADS:2 * GDN_HEADS])
        e_dec.append(jnp.exp(d))
        e_rem.append(jnp.exp(dlast - d))
        e_last.append(jnp.exp(dlast))

    def colv(per_chunk, u):
        j, hd = units[u]
        return per_chunk[j][:, hd:hd + 1]

    q = [qkv_ref[rows(j), _head_cols(0, hd)] for j, hd in units]
    k = [qkv_ref[rows(j), _head_cols(GDN_WIDTH, hd)] for j, hd in units]
    kf = [k[u].astype(F32) for u in n_units]
    kb = [kf[u] * colv(betas, u) for u in n_units]
    gram = [_dot_nt(jnp.concatenate([_bf(kb[u]), q[u]], axis=0), k[u]) for u in n_units]
    lmat = [jnp.exp(jnp.where(causal, colv(dcols, u) - row_ref[hd:hd + 1, rows(j)], -jnp.inf))
            for u, (j, hd) in enumerate(units)]
    a = [jnp.where(strict, gram[u][0:c] * lmat[u], 0.0) for u in n_units]
    qk = [_bf(gram[u][c:2 * c] * lmat[u]) for u in n_units]
    a_b = [_bf(a[u]) for u in n_units]
    tb = [_bf(eye) - jnp.where((ii // 2) == (jj // 2), a_b[u], jnp.zeros_like(a_b[u])) for u in n_units]
    b = 2
    while b < c:
        off = _bf((((ii // (2 * b)) == (jj // (2 * b))) & ((ii // b) != (jj // b))).astype(F32))
        if b % BF16_ROWS == 0:
            lower = [slice(s0 + b, s0 + 2 * b) for s0 in range(0, c, 2 * b)]
            m = [_dot(jnp.concatenate([tb[u][rs] for rs in lower], axis=0), a_b[u] * off) for u in n_units]
            upd = [_bf(_dot(_bf(m[u]), tb[u])) for u in n_units]
            tb = [jnp.concatenate(
                [piece for n, rs in enumerate(lower)
                 for piece in (tb[u][rs.start - b:rs.start], tb[u][rs] - upd[u][n * b:(n + 1) * b])], axis=0)
                 for u in n_units]
        else:
            m = [_dot(tb[u], a_b[u] * off) for u in n_units]
            tb = [tb[u] - _bf(_dot(_bf(m[u]), tb[u])) for u in n_units]
        b *= 2
    r = [jnp.concatenate([qkv_ref[rows(j), _head_cols(2 * GDN_WIDTH, hd)].astype(F32) * colv(betas, u),
                          kb[u] * colv(e_dec, u)], axis=1) for u, (j, hd) in enumerate(units)]
    uw = [_dot(tb[u], _bf(r[u])) for u in n_units]
    w_b = [_bf(uw[u][:, HEAD_DIM:2 * HEAD_DIM]) for u in n_units]
    q_qk = [jnp.concatenate([_bf(q[u].astype(F32) * colv(e_dec, u)), qk[u]], axis=1) for u in n_units]
    kd = [_bf(kf[u] * colv(e_rem, u)) for u in n_units]
    s = [state_ref[hd] for hd in range(GDN_HEADS)]
    for j in range(n_chunks):
        us = [j * GDN_HEADS + hd for hd in range(GDN_HEADS)]
        s_b = [_bf(s[hd]) for hd in range(GDN_HEADS)]
        v_new = [_bf(uw[u][:, 0:HEAD_DIM] - _dot(w_b[u], s_b[hd])) for hd, u in enumerate(us)]
        o = [_dot(q_qk[u], jnp.concatenate([s_b[hd], v_new[hd]], axis=0)) for hd, u in enumerate(us)]
        s = [s[hd] * colv(e_last, u) + _dot_tn(kd[u], v_new[hd]) for hd, u in enumerate(us)]
        for hd in range(GDN_HEADS):
            on = o[hd] * lax.rsqrt(jnp.mean(o[hd] * o[hd], axis=-1, keepdims=True) + EPS) * gnw_ref[...]
            yb_ref[rows(j), _head_cols(0, hd)] = (
                on * zb_ref[rows(j), _head_cols(0, hd)].astype(F32)).astype(yb_ref.dtype)
    for hd in range(GDN_HEADS):
        state_ref[hd] = s[hd]


def _mix_kernel(tiles_per_seq,
                h_ref, nw_ref, w_ref, wab_ref, wabt_ref, lng_ref, lnb_ref, wsp_ref, bspt_ref,
                cw_ref, alog_c_ref, dtb_c_ref, alog_r_ref, dtb_r_ref, gnw_ref,
                ya_ref, yb_ref,
                cs_ref, state_ref, qkv_ref, zb_ref, col_ref, row_ref):
    @pl.when(pl.program_id(0) % tiles_per_seq == 0)
    def _():
        cs_ref[...] = jnp.zeros(cs_ref.shape, F32)
        state_ref[...] = jnp.zeros(state_ref.shape, F32)

    xn = _bf(_rms(h_ref[...], nw_ref[...]))
    _gmlp_branch(xn, w_ref, lng_ref, lnb_ref, wsp_ref, bspt_ref, ya_ref)
    _gdn_prepare(xn, w_ref, wab_ref, wabt_ref, cw_ref, alog_c_ref, dtb_c_ref, alog_r_ref, dtb_r_ref,
                 cs_ref, qkv_ref, zb_ref, col_ref, row_ref)
    for r0 in range(0, h_ref.shape[0], GDN_GROUP_ROWS):
        rs = slice(r0, r0 + GDN_GROUP_ROWS)
        _gdn_chunks(qkv_ref.at[rs], zb_ref.at[rs], col_ref.at[rs], row_ref.at[:, rs], gnw_ref, state_ref,
                    yb_ref.at[rs])


def _layer_spec(layer, shape):
    return pl.BlockSpec((None,) + shape, lambda i: (layer,) + (0,) * len(shape),
                        pipeline_mode=pl.Buffered(1))


def _mix(layer, h, nw, w_main, wab, wabt, lng, lnb, wsp, bspt, cw, alog_c, dtb_c, alog_r, dtb_r, gnw, seq):
    t = h.shape[0]
    tm = TILE_MIX
    spec = functools.partial(_layer_spec, layer)
    return pl.pallas_call(
        functools.partial(_mix_kernel, seq // tm),
        out_shape=(
            jax.ShapeDtypeStruct((t, GMLP_WIDTH), BF16),
            jax.ShapeDtypeStruct((t, GDN_WIDTH), BF16),
        ),
        grid=(t // tm,),
        in_specs=[
            pl.BlockSpec((tm, D_MODEL), lambda i: (i, 0)),
            spec((1, D_MODEL)),
            spec((MAIN_COLS, D_MODEL)),
            spec((D_MODEL, 2 * GDN_HEADS)),
            spec((GDN_HEADS, D_MODEL)),
            spec((1, GMLP_WIDTH)),
            spec((1, GMLP_WIDTH)),
            spec((GMLP_GROUPS, SPATIAL_BLOCK, SPATIAL_BLOCK)),
            spec((SPATIAL_BLOCK, GMLP_GROUPS)),
            spec((CONV_K, 3 * GDN_WIDTH)),
            spec((1, GDN_HEADS)),
            spec((1, GDN_HEADS)),
            spec((GDN_HEADS, 1)),
            spec((GDN_HEADS, 1)),
            spec((1, HEAD_DIM)),
        ],
        out_specs=(
            pl.BlockSpec((tm, GMLP_WIDTH), lambda i: (i, 0)),
            pl.BlockSpec((tm, GDN_WIDTH), lambda i: (i, 0)),
        ),
        scratch_shapes=[
            pltpu.VMEM((CONV_PAD, 3 * GDN_WIDTH), F32),
            pltpu.VMEM((GDN_HEADS, HEAD_DIM, HEAD_DIM), F32),
            pltpu.VMEM((tm, 3 * GDN_WIDTH), BF16),
            pltpu.VMEM((tm, GDN_WIDTH), BF16),
            pltpu.VMEM((tm, 2 * GDN_HEADS), F32),
            pltpu.VMEM((GDN_HEADS, tm), F32),
        ],
        compiler_params=pltpu.CompilerParams(
            dimension_semantics=("arbitrary",), vmem_limit_bytes=VMEM_LIMIT_BYTES),
        name="mix",
    )(h, nw, w_main, wab, wabt, lng, lnb, wsp, bspt, cw, alog_c, dtb_c, alog_r, dtb_r, gnw)


def _out_ple_kernel(apply_final_norm,
                    h_ref, ya_ref, yb_ref, p_ref, wout_ref, wple_ref, pnw_ref, gnw_ref, wg_ref, fnw_ref,
                    o_ref):
    h1 = (h_ref[...] + _dot(ya_ref[...], wout_ref[0:GMLP_WIDTH, :])
          + _dot(yb_ref[...], wout_ref[GMLP_WIDTH:GMLP_WIDTH + GDN_WIDTH, :]))
    e = _rms(_dot(_bf(p_ref[...]), wple_ref[...]), pnw_ref[...])
    gate = jax.nn.sigmoid(_dot(_bf(_rms(h1, gnw_ref[...])), wg_ref[...]))
    h2 = h1 + gate * e
    if apply_final_norm:
        h2 = _rms(h2, fnw_ref[...])
    o_ref[...] = h2


def _out_ple(layer, h, ya, yb, p, wout, wple, pnw, gnw, wg, fnw, apply_final_norm):
    t = h.shape[0]
    tm = TILE_OUT
    spec = functools.partial(_layer_spec, layer)
    tok = lambda width: pl.BlockSpec((tm, width), lambda i: (i, 0))
    return pl.pallas_call(
        functools.partial(_out_ple_kernel, apply_final_norm),
        out_shape=jax.ShapeDtypeStruct((t, D_MODEL), F32),
        grid=(t // tm,),
        in_specs=[
            tok(D_MODEL), tok(GMLP_WIDTH), tok(GDN_WIDTH),
            pl.BlockSpec((None, tm, PLE_DIM), lambda i: (layer, i, 0)),
            spec((GMLP_WIDTH + GDN_WIDTH, D_MODEL)),
            spec((PLE_DIM, D_MODEL)),
            spec((1, D_MODEL)),
            spec((1, D_MODEL)),
            spec((D_MODEL, D_MODEL)),
            pl.BlockSpec((1, D_MODEL), lambda i: (0, 0)),
        ],
        out_specs=tok(D_MODEL),
        compiler_params=pltpu.CompilerParams(
            dimension_semantics=("arbitrary",), vmem_limit_bytes=VMEM_LIMIT_BYTES),
        name="out_ple",
    )(h, ya, yb, p, wout, wple, pnw, gnw, wg, fnw)


def kernel(x, p, norm_w, w_in, ln_v_g, ln_v_b, w_spatial, b_spatial, conv_w, A_log, dt_bias,
           gdn_norm_w, w_out, w_ple, ple_norm_w, ple_gate_norm_w, w_ple_gate, final_norm_w):
    batch, seq, d = x.shape
    depth = w_in.shape[0]
    assert d == D_MODEL and seq % TILE_MIX == 0 and (batch * seq) % TILE_OUT == 0
    assert w_in.shape[2] == MAIN_COLS + 2 * GDN_HEADS
    t = batch * seq
    h = x.reshape(t, d)
    p2 = p.reshape(depth, t, PLE_DIM)
    w_in_t = _bf(jnp.swapaxes(w_in, 1, 2))
    wabt = w_in_t[:, MAIN_COLS:MAIN_COLS + GDN_HEADS, :]
    wab = jnp.swapaxes(w_in_t[:, MAIN_COLS:, :], 1, 2)
    wsp_b, wout_b, wple_b, wg_b = _bf(w_spatial), _bf(w_out), _bf(w_ple), _bf(w_ple_gate)
    bspt = jnp.swapaxes(b_spatial, 1, 2)
    row3 = lambda a: a[:, None, :]
    col3 = lambda a: a[:, :, None]
    for i in range(depth):
        ya, yb = _mix(i, h, row3(norm_w), w_in_t, wab, wabt, row3(ln_v_g), row3(ln_v_b), wsp_b, bspt,
                      conv_w, row3(A_log), row3(dt_bias), col3(A_log), col3(dt_bias), row3(gdn_norm_w), seq)
        h = _out_ple(i, h, ya, yb, p2, wout_b, wple_b, row3(ple_norm_w), row3(ple_gate_norm_w), wg_b,
                     final_norm_w[None], apply_final_norm=(i == depth - 1))
    return h.reshape(batch, seq, d)
```

```python
import functools

import jax
import jax.numpy as jnp
from jax import lax
from jax.experimental import pallas as pl
from jax.experimental.pallas import tpu as pltpu

D_MODEL = 1024
GMLP_WIDTH = 1024
GMLP_GROUPS = 8
GROUP_DIM = 128
SPATIAL_BLOCK = 128
SPATIAL_CHUNK = 64
GDN_WIDTH = 1024
GDN_HEADS = 8
HEAD_DIM = 128
CONV_K = 4
PLE_DIM = 256
EPS = 1e-6
MAIN_COLS = 3 * GMLP_WIDTH + 4 * GDN_WIDTH

GDN_CHUNK = 128
TILE_MIX = 512
TILE_OUT = 512
GDN_GROUP_ROWS = 256
STRIP_COLS = 256
BF16_ROWS = 16
CONV_PAD = 8
VMEM_LIMIT_BYTES = 56 * 1024 * 1024

F32 = jnp.float32
BF16 = jnp.bfloat16
LOG2E = 1.4426950408889634
GELU_K1 = -2.0 * 0.7978845608028654 * LOG2E
GELU_K3 = GELU_K1 * 0.044715


def _bf(x):
    return x.astype(BF16)


def _dot(a, b):
    return jnp.dot(a, b, preferred_element_type=F32)


def _dot_nt(a, b):
    return lax.dot_general(a, b, (((1,), (1,)), ((), ())), preferred_element_type=F32)


def _dot_tn(a, b):
    return lax.dot_general(a, b, (((0,), (0,)), ((), ())), preferred_element_type=F32)


def _rms(x, w):
    return x * lax.rsqrt(jnp.mean(x * x, axis=-1, keepdims=True) + EPS) * w


def _silu(x):
    return x / (1.0 + jnp.exp2(x * -LOG2E))


def _gelu_tanh(x):
    return x / (1.0 + jnp.exp2(x * (GELU_K1 + GELU_K3 * (x * x))))


def _softplus(x):
    return jnp.maximum(x, 0.0) + jnp.log1p(jnp.exp(-jnp.abs(x)))


def _head_cols(base, hd):
    return slice(base + hd * HEAD_DIM, base + (hd + 1) * HEAD_DIM)


def _gmlp_branch(xn, w_ref, lng_ref, lnb_ref, wsp_ref, bspt_ref, ya_ref):
    tm = xn.shape[0]
    strips = range(0, GMLP_WIDTH, STRIP_COLS)

    def proj(base, c0):
        return _dot_nt(xn, w_ref[base + c0:base + c0 + STRIP_COLS, :])

    v = jnp.concatenate([_gelu_tanh(proj(GMLP_WIDTH, c0)) for c0 in strips], axis=1)
    mu = jnp.mean(v, axis=-1, keepdims=True)
    vc = v - mu
    var = jnp.mean(vc * vc, axis=-1, keepdims=True)
    vln = _bf(vc * lax.rsqrt(var + EPS) * lng_ref[...] + lnb_ref[...])
    uz = jnp.concatenate([_gelu_tanh(proj(0, c0)) * _silu(proj(2 * GMLP_WIDTH, c0)) for c0 in strips], axis=1)
    ri = lax.broadcasted_iota(jnp.int32, (SPATIAL_BLOCK, SPATIAL_BLOCK), 0) // SPATIAL_CHUNK
    ci = lax.broadcasted_iota(jnp.int32, (SPATIAL_BLOCK, SPATIAL_BLOCK), 1) // SPATIAL_CHUNK
    chunk_causal = ci <= ri
    for g in range(GMLP_GROUPS):
        gs = slice(g * GROUP_DIM, (g + 1) * GROUP_DIM)
        wm = jnp.where(chunk_causal, wsp_ref[g], jnp.zeros_like(wsp_ref[g]))
        bias = bspt_ref[:, g:g + 1]
        for r in range(tm // SPATIAL_BLOCK):
            rs = slice(r * SPATIAL_BLOCK, (r + 1) * SPATIAL_BLOCK)
            mixed = _dot(wm, vln[rs, gs]) + bias
            ya_ref[rs, gs] = (uz[rs, gs] * mixed).astype(ya_ref.dtype)


def _gdn_prepare(xn, w_ref, wab_ref, wabt_ref, cw_ref, alog_c_ref, dtb_c_ref, alog_r_ref, dtb_r_ref,
                 cs_ref, qkv_ref, zb_ref, col_ref, row_ref):
    tm = xn.shape[0]
    sub = lax.broadcasted_iota(jnp.int32, (CONV_PAD, STRIP_COLS), 0)
    for c0 in range(0, 3 * GDN_WIDTH, STRIP_COLS):
        cols = slice(c0, c0 + STRIP_COLS)
        x = _dot_nt(xn, w_ref[3 * GMLP_WIDTH + c0:3 * GMLP_WIDTH + c0 + STRIP_COLS, :])
        prev = cs_ref[:, cols]
        acc = x * cw_ref[CONV_K - 1:CONV_K, cols]
        for shift in range(1, CONV_K):
            xs = pltpu.roll(x, shift, axis=0)
            head = jnp.where(sub < shift, pltpu.roll(prev, shift, axis=0), xs[0:CONV_PAD])
            shifted = jnp.concatenate([head, xs[CONV_PAD:]], axis=0)
            acc = acc + shifted * cw_ref[CONV_K - 1 - shift:CONV_K - shift, cols]
        cs_ref[:, cols] = x[tm - CONV_PAD:tm]
        act = _silu(acc)
        if c0 >= 2 * GDN_WIDTH:
            qkv_ref[:, cols] = _bf(act)
        else:
            scale = HEAD_DIM ** -0.5 if c0 < GDN_WIDTH else 1.0
            for hd in range(STRIP_COLS // HEAD_DIM):
                t = act[:, _head_cols(0, hd)]
                t = t * (lax.rsqrt(jnp.sum(t * t, axis=-1, keepdims=True) + EPS) * scale)
                qkv_ref[:, _head_cols(c0, hd)] = _bf(t)

    for c0 in range(0, GDN_WIDTH, STRIP_COLS):
        zb_ref[:, c0:c0 + STRIP_COLS] = _bf(_silu(_dot_nt(
            xn, w_ref[3 * GMLP_WIDTH + 3 * GDN_WIDTH + c0:3 * GMLP_WIDTH + 3 * GDN_WIDTH + c0 + STRIP_COLS, :])))

    ab = _dot(xn, wab_ref[...])
    g_col = -jnp.exp(alog_c_ref[...]) * _softplus(ab[:, 0:GDN_HEADS] + dtb_c_ref[...])
    beta = jax.nn.sigmoid(ab[:, GDN_HEADS:2 * GDN_HEADS])
    a_row = _dot_nt(wabt_ref[...], xn)
    g_row = -jnp.exp(alog_r_ref[...]) * _softplus(a_row + dtb_r_ref[...])
    ii = lax.broadcasted_iota(jnp.int32, (GDN_CHUNK, GDN_CHUNK), 0)
    jj = lax.broadcasted_iota(jnp.int32, (GDN_CHUNK, GDN_CHUNK), 1)
    tri_lower = (jj <= ii).astype(F32)
    tri_upper = (ii <= jj).astype(F32)
    for r in range(tm // GDN_CHUNK):
        rs = slice(r * GDN_CHUNK, (r + 1) * GDN_CHUNK)
        col_ref[rs, 0:GDN_HEADS] = jnp.dot(tri_lower, g_col[rs, :], precision=lax.Precision.HIGHEST,
                                           preferred_element_type=F32)
        col_ref[rs, GDN_HEADS:2 * GDN_HEADS] = beta[rs, :]
        row_ref[:, rs] = jnp.dot(g_row[:, rs], tri_upper, precision=lax.Precision.HIGHEST,
                                 preferred_element_type=F32)


def _gdn_chunks(qkv_ref, zb_ref, col_ref, row_ref, gnw_ref, state_ref, yb_ref):
    c = GDN_CHUNK
    n_chunks = qkv_ref.shape[0] // c
    ii = lax.broadcasted_iota(jnp.int32, (c, c), 0)
    jj = lax.broadcasted_iota(jnp.int32, (c, c), 1)
    causal = jj <= ii
    strict = jj < ii
    eye = (ii == jj).astype(F32)
    units = [(j, hd) for j in range(n_chunks) for hd in range(GDN_HEADS)]
    n_units = range(len(units))

    def rows(j):
        return slice(j * c, (j + 1) * c)

    dcols, betas, e_dec, e_rem, e_last = [], [], [], [], []
    for j in range(n_chunks):
        col = col_ref[rows(j), :]
        d = col[:, 0:GDN_HEADS]
        dlast = d[c - 1:c, :]
        dcols.append(d)
        betas.append(col[:, GDN_HEADS:2 * GDN_HEADS])
        e_dec.append(jnp.exp(d))
        e_rem.append(jnp.exp(dlast - d))
        e_last.append(jnp.exp(dlast))

    def colv(per_chunk, u):
        j, hd = units[u]
        return per_chunk[j][:, hd:hd + 1]

    q = [qkv_ref[rows(j), _head_cols(0, hd)] for j, hd in units]
    k = [qkv_ref[rows(j), _head_cols(GDN_WIDTH, hd)] for j, hd in units]
    kf = [k[u].astype(F32) for u in n_units]
    kb = [kf[u] * colv(betas, u) for u in n_units]
    gram = [_dot_nt(jnp.concatenate([_bf(kb[u]), q[u]], axis=0), k[u]) for u in n_units]
    lmat = [jnp.exp(jnp.where(causal, colv(dcols, u) - row_ref[hd:hd + 1, rows(j)], -jnp.inf))
            for u, (j, hd) in enumerate(units)]
    a = [jnp.where(strict, gram[u][0:c] * lmat[u], 0.0) for u in n_units]
    qk = [_bf(gram[u][c:2 * c] * lmat[u]) for u in n_units]
    a_b = [_bf(a[u]) for u in n_units]
    tb = [_bf(eye) - jnp.where((ii // 2) == (jj // 2), a_b[u], jnp.zeros_like(a_b[u])) for u in n_units]
    b = 2
    while b < c:
        off = _bf((((ii // (2 * b)) == (jj // (2 * b))) & ((ii // b) != (jj // b))).astype(F32))
        if b % BF16_ROWS == 0:
            lower = [slice(s0 + b, s0 + 2 * b) for s0 in range(0, c, 2 * b)]
            m = [_dot(jnp.concatenate([tb[u][rs] for rs in lower], axis=0), a_b[u] * off) for u in n_units]
            upd = [_bf(_dot(_bf(m[u]), tb[u])) for u in n_units]
            tb = [jnp.concatenate(
                [piece for n, rs in enumerate(lower)
                 for piece in (tb[u][rs.start - b:rs.start], tb[u][rs] - upd[u][n * b:(n + 1) * b])], axis=0)
                 for u in n_units]
        else:
            m = [_dot(tb[u], a_b[u] * off) for u in n_units]
            tb = [tb[u] - _bf(_dot(_bf(m[u]), tb[u])) for u in n_units]
        b *= 2
    r = [jnp.concatenate([qkv_ref[rows(j), _head_cols(2 * GDN_WIDTH, hd)].astype(F32) * colv(betas, u),
                          kb[u] * colv(e_dec, u)], axis=1) for u, (j, hd) in enumerate(units)]
    uw = [_dot(tb[u], _bf(r[u])) for u in n_units]
    w_b = [_bf(uw[u][:, HEAD_DIM:2 * HEAD_DIM]) for u in n_units]
    q_qk = [jnp.concatenate([_bf(q[u].astype(F32) * colv(e_dec, u)), qk[u]], axis=1) for u in n_units]
    kd = [_bf(kf[u] * colv(e_rem, u)) for u in n_units]
    s = [state_ref[hd] for hd in range(GDN_HEADS)]
    for j in range(n_chunks):
        us = [j * GDN_HEADS + hd for hd in range(GDN_HEADS)]
        s_b = [_bf(s[hd]) for hd in range(GDN_HEADS)]
        v_new = [_bf(uw[u][:, 0:HEAD_DIM] - _dot(w_b[u], s_b[hd])) for hd, u in enumerate(us)]
        o = [_dot(q_qk[u], jnp.concatenate([s_b[hd], v_new[hd]], axis=0)) for hd, u in enumerate(us)]
        s = [s[hd] * colv(e_last, u) + _dot_tn(kd[u], v_new[hd]) for hd, u in enumerate(us)]
        for hd in range(GDN_HEADS):
            on = o[hd] * lax.rsqrt(jnp.mean(o[hd] * o[hd], axis=-1, keepdims=True) + EPS) * gnw_ref[...]
            yb_ref[rows(j), _head_cols(0, hd)] = (
                on * zb_ref[rows(j), _head_cols(0, hd)].astype(F32)).astype(yb_ref.dtype)
    for hd in range(GDN_HEADS):
        state_ref[hd] = s[hd]


def _mix_kernel(tiles_per_seq,
                h_ref, nw_ref, w_ref, wab_ref, wabt_ref, lng_ref, lnb_ref, wsp_ref, bspt_ref,
                cw_ref, alog_c_ref, dtb_c_ref, alog_r_ref, dtb_r_ref, gnw_ref,
                ya_ref, yb_ref,
                cs_ref, state_ref, qkv_ref, zb_ref, col_ref, row_ref):
    @pl.when(pl.program_id(0) % tiles_per_seq == 0)
    def _():
        cs_ref[...] = jnp.zeros(cs_ref.shape, F32)
        state_ref[...] = jnp.zeros(state_ref.shape, F32)

    xn = _bf(_rms(h_ref[...], nw_ref[...]))
    _gmlp_branch(xn, w_ref, lng_ref, lnb_ref, wsp_ref, bspt_ref, ya_ref)
    _gdn_prepare(xn, w_ref, wab_ref, wabt_ref, cw_ref, alog_c_ref, dtb_c_ref, alog_r_ref, dtb_r_ref,
                 cs_ref, qkv_ref, zb_ref, col_ref, row_ref)
    for r0 in range(0, h_ref.shape[0], GDN_GROUP_ROWS):
        rs = slice(r0, r0 + GDN_GROUP_ROWS)
        _gdn_chunks(qkv_ref.at[rs], zb_ref.at[rs], col_ref.at[rs], row_ref.at[:, rs], gnw_ref, state_ref,
                    yb_ref.at[rs])


def _layer_spec(layer, shape):
    return pl.BlockSpec((None,) + shape, lambda i: (layer,) + (0,) * len(shape),
                        pipeline_mode=pl.Buffered(1))


def _mix(layer, h, nw, w_in_t, wab, wabt, lng, lnb, wsp, bspt, cw, alog_c, dtb_c, alog_r, dtb_r, gnw, seq):
    t = h.shape[0]
    tm = TILE_MIX
    spec = functools.partial(_layer_spec, layer)
    return pl.pallas_call(
        functools.partial(_mix_kernel, seq // tm),
        out_shape=(
            jax.ShapeDtypeStruct((t, GMLP_WIDTH), BF16),
            jax.ShapeDtypeStruct((t, GDN_WIDTH), BF16),
        ),
        grid=(t // tm,),
        in_specs=[
            pl.BlockSpec((tm, D_MODEL), lambda i: (i, 0)),
            spec((1, D_MODEL)),
            spec((MAIN_COLS, D_MODEL)),
            spec((D_MODEL, 2 * GDN_HEADS)),
            spec((GDN_HEADS, D_MODEL)),
            spec((1, GMLP_WIDTH)),
            spec((1, GMLP_WIDTH)),
            spec((GMLP_GROUPS, SPATIAL_BLOCK, SPATIAL_BLOCK)),
            spec((SPATIAL_BLOCK, GMLP_GROUPS)),
            spec((CONV_K, 3 * GDN_WIDTH)),
            spec((1, GDN_HEADS)),
            spec((1, GDN_HEADS)),
            spec((GDN_HEADS, 1)),
            spec((GDN_HEADS, 1)),
            spec((1, HEAD_DIM)),
        ],
        out_specs=(
            pl.BlockSpec((tm, GMLP_WIDTH), lambda i: (i, 0)),
            pl.BlockSpec((tm, GDN_WIDTH), lambda i: (i, 0)),
        ),
        scratch_shapes=[
            pltpu.VMEM((CONV_PAD, 3 * GDN_WIDTH), F32),
            pltpu.VMEM((GDN_HEADS, HEAD_DIM, HEAD_DIM), F32),
            pltpu.VMEM((tm, 3 * GDN_WIDTH), BF16),
            pltpu.VMEM((tm, GDN_WIDTH), BF16),
            pltpu.VMEM((tm, 2 * GDN_HEADS), F32),
            pltpu.VMEM((GDN_HEADS, tm), F32),
        ],
        compiler_params=pltpu.CompilerParams(
            dimension_semantics=("arbitrary",), vmem_limit_bytes=VMEM_LIMIT_BYTES),
        name="mix",
    )(h, nw, w_in_t, wab, wabt, lng, lnb, wsp, bspt, cw, alog_c, dtb_c, alog_r, dtb_r, gnw)


def _out_ple_kernel(apply_final_norm,
                    h_ref, ya_ref, yb_ref, p_ref, wout_ref, wple_ref, pnw_ref, gnw_ref, wg_ref, fnw_ref,
                    o_ref):
    h1 = (h_ref[...] + _dot(ya_ref[...], wout_ref[0:GMLP_WIDTH, :])
          + _dot(yb_ref[...], wout_ref[GMLP_WIDTH:GMLP_WIDTH + GDN_WIDTH, :]))
    e = _rms(_dot(_bf(p_ref[...]), wple_ref[...]), pnw_ref[...])
    gate = jax.nn.sigmoid(_dot(_bf(_rms(h1, gnw_ref[...])), wg_ref[...]))
    h2 = h1 + gate * e
    if apply_final_norm:
        h2 = _rms(h2, fnw_ref[...])
    o_ref[...] = h2


def _out_ple(layer, h, ya, yb, p, wout, wple, pnw, gnw, wg, fnw, apply_final_norm):
    t = h.shape[0]
    tm = TILE_OUT
    spec = functools.partial(_layer_spec, layer)
    tok = lambda width: pl.BlockSpec((tm, width), lambda i: (i, 0))
    return pl.pallas_call(
        functools.partial(_out_ple_kernel, apply_final_norm),
        out_shape=jax.ShapeDtypeStruct((t, D_MODEL), F32),
        grid=(t // tm,),
        in_specs=[
            tok(D_MODEL), tok(GMLP_WIDTH), tok(GDN_WIDTH),
            pl.BlockSpec((None, tm, PLE_DIM), lambda i: (layer, i, 0)),
            spec((GMLP_WIDTH + GDN_WIDTH, D_MODEL)),
            spec((PLE_DIM, D_MODEL)),
            spec((1, D_MODEL)),
            spec((1, D_MODEL)),
            spec((D_MODEL, D_MODEL)),
            pl.BlockSpec((1, D_MODEL), lambda i: (0, 0)),
        ],
        out_specs=tok(D_MODEL),
        compiler_params=pltpu.CompilerParams(
            dimension_semantics=("arbitrary",), vmem_limit_bytes=VMEM_LIMIT_BYTES),
        name="out_ple",
    )(h, ya, yb, p, wout, wple, pnw, gnw, wg, fnw)


def kernel(x, p, norm_w, w_in, ln_v_g, ln_v_b, w_spatial, b_spatial, conv_w, A_log, dt_bias,
           gdn_norm_w, w_out, w_ple, ple_norm_w, ple_gate_norm_w, w_ple_gate, final_norm_w):
    batch, seq, d = x.shape
    depth = w_in.shape[0]
    assert d == D_MODEL and seq % TILE_MIX == 0 and (batch * seq) % TILE_OUT == 0
    assert w_in.shape[2] == MAIN_COLS + 2 * GDN_HEADS
    t = batch * seq
    h = x.reshape(t, d)
    p2 = p.reshape(depth, t, PLE_DIM)
    w_in_t = _bf(jnp.swapaxes(w_in, 1, 2))
    wabt = w_in_t[:, MAIN_COLS:MAIN_COLS + GDN_HEADS, :]
    wab = jnp.swapaxes(w_in_t[:, MAIN_COLS:, :], 1, 2)
    wsp_b, wout_b, wple_b, wg_b = _bf(w_spatial), _bf(w_out), _bf(w_ple), _bf(w_ple_gate)
    bspt = jnp.swapaxes(b_spatial, 1, 2)
    row3 = lambda a: a[:, None, :]
    col3 = lambda a: a[:, :, None]
    for i in range(depth):
        ya, yb = _mix(i, h, row3(norm_w), w_in_t, wab, wabt, row3(ln_v_g), row3(ln_v_b), wsp_b, bspt,
                      conv_w, row3(A_log), row3(dt_bias), col3(A_log), col3(dt_bias), row3(gdn_norm_w), seq)
        h = _out_ple(i, h, ya, yb, p2, wout_b, wple_b, row3(ple_norm_w), row3(ple_gate_norm_w), wg_b,
                     final_norm_w[None], apply_final_norm=(i == depth - 1))
    return h.reshape(batch, seq, d)
```

```python
import functools

import jax
import jax.numpy as jnp
from jax import lax
from jax.experimental import pallas as pl
from jax.experimental.pallas import tpu as pltpu

D_MODEL = 1024
GMLP_WIDTH = 1024
GMLP_GROUPS = 8
GROUP_DIM = 128
SPATIAL_BLOCK = 128
SPATIAL_CHUNK = 64
GDN_WIDTH = 1024
GDN_HEADS = 8
HEAD_DIM = 128
CONV_K = 4
PLE_DIM = 256
EPS = 1e-6
MAIN_COLS = 3 * GMLP_WIDTH + 4 * GDN_WIDTH

GDN_CHUNK = 128
TILE_MIX = 512
TILE_OUT = 512
GDN_GROUP_ROWS = 128
STRIP_COLS = 256
BF16_ROWS = 16
CONV_PAD = 8
VMEM_LIMIT_BYTES = 56 * 1024 * 1024

F32 = jnp.float32
BF16 = jnp.bfloat16
LOG2E = 1.4426950408889634
GELU_K1 = -2.0 * 0.7978845608028654 * LOG2E
GELU_K3 = GELU_K1 * 0.044715


def _bf(x):
    return x.astype(BF16)


def _dot(a, b):
    return jnp.dot(a, b, preferred_element_type=F32)


def _dot_nt(a, b):
    return lax.dot_general(a, b, (((1,), (1,)), ((), ())), preferred_element_type=F32)


def _dot_tn(a, b):
    return lax.dot_general(a, b, (((0,), (0,)), ((), ())), preferred_element_type=F32)


def _rms(x, w):
    return x * lax.rsqrt(jnp.mean(x * x, axis=-1, keepdims=True) + EPS) * w


def _silu(x):
    return x / (1.0 + jnp.exp2(x * -LOG2E))


def _gelu_tanh(x):
    return x / (1.0 + jnp.exp2(x * (GELU_K1 + GELU_K3 * (x * x))))


def _softplus(x):
    return jnp.maximum(x, 0.0) + jnp.log1p(jnp.exp(-jnp.abs(x)))


def _head_cols(base, hd):
    return slice(base + hd * HEAD_DIM, base + (hd + 1) * HEAD_DIM)


def _gmlp_branch(xn, w_ref, lng_ref, lnb_ref, wsp_ref, bspt_ref, ya_ref):
    tm = xn.shape[0]
    strips = range(0, GMLP_WIDTH, STRIP_COLS)

    def proj(base, c0):
        return _dot_nt(xn, w_ref[base + c0:base + c0 + STRIP_COLS, :])

    v = jnp.concatenate([_gelu_tanh(proj(GMLP_WIDTH, c0)) for c0 in strips], axis=1)
    mu = jnp.mean(v, axis=-1, keepdims=True)
    vc = v - mu
    var = jnp.mean(vc * vc, axis=-1, keepdims=True)
    vln = _bf(vc * lax.rsqrt(var + EPS) * lng_ref[...] + lnb_ref[...])
    uz = jnp.concatenate([_gelu_tanh(proj(0, c0)) * _silu(proj(2 * GMLP_WIDTH, c0)) for c0 in strips], axis=1)
    ri = lax.broadcasted_iota(jnp.int32, (SPATIAL_BLOCK, SPATIAL_BLOCK), 0) // SPATIAL_CHUNK
    ci = lax.broadcasted_iota(jnp.int32, (SPATIAL_BLOCK, SPATIAL_BLOCK), 1) // SPATIAL_CHUNK
    chunk_causal = ci <= ri
    for g in range(GMLP_GROUPS):
        gs = slice(g * GROUP_DIM, (g + 1) * GROUP_DIM)
        wm = jnp.where(chunk_causal, wsp_ref[g], jnp.zeros_like(wsp_ref[g]))
        bias = bspt_ref[:, g:g + 1]
        for r in range(tm // SPATIAL_BLOCK):
            rs = slice(r * SPATIAL_BLOCK, (r + 1) * SPATIAL_BLOCK)
            mixed = _dot(wm, vln[rs, gs]) + bias
            ya_ref[rs, gs] = (uz[rs, gs] * mixed).astype(ya_ref.dtype)


def _gdn_prepare(xn, w_ref, wab_ref, wabt_ref, cw_ref, alog_c_ref, dtb_c_ref, alog_r_ref, dtb_r_ref,
                 cs_ref, qkv_ref, zb_ref, col_ref, row_ref):
    tm = xn.shape[0]
    sub = lax.broadcasted_iota(jnp.int32, (CONV_PAD, STRIP_COLS), 0)
    for c0 in range(0, 3 * GDN_WIDTH, STRIP_COLS):
        cols = slice(c0, c0 + STRIP_COLS)
        x = _dot_nt(xn, w_ref[3 * GMLP_WIDTH + c0:3 * GMLP_WIDTH + c0 + STRIP_COLS, :])
        prev = cs_ref[:, cols]
        acc = x * cw_ref[CONV_K - 1:CONV_K, cols]
        for shift in range(1, CONV_K):
            xs = pltpu.roll(x, shift, axis=0)
            head = jnp.where(sub < shift, pltpu.roll(prev, shift, axis=0), xs[0:CONV_PAD])
            shifted = jnp.concatenate([head, xs[CONV_PAD:]], axis=0)
            acc = acc + shifted * cw_ref[CONV_K - 1 - shift:CONV_K - shift, cols]
        cs_ref[:, cols] = x[tm - CONV_PAD:tm]
        act = _silu(acc)
        if c0 >= 2 * GDN_WIDTH:
            qkv_ref[:, cols] = _bf(act)
        else:
            scale = HEAD_DIM ** -0.5 if c0 < GDN_WIDTH else 1.0
            for hd in range(STRIP_COLS // HEAD_DIM):
                t = act[:, _head_cols(0, hd)]
                t = t * (lax.rsqrt(jnp.sum(t * t, axis=-1, keepdims=True) + EPS) * scale)
                qkv_ref[:, _head_cols(c0, hd)] = _bf(t)

    for c0 in range(0, GDN_WIDTH, STRIP_COLS):
        zb_ref[:, c0:c0 + STRIP_COLS] = _bf(_silu(_dot_nt(
            xn, w_ref[3 * GMLP_WIDTH + 3 * GDN_WIDTH + c0:3 * GMLP_WIDTH + 3 * GDN_WIDTH + c0 + STRIP_COLS, :])))

    ab = _dot(xn, wab_ref[...])
    g_col = -jnp.exp(alog_c_ref[...]) * _softplus(ab[:, 0:GDN_HEADS] + dtb_c_ref[...])
    beta = jax.nn.sigmoid(ab[:, GDN_HEADS:2 * GDN_HEADS])
    a_row = _dot_nt(wabt_ref[...], xn)
    g_row = -jnp.exp(alog_r_ref[...]) * _softplus(a_row + dtb_r_ref[...])
    ii = lax.broadcasted_iota(jnp.int32, (GDN_CHUNK, GDN_CHUNK), 0)
    jj = lax.broadcasted_iota(jnp.int32, (GDN_CHUNK, GDN_CHUNK), 1)
    tri_lower = (jj <= ii).astype(F32)
    tri_upper = (ii <= jj).astype(F32)
    for r in range(tm // GDN_CHUNK):
        rs = slice(r * GDN_CHUNK, (r + 1) * GDN_CHUNK)
        col_ref[rs, 0:GDN_HEADS] = jnp.dot(tri_lower, g_col[rs, :], precision=lax.Precision.HIGHEST,
                                           preferred_element_type=F32)
        col_ref[rs, GDN_HEADS:2 * GDN_HEADS] = beta[rs, :]
        row_ref[:, rs] = jnp.dot(g_row[:, rs], tri_upper, precision=lax.Precision.HIGHEST,
                                 preferred_element_type=F32)


def _gdn_chunks(qkv_ref, zb_ref, col_ref, row_ref, gnw_ref, state_ref, yb_ref):
    c = GDN_CHUNK
    n_chunks = qkv_ref.shape[0] // c
    ii = lax.broadcasted_iota(jnp.int32, (c, c), 0)
    jj = lax.broadcasted_iota(jnp.int32, (c, c), 1)
    causal = jj <= ii
    strict = jj < ii
    eye = (ii == jj).astype(F32)
    units = [(j, hd) for j in range(n_chunks) for hd in range(GDN_HEADS)]
    n_units = range(len(units))

    def rows(j):
        return slice(j * c, (j + 1) * c)

    dcols, betas, e_dec, e_rem, e_last = [], [], [], [], []
    for j in range(n_chunks):
        col = col_ref[rows(j), :]
        d = col[:, 0:GDN_HEADS]
        dlast = d[c - 1:c, :]
        dcols.append(d)
        betas.append(col[:, GDN_HEADS:2 * GDN_HEADS])
        e_dec.append(jnp.exp(d))
        e_rem.append(jnp.exp(dlast - d))
        e_last.append(jnp.exp(dlast))

    def colv(per_chunk, u):
        j, hd = units[u]
        return per_chunk[j][:, hd:hd + 1]

    q = [qkv_ref[rows(j), _head_cols(0, hd)] for j, hd in units]
    k = [qkv_ref[rows(j), _head_cols(GDN_WIDTH, hd)] for j, hd in units]
    kf = [k[u].astype(F32) for u in n_units]
    kb = [kf[u] * colv(betas, u) for u in n_units]
    gram = [_dot_nt(jnp.concatenate([_bf(kb[u]), q[u]], axis=0), k[u]) for u in n_units]
    lmat = [jnp.exp(jnp.where(causal, colv(dcols, u) - row_ref[hd:hd + 1, rows(j)], -jnp.inf))
            for u, (j, hd) in enumerate(units)]
    a = [jnp.where(strict, gram[u][0:c] * lmat[u], 0.0) for u in n_units]
    qk = [_bf(gram[u][c:2 * c] * lmat[u]) for u in n_units]
    a_b = [_bf(a[u]) for u in n_units]
    tb = [_bf(eye) - jnp.where((ii // 2) == (jj // 2), a_b[u], jnp.zeros_like(a_b[u])) for u in n_units]
    b = 2
    while b < c:
        off = _bf((((ii // (2 * b)) == (jj // (2 * b))) & ((ii // b) != (jj // b))).astype(F32))
        if b % BF16_ROWS == 0:
            lower = [slice(s0 + b, s0 + 2 * b) for s0 in range(0, c, 2 * b)]
            m = [_dot(jnp.concatenate([tb[u][rs] for rs in lower], axis=0), a_b[u] * off) for u in n_units]
            upd = [_bf(_dot(_bf(m[u]), tb[u])) for u in n_units]
            tb = [jnp.concatenate(
                [piece for n, rs in enumerate(lower)
                 for piece in (tb[u][rs.start - b:rs.start], tb[u][rs] - upd[u][n * b:(n + 1) * b])], axis=0)
                 for u in n_units]
        else:
            m = [_dot(tb[u], a_b[u] * off) for u in n_units]
            tb = [tb[u] - _bf(_dot(_bf(m[u]), tb[u])) for u in n_units]
        b *= 2
    r = [jnp.concatenate([qkv_ref[rows(j), _head_cols(2 * GDN_WIDTH, hd)].astype(F32) * colv(betas, u),
                          kb[u] * colv(e_dec, u)], axis=1) for u, (j, hd) in enumerate(units)]
    uw = [_dot(tb[u], _bf(r[u])) for u in n_units]
    w_b = [_bf(uw[u][:, HEAD_DIM:2 * HEAD_DIM]) for u in n_units]
    q_qk = [jnp.concatenate([_bf(q[u].astype(F32) * colv(e_dec, u)), qk[u]], axis=1) for u in n_units]
    kd = [_bf(kf[u] * colv(e_rem, u)) for u in n_units]
    s = [state_ref[hd] for hd in range(GDN_HEADS)]
    for j in range(n_chunks):
        us = [j * GDN_HEADS + hd for hd in range(GDN_HEADS)]
        s_b = [_bf(s[hd]) for hd in range(GDN_HEADS)]
        v_new = [_bf(uw[u][:, 0:HEAD_DIM] - _dot(w_b[u], s_b[hd])) for hd, u in enumerate(us)]
        o = [_dot(q_qk[u], jnp.concatenate([s_b[hd], v_new[hd]], axis=0)) for hd, u in enumerate(us)]
        s = [s[hd] * colv(e_last, u) + _dot_tn(kd[u], v_new[hd]) for hd, u in enumerate(us)]
        for hd in range(GDN_HEADS):
            on = o[hd] * lax.rsqrt(jnp.mean(o[hd] * o[hd], axis=-1, keepdims=True) + EPS) * gnw_ref[...]
            yb_ref[rows(j), _head_cols(0, hd)] = (
                on * zb_ref[rows(j), _head_cols(0, hd)].astype(F32)).astype(yb_ref.dtype)
    for hd in range(GDN_HEADS):
        state_ref[hd] = s[hd]


def _mix_kernel(tiles_per_seq,
                h_ref, nw_ref, w_ref, wab_ref, wabt_ref, lng_ref, lnb_ref, wsp_ref, bspt_ref,
                cw_ref, alog_c_ref, dtb_c_ref, alog_r_ref, dtb_r_ref, gnw_ref,
                ya_ref, yb_ref,
                cs_ref, state_ref, qkv_ref, zb_ref, col_ref, row_ref):
    @pl.when(pl.program_id(0) % tiles_per_seq == 0)
    def _():
        cs_ref[...] = jnp.zeros(cs_ref.shape, F32)
        state_ref[...] = jnp.zeros(state_ref.shape, F32)

    xn = _bf(_rms(h_ref[...], nw_ref[...]))
    _gmlp_branch(xn, w_ref, lng_ref, lnb_ref, wsp_ref, bspt_ref, ya_ref)
    _gdn_prepare(xn, w_ref, wab_ref, wabt_ref, cw_ref, alog_c_ref, dtb_c_ref, alog_r_ref, dtb_r_ref,
                 cs_ref, qkv_ref, zb_ref, col_ref, row_ref)
    for r0 in range(0, h_ref.shape[0], GDN_GROUP_ROWS):
        rs = slice(r0, r0 + GDN_GROUP_ROWS)
        _gdn_chunks(qkv_ref.at[rs], zb_ref.at[rs], col_ref.at[rs], row_ref.at[:, rs], gnw_ref, state_ref,
                    yb_ref.at[rs])


def _layer_spec(layer, shape):
    return pl.BlockSpec((None,) + shape, lambda i: (layer,) + (0,) * len(shape),
                        pipeline_mode=pl.Buffered(1))


def _mix(layer, h, nw, w_in_t, wab, wabt, lng, lnb, wsp, bspt, cw, alog_c, dtb_c, alog_r, dtb_r, gnw, seq):
    t = h.shape[0]
    tm = TILE_MIX
    spec = functools.partial(_layer_spec, layer)
    return pl.pallas_call(
        functools.partial(_mix_kernel, seq // tm),
        out_shape=(
            jax.ShapeDtypeStruct((t, GMLP_WIDTH), BF16),
            jax.ShapeDtypeStruct((t, GDN_WIDTH), BF16),
        ),
        grid=(t // tm,),
        in_specs=[
            pl.BlockSpec((tm, D_MODEL), lambda i: (i, 0)),
            spec((1, D_MODEL)),
            spec((MAIN_COLS, D_MODEL)),
            spec((D_MODEL, 2 * GDN_HEADS)),
            spec((GDN_HEADS, D_MODEL)),
            spec((1, GMLP_WIDTH)),
            spec((1, GMLP_WIDTH)),
            spec((GMLP_GROUPS, SPATIAL_BLOCK, SPATIAL_BLOCK)),
            spec((SPATIAL_BLOCK, GMLP_GROUPS)),
            spec((CONV_K, 3 * GDN_WIDTH)),
            spec((1, GDN_HEADS)),
            spec((1, GDN_HEADS)),
            spec((GDN_HEADS, 1)),
            spec((GDN_HEADS, 1)),
            spec((1, HEAD_DIM)),
        ],
        out_specs=(
            pl.BlockSpec((tm, GMLP_WIDTH), lambda i: (i, 0)),
            pl.BlockSpec((tm, GDN_WIDTH), lambda i: (i, 0)),
        ),
        scratch_shapes=[
            pltpu.VMEM((CONV_PAD, 3 * GDN_WIDTH), F32),
            pltpu.VMEM((GDN_HEADS, HEAD_DIM, HEAD_DIM), F32),
            pltpu.VMEM((tm, 3 * GDN_WIDTH), BF16),
            pltpu.VMEM((tm, GDN_WIDTH), BF16),
            pltpu.VMEM((tm, 2 * GDN_HEADS), F32),
            pltpu.VMEM((GDN_HEADS, tm), F32),
        ],
        compiler_params=pltpu.CompilerParams(
            dimension_semantics=("arbitrary",), vmem_limit_bytes=VMEM_LIMIT_BYTES),
        name="mix",
    )(h, nw, w_in_t, wab, wabt, lng, lnb, wsp, bspt, cw, alog_c, dtb_c, alog_r, dtb_r, gnw)


def _out_ple_kernel(apply_final_norm,
                    h_ref, ya_ref, yb_ref, p_ref, wout_ref, wple_ref, pnw_ref, gnw_ref, wg_ref, fnw_ref,
                    o_ref):
    h1 = (h_ref[...] + _dot(ya_ref[...], wout_ref[0:GMLP_WIDTH, :])
          + _dot(yb_ref[...], wout_ref[GMLP_WIDTH:GMLP_WIDTH + GDN_WIDTH, :]))
    e = _rms(_dot(_bf(p_ref[...]), wple_ref[...]), pnw_ref[...])
    gate = jax.nn.sigmoid(_dot(_bf(_rms(h1, gnw_ref[...])), wg_ref[...]))
    h2 = h1 + gate * e
    if apply_final_norm:
        h2 = _rms(h2, fnw_ref[...])
    o_ref[...] = h2


def _out_ple(layer, h, ya, yb, p, wout, wple, pnw, gnw, wg, fnw, apply_final_norm):
    t = h.shape[0]
    tm = TILE_OUT
    spec = functools.partial(_layer_spec, layer)
    tok = lambda width: pl.BlockSpec((tm, width), lambda i: (i, 0))
    return pl.pallas_call(
        functools.partial(_out_ple_kernel, apply_final_norm),
        out_shape=jax.ShapeDtypeStruct((t, D_MODEL), F32),
        grid=(t // tm,),
        in_specs=[
            tok(D_MODEL), tok(GMLP_WIDTH), tok(GDN_WIDTH),
            pl.BlockSpec((None, tm, PLE_DIM), lambda i: (layer, i, 0)),
            spec((GMLP_WIDTH + GDN_WIDTH, D_MODEL)),
            spec((PLE_DIM, D_MODEL)),
            spec((1, D_MODEL)),
            spec((1, D_MODEL)),
            spec((D_MODEL, D_MODEL)),
            pl.BlockSpec((1, D_MODEL), lambda i: (0, 0)),
        ],
        out_specs=tok(D_MODEL),
        compiler_params=pltpu.CompilerParams(
            dimension_semantics=("arbitrary",), vmem_limit_bytes=VMEM_LIMIT_BYTES),
        name="out_ple",
    )(h, ya, yb, p, wout, wple, pnw, gnw, wg, fnw)


def kernel(x, p, norm_w, w_in, ln_v_g, ln_v_b, w_spatial, b_spatial, conv_w, A_log, dt_bias,
           gdn_norm_w, w_out, w_ple, ple_norm_w, ple_gate_norm_w, w_ple_gate, final_norm_w):
    batch, seq, d = x.shape
    depth = w_in.shape[0]
    assert d == D_MODEL and seq % TILE_MIX == 0 and (batch * seq) % TILE_OUT == 0
    assert w_in.shape[2] == MAIN_COLS + 2 * GDN_HEADS
    t = batch * seq
    h = x.reshape(t, d)
    p2 = p.reshape(depth, t, PLE_DIM)
    w_in_t = _bf(jnp.swapaxes(w_in, 1, 2))
    wabt = w_in_t[:, MAIN_COLS:MAIN_COLS + GDN_HEADS, :]
    wab = jnp.swapaxes(w_in_t[:, MAIN_COLS:, :], 1, 2)
    wsp_b, wout_b, wple_b, wg_b = _bf(w_spatial), _bf(w_out), _bf(w_ple), _bf(w_ple_gate)
    bspt = jnp.swapaxes(b_spatial, 1, 2)
    row3 = lambda a: a[:, None, :]
    col3 = lambda a: a[:, :, None]
    for i in range(depth):
        ya, yb = _mix(i, h, row3(norm_w), w_in_t, wab, wabt, row3(ln_v_g), row3(ln_v_b), wsp_b, bspt,
                      conv_w, row3(A_log), row3(dt_bias), col3(A_log), col3(dt_bias), row3(gdn_norm_w), seq)
        h = _out_ple(i, h, ya, yb, p2, wout_b, wple_b, row3(ple_norm_w), row3(ple_gate_norm_w), wg_b,
                     final_norm_w[None], apply_final_norm=(i == depth - 1))
    return h.reshape(batch, seq, d)
```

```python
import functools

import jax
import jax.numpy as jnp
from jax import lax
from jax.experimental import pallas as pl
from jax.experimental.pallas import tpu as pltpu

D_MODEL = 1024
GMLP_WIDTH = 1024
GMLP_GROUPS = 8
GROUP_DIM = 128
SPATIAL_BLOCK = 128
SPATIAL_CHUNK = 64
GDN_WIDTH = 1024
GDN_HEADS = 8
HEAD_DIM = 128
CONV_K = 4
PLE_DIM = 256
EPS = 1e-6
MAIN_COLS = 3 * GMLP_WIDTH + 4 * GDN_WIDTH

GDN_CHUNK = 128
TILE_MIX = 512
TILE_OUT = 512
GDN_GROUP_ROWS = 256
STRIP_COLS = 128
BF16_ROWS = 16
CONV_PAD = 8
VMEM_LIMIT_BYTES = 56 * 1024 * 1024

F32 = jnp.float32
BF16 = jnp.bfloat16
LOG2E = 1.4426950408889634
GELU_K1 = -2.0 * 0.7978845608028654 * LOG2E
GELU_K3 = GELU_K1 * 0.044715


def _bf(x):
    return x.astype(BF16)


def _dot(a, b):
    return jnp.dot(a, b, preferred_element_type=F32)


def _dot_nt(a, b):
    return lax.dot_general(a, b, (((1,), (1,)), ((), ())), preferred_element_type=F32)


def _dot_tn(a, b):
    return lax.dot_general(a, b, (((0,), (0,)), ((), ())), preferred_element_type=F32)


def _rms(x, w):
    return x * lax.rsqrt(jnp.mean(x * x, axis=-1, keepdims=True) + EPS) * w


def _silu(x):
    return x / (1.0 + jnp.exp2(x * -LOG2E))


def _gelu_tanh(x):
    return x / (1.0 + jnp.exp2(x * (GELU_K1 + GELU_K3 * (x * x))))


def _softplus(x):
    return jnp.maximum(x, 0.0) + jnp.log1p(jnp.exp(-jnp.abs(x)))


def _head_cols(base, hd):
    return slice(base + hd * HEAD_DIM, base + (hd + 1) * HEAD_DIM)


def _gmlp_branch(xn, w_ref, lng_ref, lnb_ref, wsp_ref, bspt_ref, ya_ref):
    tm = xn.shape[0]
    strips = range(0, GMLP_WIDTH, STRIP_COLS)

    def proj(base, c0):
        return _dot_nt(xn, w_ref[base + c0:base + c0 + STRIP_COLS, :])

    v = jnp.concatenate([_gelu_tanh(proj(GMLP_WIDTH, c0)) for c0 in strips], axis=1)
    mu = jnp.mean(v, axis=-1, keepdims=True)
    vc = v - mu
    var = jnp.mean(vc * vc, axis=-1, keepdims=True)
    vln = _bf(vc * lax.rsqrt(var + EPS) * lng_ref[...] + lnb_ref[...])
    uz = jnp.concatenate([_gelu_tanh(proj(0, c0)) * _silu(proj(2 * GMLP_WIDTH, c0)) for c0 in strips], axis=1)
    ri = lax.broadcasted_iota(jnp.int32, (SPATIAL_BLOCK, SPATIAL_BLOCK), 0) // SPATIAL_CHUNK
    ci = lax.broadcasted_iota(jnp.int32, (SPATIAL_BLOCK, SPATIAL_BLOCK), 1) // SPATIAL_CHUNK
    chunk_causal = ci <= ri
    for g in range(GMLP_GROUPS):
        gs = slice(g * GROUP_DIM, (g + 1) * GROUP_DIM)
        wm = jnp.where(chunk_causal, wsp_ref[g], jnp.zeros_like(wsp_ref[g]))
        bias = bspt_ref[:, g:g + 1]
        for r in range(tm // SPATIAL_BLOCK):
            rs = slice(r * SPATIAL_BLOCK, (r + 1) * SPATIAL_BLOCK)
            mixed = _dot(wm, vln[rs, gs]) + bias
            ya_ref[rs, gs] = (uz[rs, gs] * mixed).astype(ya_ref.dtype)


def _gdn_prepare(xn, w_ref, wab_ref, wabt_ref, cw_ref, alog_c_ref, dtb_c_ref, alog_r_ref, dtb_r_ref,
                 cs_ref, qkv_ref, zb_ref, col_ref, row_ref):
    tm = xn.shape[0]
    sub = lax.broadcasted_iota(jnp.int32, (CONV_PAD, STRIP_COLS), 0)
    for c0 in range(0, 3 * GDN_WIDTH, STRIP_COLS):
        cols = slice(c0, c0 + STRIP_COLS)
        x = _dot_nt(xn, w_ref[3 * GMLP_WIDTH + c0:3 * GMLP_WIDTH + c0 + STRIP_COLS, :])
        prev = cs_ref[:, cols]
        acc = x * cw_ref[CONV_K - 1:CONV_K, cols]
        for shift in range(1, CONV_K):
            xs = pltpu.roll(x, shift, axis=0)
            head = jnp.where(sub < shift, pltpu.roll(prev, shift, axis=0), xs[0:CONV_PAD])
            shifted = jnp.concatenate([head, xs[CONV_PAD:]], axis=0)
            acc = acc + shifted * cw_ref[CONV_K - 1 - shift:CONV_K - shift, cols]
        cs_ref[:, cols] = x[tm - CONV_PAD:tm]
        act = _silu(acc)
        if c0 >= 2 * GDN_WIDTH:
            qkv_ref[:, cols] = _bf(act)
        else:
            scale = HEAD_DIM ** -0.5 if c0 < GDN_WIDTH else 1.0
            for hd in range(STRIP_COLS // HEAD_DIM):
                t = act[:, _head_cols(0, hd)]
                t = t * (lax.rsqrt(jnp.sum(t * t, axis=-1, keepdims=True) + EPS) * scale)
                qkv_ref[:, _head_cols(c0, hd)] = _bf(t)

    for c0 in range(0, GDN_WIDTH, STRIP_COLS):
        zb_ref[:, c0:c0 + STRIP_COLS] = _bf(_silu(_dot_nt(
            xn, w_ref[3 * GMLP_WIDTH + 3 * GDN_WIDTH + c0:3 * GMLP_WIDTH + 3 * GDN_WIDTH + c0 + STRIP_COLS, :])))

    ab = _dot(xn, wab_ref[...])
    g_col = -jnp.exp(alog_c_ref[...]) * _softplus(ab[:, 0:GDN_HEADS] + dtb_c_ref[...])
    beta = jax.nn.sigmoid(ab[:, GDN_HEADS:2 * GDN_HEADS])
    a_row = _dot_nt(wabt_ref[...], xn)
    g_row = -jnp.exp(alog_r_ref[...]) * _softplus(a_row + dtb_r_ref[...])
    ii = lax.broadcasted_iota(jnp.int32, (GDN_CHUNK, GDN_CHUNK), 0)
    jj = lax.broadcasted_iota(jnp.int32, (GDN_CHUNK, GDN_CHUNK), 1)
    tri_lower = (jj <= ii).astype(F32)
    tri_upper = (ii <= jj).astype(F32)
    for r in range(tm // GDN_CHUNK):
        rs = slice(r * GDN_CHUNK, (r + 1) * GDN_CHUNK)
        col_ref[rs, 0:GDN_HEADS] = jnp.dot(tri_lower, g_col[rs, :], precision=lax.Precision.HIGHEST,
                                           preferred_element_type=F32)
        col_ref[rs, GDN_HEADS:2 * GDN_HEADS] = beta[rs, :]
        row_ref[:, rs] = jnp.dot(g_row[:, rs], tri_upper, precision=lax.Precision.HIGHEST,
                                 preferred_element_type=F32)


def _gdn_chunks(qkv_ref, zb_ref, col_ref, row_ref, gnw_ref, state_ref, yb_ref):
    c = GDN_CHUNK
    n_chunks = qkv_ref.shape[0] // c
    ii = lax.broadcasted_iota(jnp.int32, (c, c), 0)
    jj = lax.broadcasted_iota(jnp.int32, (c, c), 1)
    causal = jj <= ii
    strict = jj < ii
    eye = (ii == jj).astype(F32)
    units = [(j, hd) for j in range(n_chunks) for hd in range(GDN_HEADS)]
    n_units = range(len(units))

    def rows(j):
        return slice(j * c, (j + 1) * c)

    dcols, betas, e_dec, e_rem, e_last = [], [], [], [], []
    for j in range(n_chunks):
        col = col_ref[rows(j), :]
        d = col[:, 0:GDN_HEADS]
        dlast = d[c - 1:c, :]
        dcols.append(d)
        betas.append(col[:, GDN_HEADS:2 * GDN_HEADS])
        e_dec.append(jnp.exp(d))
        e_rem.append(jnp.exp(dlast - d))
        e_last.append(jnp.exp(dlast))

    def colv(per_chunk, u):
        j, hd = units[u]
        return per_chunk[j][:, hd:hd + 1]

    q = [qkv_ref[rows(j), _head_cols(0, hd)] for j, hd in units]
    k = [qkv_ref[rows(j), _head_cols(GDN_WIDTH, hd)] for j, hd in units]
    kf = [k[u].astype(F32) for u in n_units]
    kb = [kf[u] * colv(betas, u) for u in n_units]
    gram = [_dot_nt(jnp.concatenate([_bf(kb[u]), q[u]], axis=0), k[u]) for u in n_units]
    lmat = [jnp.exp(jnp.where(causal, colv(dcols, u) - row_ref[hd:hd + 1, rows(j)], -jnp.inf))
            for u, (j, hd) in enumerate(units)]
    a = [jnp.where(strict, gram[u][0:c] * lmat[u], 0.0) for u in n_units]
    qk = [_bf(gram[u][c:2 * c] * lmat[u]) for u in n_units]
    a_b = [_bf(a[u]) for u in n_units]
    tb = [_bf(eye) - jnp.where((ii // 2) == (jj // 2), a_b[u], jnp.zeros_like(a_b[u])) for u in n_units]
    b = 2
    while b < c:
        off = _bf((((ii // (2 * b)) == (jj // (2 * b))) & ((ii // b) != (jj // b))).astype(F32))
        if b % BF16_ROWS == 0:
            lower = [slice(s0 + b, s0 + 2 * b) for s0 in range(0, c, 2 * b)]
            m = [_dot(jnp.concatenate([tb[u][rs] for rs in lower], axis=0), a_b[u] * off) for u in n_units]
            upd = [_bf(_dot(_bf(m[u]), tb[u])) for u in n_units]
            tb = [jnp.concatenate(
                [piece for n, rs in enumerate(lower)
                 for piece in (tb[u][rs.start - b:rs.start], tb[u][rs] - upd[u][n * b:(n + 1) * b])], axis=0)
                 for u in n_units]
        else:
            m = [_dot(tb[u], a_b[u] * off) for u in n_units]
            tb = [tb[u] - _bf(_dot(_bf(m[u]), tb[u])) for u in n_units]
        b *= 2
    r = [jnp.concatenate([qkv_ref[rows(j), _head_cols(2 * GDN_WIDTH, hd)].astype(F32) * colv(betas, u),
                          kb[u] * colv(e_dec, u)], axis=1) for u, (j, hd) in enumerate(units)]
    uw = [_dot(tb[u], _bf(r[u])) for u in n_units]
    w_b = [_bf(uw[u][:, HEAD_DIM:2 * HEAD_DIM]) for u in n_units]
    q_qk = [jnp.concatenate([_bf(q[u].astype(F32) * colv(e_dec, u)), qk[u]], axis=1) for u in n_units]
    kd = [_bf(kf[u] * colv(e_rem, u)) for u in n_units]
    s = [state_ref[hd] for hd in range(GDN_HEADS)]
    for j in range(n_chunks):
        us = [j * GDN_HEADS + hd for hd in range(GDN_HEADS)]
        s_b = [_bf(s[hd]) for hd in range(GDN_HEADS)]
        v_new = [_bf(uw[u][:, 0:HEAD_DIM] - _dot(w_b[u], s_b[hd])) for hd, u in enumerate(us)]
        o = [_dot(q_qk[u], jnp.concatenate([s_b[hd], v_new[hd]], axis=0)) for hd, u in enumerate(us)]
        s = [s[hd] * colv(e_last, u) + _dot_tn(kd[u], v_new[hd]) for hd, u in enumerate(us)]
        for hd in range(GDN_HEADS):
            on = o[hd] * lax.rsqrt(jnp.mean(o[hd] * o[hd], axis=-1, keepdims=True) + EPS) * gnw_ref[...]
            yb_ref[rows(j), _head_cols(0, hd)] = (
                on * zb_ref[rows(j), _head_cols(0, hd)].astype(F32)).astype(yb_ref.dtype)
    for hd in range(GDN_HEADS):
        state_ref[hd] = s[hd]


def _mix_kernel(tiles_per_seq,
                h_ref, nw_ref, w_ref, wab_ref, wabt_ref, lng_ref, lnb_ref, wsp_ref, bspt_ref,
                cw_ref, alog_c_ref, dtb_c_ref, alog_r_ref, dtb_r_ref, gnw_ref,
                ya_ref, yb_ref,
                cs_ref, state_ref, qkv_ref, zb_ref, col_ref, row_ref):
    @pl.when(pl.program_id(0) % tiles_per_seq == 0)
    def _():
        cs_ref[...] = jnp.zeros(cs_ref.shape, F32)
        state_ref[...] = jnp.zeros(state_ref.shape, F32)

    xn = _bf(_rms(h_ref[...], nw_ref[...]))
    _gmlp_branch(xn, w_ref, lng_ref, lnb_ref, wsp_ref, bspt_ref, ya_ref)
    _gdn_prepare(xn, w_ref, wab_ref, wabt_ref, cw_ref, alog_c_ref, dtb_c_ref, alog_r_ref, dtb_r_ref,
                 cs_ref, qkv_ref, zb_ref, col_ref, row_ref)
    for r0 in range(0, h_ref.shape[0], GDN_GROUP_ROWS):
        rs = slice(r0, r0 + GDN_GROUP_ROWS)
        _gdn_chunks(qkv_ref.at[rs], zb_ref.at[rs], col_ref.at[rs], row_ref.at[:, rs], gnw_ref, state_ref,
                    yb_ref.at[rs])


def _layer_spec(layer, shape):
    return pl.BlockSpec((None,) + shape, lambda i: (layer,) + (0,) * len(shape),
                        pipeline_mode=pl.Buffered(1))


def _mix(layer, h, nw, w_in_t, wab, wabt, lng, lnb, wsp, bspt, cw, alog_c, dtb_c, alog_r, dtb_r, gnw, seq):
    t = h.shape[0]
    tm = TILE_MIX
    spec = functools.partial(_layer_spec, layer)
    return pl.pallas_call(
        functools.partial(_mix_kernel, seq // tm),
        out_shape=(
            jax.ShapeDtypeStruct((t, GMLP_WIDTH), BF16),
            jax.ShapeDtypeStruct((t, GDN_WIDTH), BF16),
        ),
        grid=(t // tm,),
        in_specs=[
            pl.BlockSpec((tm, D_MODEL), lambda i: (i, 0)),
            spec((1, D_MODEL)),
            spec((MAIN_COLS, D_MODEL)),
            spec((D_MODEL, 2 * GDN_HEADS)),
            spec((GDN_HEADS, D_MODEL)),
            spec((1, GMLP_WIDTH)),
            spec((1, GMLP_WIDTH)),
            spec((GMLP_GROUPS, SPATIAL_BLOCK, SPATIAL_BLOCK)),
            spec((SPATIAL_BLOCK, GMLP_GROUPS)),
            spec((CONV_K, 3 * GDN_WIDTH)),
            spec((1, GDN_HEADS)),
            spec((1, GDN_HEADS)),
            spec((GDN_HEADS, 1)),
            spec((GDN_HEADS, 1)),
            spec((1, HEAD_DIM)),
        ],
        out_specs=(
            pl.BlockSpec((tm, GMLP_WIDTH), lambda i: (i, 0)),
            pl.BlockSpec((tm, GDN_WIDTH), lambda i: (i, 0)),
        ),
        scratch_shapes=[
            pltpu.VMEM((CONV_PAD, 3 * GDN_WIDTH), F32),
            pltpu.VMEM((GDN_HEADS, HEAD_DIM, HEAD_DIM), F32),
            pltpu.VMEM((tm, 3 * GDN_WIDTH), BF16),
            pltpu.VMEM((tm, GDN_WIDTH), BF16),
            pltpu.VMEM((tm, 2 * GDN_HEADS), F32),
            pltpu.VMEM((GDN_HEADS, tm), F32),
        ],
        compiler_params=pltpu.CompilerParams(
            dimension_semantics=("arbitrary",), vmem_limit_bytes=VMEM_LIMIT_BYTES),
        name="mix",
    )(h, nw, w_in_t, wab, wabt, lng, lnb, wsp, bspt, cw, alog_c, dtb_c, alog_r, dtb_r, gnw)


def _out_ple_kernel(apply_final_norm,
                    h_ref, ya_ref, yb_ref, p_ref, wout_ref, wple_ref, pnw_ref, gnw_ref, wg_ref, fnw_ref,
                    o_ref):
    h1 = (h_ref[...] + _dot(ya_ref[...], wout_ref[0:GMLP_WIDTH, :])
          + _dot(yb_ref[...], wout_ref[GMLP_WIDTH:GMLP_WIDTH + GDN_WIDTH, :]))
    e = _rms(_dot(_bf(p_ref[...]), wple_ref[...]), pnw_ref[...])
    gate = jax.nn.sigmoid(_dot(_bf(_rms(h1, gnw_ref[...])), wg_ref[...]))
    h2 = h1 + gate * e
    if apply_final_norm:
        h2 = _rms(h2, fnw_ref[...])
    o_ref[...] = h2


def _out_ple(layer, h, ya, yb, p, wout, wple, pnw, gnw, wg, fnw, apply_final_norm):
    t = h.shape[0]
    tm = TILE_OUT
    spec = functools.partial(_layer_spec, layer)
    tok = lambda width: pl.BlockSpec((tm, width), lambda i: (i, 0))
    return pl.pallas_call(
        functools.partial(_out_ple_kernel, apply_final_norm),
        out_shape=jax.ShapeDtypeStruct((t, D_MODEL), F32),
        grid=(t // tm,),
        in_specs=[
            tok(D_MODEL), tok(GMLP_WIDTH), tok(GDN_WIDTH),
            pl.BlockSpec((None, tm, PLE_DIM), lambda i: (layer, i, 0)),
            spec((GMLP_WIDTH + GDN_WIDTH, D_MODEL)),
            spec((PLE_DIM, D_MODEL)),
            spec((1, D_MODEL)),
            spec((1, D_MODEL)),
            spec((D_MODEL, D_MODEL)),
            pl.BlockSpec((1, D_MODEL), lambda i: (0, 0)),
        ],
        out_specs=tok(D_MODEL),
        compiler_params=pltpu.CompilerParams(
            dimension_semantics=("arbitrary",), vmem_limit_bytes=VMEM_LIMIT_BYTES),
        name="out_ple",
    )(h, ya, yb, p, wout, wple, pnw, gnw, wg, fnw)


def kernel(x, p, norm_w, w_in, ln_v_g, ln_v_b, w_spatial, b_spatial, conv_w, A_log, dt_bias,
           gdn_norm_w, w_out, w_ple, ple_norm_w, ple_gate_norm_w, w_ple_gate, final_norm_w):
    batch, seq, d = x.shape
    depth = w_in.shape[0]
    assert d == D_MODEL and seq % TILE_MIX == 0 and (batch * seq) % TILE_OUT == 0
    assert w_in.shape[2] == MAIN_COLS + 2 * GDN_HEADS
    t = batch * seq
    h = x.reshape(t, d)
    p2 = p.reshape(depth, t, PLE_DIM)
    w_in_t = _bf(jnp.swapaxes(w_in, 1, 2))
    wabt = w_in_t[:, MAIN_COLS:MAIN_COLS + GDN_HEADS, :]
    wab = jnp.swapaxes(w_in_t[:, MAIN_COLS:, :], 1, 2)
    wsp_b, wout_b, wple_b, wg_b = _bf(w_spatial), _bf(w_out), _bf(w_ple), _bf(w_ple_gate)
    bspt = jnp.swapaxes(b_spatial, 1, 2)
    row3 = lambda a: a[:, None, :]
    col3 = lambda a: a[:, :, None]
    for i in range(depth):
        ya, yb = _mix(i, h, row3(norm_w), w_in_t, wab, wabt, row3(ln_v_g), row3(ln_v_b), wsp_b, bspt,
                      conv_w, row3(A_log), row3(dt_bias), col3(A_log), col3(dt_bias), row3(gdn_norm_w), seq)
        h = _out_ple(i, h, ya, yb, p2, wout_b, wple_b, row3(ple_norm_w), row3(ple_gate_norm_w), wg_b,
                     final_norm_w[None], apply_final_norm=(i == depth - 1))
    return h.reshape(batch, seq, d)
```

```python
import functools

import jax
import jax.numpy as jnp
from jax import lax
from jax.experimental import pallas as pl
from jax.experimental.pallas import tpu as pltpu

D_MODEL = 1024
GMLP_WIDTH = 1024
GMLP_GROUPS = 8
GROUP_DIM = 128
SPATIAL_BLOCK = 128
SPATIAL_CHUNK = 64
GDN_WIDTH = 1024
GDN_HEADS = 8
HEAD_DIM = 128
CONV_K = 4
PLE_DIM = 256
EPS = 1e-6
MAIN_COLS = 3 * GMLP_WIDTH + 4 * GDN_WIDTH

GDN_CHUNK = 128
TILE_MIX = 512
TILE_OUT = 512
GDN_GROUP_ROWS = 256
STRIP_COLS = 256
BF16_ROWS = 16
CONV_PAD = 8
VMEM_LIMIT_BYTES = 56 * 1024 * 1024

F32 = jnp.float32
BF16 = jnp.bfloat16
LOG2E = 1.4426950408889634
GELU_K1 = -2.0 * 0.7978845608028654 * LOG2E
GELU_K3 = GELU_K1 * 0.044715


def _bf(x):
    return x.astype(BF16)


def _dot(a, b):
    return jnp.dot(a, b, preferred_element_type=F32)


def _dot_nt(a, b):
    return lax.dot_general(a, b, (((1,), (1,)), ((), ())), preferred_element_type=F32)


def _dot_tn(a, b):
    return lax.dot_general(a, b, (((0,), (0,)), ((), ())), preferred_element_type=F32)


def _rms(x, w):
    return x * lax.rsqrt(jnp.mean(x * x, axis=-1, keepdims=True) + EPS) * w


def _silu(x):
    return x / (1.0 + jnp.exp2(x * -LOG2E))


def _gelu_tanh(x):
    return x / (1.0 + jnp.exp2(x * (GELU_K1 + GELU_K3 * (x * x))))


def _softplus(x):
    return jnp.maximum(x, 0.0) + jnp.log1p(jnp.exp(-jnp.abs(x)))


def _pair_dot(dot, lhs1, lhs2, rhs1, rhs2):
    z = jnp.zeros_like(rhs1)
    rhs = jnp.concatenate([jnp.concatenate([rhs1, z], axis=1), jnp.concatenate([z, rhs2], axis=1)], axis=0)
    out = dot(jnp.concatenate([lhs1, lhs2], axis=1), rhs)
    return out[:, 0:HEAD_DIM], out[:, HEAD_DIM:2 * HEAD_DIM]


def _paired(dot, lhs, rhs):
    out = []
    for i in range(0, len(lhs), 2):
        out.extend(_pair_dot(dot, lhs[i], lhs[i + 1], rhs[i], rhs[i + 1]))
    return out


def _head_cols(base, hd):
    return slice(base + hd * HEAD_DIM, base + (hd + 1) * HEAD_DIM)


def _gmlp_branch(xn, w_ref, lng_ref, lnb_ref, wsp_ref, bspt_ref, ya_ref):
    tm = xn.shape[0]
    strips = range(0, GMLP_WIDTH, STRIP_COLS)

    def proj(base, c0):
        return _dot_nt(xn, w_ref[base + c0:base + c0 + STRIP_COLS, :])

    v = jnp.concatenate([_gelu_tanh(proj(GMLP_WIDTH, c0)) for c0 in strips], axis=1)
    mu = jnp.mean(v, axis=-1, keepdims=True)
    vc = v - mu
    var = jnp.mean(vc * vc, axis=-1, keepdims=True)
    vln = _bf(vc * lax.rsqrt(var + EPS) * lng_ref[...] + lnb_ref[...])
    uz = jnp.concatenate([_gelu_tanh(proj(0, c0)) * _silu(proj(2 * GMLP_WIDTH, c0)) for c0 in strips], axis=1)
    ri = lax.broadcasted_iota(jnp.int32, (SPATIAL_BLOCK, SPATIAL_BLOCK), 0) // SPATIAL_CHUNK
    ci = lax.broadcasted_iota(jnp.int32, (SPATIAL_BLOCK, SPATIAL_BLOCK), 1) // SPATIAL_CHUNK
    chunk_causal = ci <= ri
    for g in range(GMLP_GROUPS):
        gs = slice(g * GROUP_DIM, (g + 1) * GROUP_DIM)
        wm = jnp.where(chunk_causal, wsp_ref[g], jnp.zeros_like(wsp_ref[g]))
        bias = bspt_ref[:, g:g + 1]
        for r in range(tm // SPATIAL_BLOCK):
            rs = slice(r * SPATIAL_BLOCK, (r + 1) * SPATIAL_BLOCK)
            mixed = _dot(wm, vln[rs, gs]) + bias
            ya_ref[rs, gs] = (uz[rs, gs] * mixed).astype(ya_ref.dtype)


def _gdn_prepare(xn, w_ref, wab_ref, wabt_ref, cw_ref, alog_c_ref, dtb_c_ref, alog_r_ref, dtb_r_ref,
                 cs_ref, qkv_ref, zb_ref, col_ref, row_ref):
    tm = xn.shape[0]
    sub = lax.broadcasted_iota(jnp.int32, (CONV_PAD, STRIP_COLS), 0)
    for c0 in range(0, 3 * GDN_WIDTH, STRIP_COLS):
        cols = slice(c0, c0 + STRIP_COLS)
        x = _dot_nt(xn, w_ref[3 * GMLP_WIDTH + c0:3 * GMLP_WIDTH + c0 + STRIP_COLS, :])
        prev = cs_ref[:, cols]
        acc = x * cw_ref[CONV_K - 1:CONV_K, cols]
        for shift in range(1, CONV_K):
            xs = pltpu.roll(x, shift, axis=0)
            head = jnp.where(sub < shift, pltpu.roll(prev, shift, axis=0), xs[0:CONV_PAD])
            shifted = jnp.concatenate([head, xs[CONV_PAD:]], axis=0)
            acc = acc + shifted * cw_ref[CONV_K - 1 - shift:CONV_K - shift, cols]
        cs_ref[:, cols] = x[tm - CONV_PAD:tm]
        act = _silu(acc)
        if c0 >= 2 * GDN_WIDTH:
            qkv_ref[:, cols] = _bf(act)
        else:
            scale = HEAD_DIM ** -0.5 if c0 < GDN_WIDTH else 1.0
            for hd in range(STRIP_COLS // HEAD_DIM):
                t = act[:, _head_cols(0, hd)]
                t = t * (lax.rsqrt(jnp.sum(t * t, axis=-1, keepdims=True) + EPS) * scale)
                qkv_ref[:, _head_cols(c0, hd)] = _bf(t)

    for c0 in range(0, GDN_WIDTH, STRIP_COLS):
        zb_ref[:, c0:c0 + STRIP_COLS] = _bf(_silu(_dot_nt(
            xn, w_ref[3 * GMLP_WIDTH + 3 * GDN_WIDTH + c0:3 * GMLP_WIDTH + 3 * GDN_WIDTH + c0 + STRIP_COLS, :])))

    ab = _dot(xn, wab_ref[...])
    g_col = -jnp.exp(alog_c_ref[...]) * _softplus(ab[:, 0:GDN_HEADS] + dtb_c_ref[...])
    beta = jax.nn.sigmoid(ab[:, GDN_HEADS:2 * GDN_HEADS])
    a_row = _dot_nt(wabt_ref[...], xn)
    g_row = -jnp.exp(alog_r_ref[...]) * _softplus(a_row + dtb_r_ref[...])
    ii = lax.broadcasted_iota(jnp.int32, (GDN_CHUNK, GDN_CHUNK), 0)
    jj = lax.broadcasted_iota(jnp.int32, (GDN_CHUNK, GDN_CHUNK), 1)
    tri_lower = (jj <= ii).astype(F32)
    tri_upper = (ii <= jj).astype(F32)
    for r in range(tm // GDN_CHUNK):
        rs = slice(r * GDN_CHUNK, (r + 1) * GDN_CHUNK)
        col_ref[rs, 0:GDN_HEADS] = jnp.dot(tri_lower, g_col[rs, :], precision=lax.Precision.HIGHEST,
                                           preferred_element_type=F32)
        col_ref[rs, GDN_HEADS:2 * GDN_HEADS] = beta[rs, :]
        row_ref[:, rs] = jnp.dot(g_row[:, rs], tri_upper, precision=lax.Precision.HIGHEST,
                                 preferred_element_type=F32)


def _gdn_chunks(qkv_ref, zb_ref, col_ref, row_ref, gnw_ref, state_ref, yb_ref):
    c = GDN_CHUNK
    n_chunks = qkv_ref.shape[0] // c
    ii = lax.broadcasted_iota(jnp.int32, (c, c), 0)
    jj = lax.broadcasted_iota(jnp.int32, (c, c), 1)
    causal = jj <= ii
    strict = jj < ii
    eye = (ii == jj).astype(F32)
    units = [(j, hd) for j in range(n_chunks) for hd in range(GDN_HEADS)]
    n_units = range(len(units))

    def rows(j):
        return slice(j * c, (j + 1) * c)

    dcols, betas, e_dec, e_rem, e_last = [], [], [], [], []
    for j in range(n_chunks):
        col = col_ref[rows(j), :]
        d = col[:, 0:GDN_HEADS]
        dlast = d[c - 1:c, :]
        dcols.append(d)
        betas.append(col[:, GDN_HEADS:2 * GDN_HEADS])
        e_dec.append(jnp.exp(d))
        e_rem.append(jnp.exp(dlast - d))
        e_last.append(jnp.exp(dlast))

    def colv(per_chunk, u):
        j, hd = units[u]
        return per_chunk[j][:, hd:hd + 1]

    q = [qkv_ref[rows(j), _head_cols(0, hd)] for j, hd in units]
    k = [qkv_ref[rows(j), _head_cols(GDN_WIDTH, hd)] for j, hd in units]
    kf = [k[u].astype(F32) for u in n_units]
    kb = [kf[u] * colv(betas, u) for u in n_units]
    gram = _paired(_dot_nt, [jnp.concatenate([_bf(kb[u]), q[u]], axis=0) for u in n_units], k)
    lmat = [jnp.exp(jnp.where(causal, colv(dcols, u) - row_ref[hd:hd + 1, rows(j)], -jnp.inf))
            for u, (j, hd) in enumerate(units)]
    a = [jnp.where(strict, gram[u][0:c] * lmat[u], 0.0) for u in n_units]
    qk = [_bf(gram[u][c:2 * c] * lmat[u]) for u in n_units]
    a_b = [_bf(a[u]) for u in n_units]
    tb = [_bf(eye) - jnp.where((ii // 2) == (jj // 2), a_b[u], jnp.zeros_like(a_b[u])) for u in n_units]
    b = 2
    while b < c:
        off = _bf((((ii // (2 * b)) == (jj // (2 * b))) & ((ii // b) != (jj // b))).astype(F32))
        if b % BF16_ROWS == 0:
            lower = [slice(s0 + b, s0 + 2 * b) for s0 in range(0, c, 2 * b)]
            m = _paired(_dot, [jnp.concatenate([tb[u][rs] for rs in lower], axis=0) for u in n_units],
                        [a_b[u] * off for u in n_units])
            upd = [_bf(x) for x in _paired(_dot, [_bf(x) for x in m], tb)]
            tb = [jnp.concatenate(
                [piece for n, rs in enumerate(lower)
                 for piece in (tb[u][rs.start - b:rs.start], tb[u][rs] - upd[u][n * b:(n + 1) * b])], axis=0)
                 for u in n_units]
        else:
            m = _paired(_dot, tb, [a_b[u] * off for u in n_units])
            upd = _paired(_dot, [_bf(x) for x in m], tb)
            tb = [tb[u] - _bf(upd[u]) for u in n_units]
        b *= 2
    r = [jnp.concatenate([qkv_ref[rows(j), _head_cols(2 * GDN_WIDTH, hd)].astype(F32) * colv(betas, u),
                          kb[u] * colv(e_dec, u)], axis=1) for u, (j, hd) in enumerate(units)]
    uw = [_dot(tb[u], _bf(r[u])) for u in n_units]
    w_b = [_bf(uw[u][:, HEAD_DIM:2 * HEAD_DIM]) for u in n_units]
    q_qk = [jnp.concatenate([_bf(q[u].astype(F32) * colv(e_dec, u)), qk[u]], axis=1) for u in n_units]
    kd = [_bf(kf[u] * colv(e_rem, u)) for u in n_units]
    s = [state_ref[hd] for hd in range(GDN_HEADS)]
    for j in range(n_chunks):
        us = [j * GDN_HEADS + hd for hd in range(GDN_HEADS)]
        s_b = [_bf(s[hd]) for hd in range(GDN_HEADS)]
        w_s = _paired(_dot, [w_b[u] for u in us], s_b)
        v_new = [_bf(uw[u][:, 0:HEAD_DIM] - w_s[hd]) for hd, u in enumerate(us)]
        o = [_dot(q_qk[u], jnp.concatenate([s_b[hd], v_new[hd]], axis=0)) for hd, u in enumerate(us)]
        s = [s[hd] * colv(e_last, u) + _dot_tn(kd[u], v_new[hd]) for hd, u in enumerate(us)]
        for hd in range(GDN_HEADS):
            on = o[hd] * lax.rsqrt(jnp.mean(o[hd] * o[hd], axis=-1, keepdims=True) + EPS) * gnw_ref[...]
            yb_ref[rows(j), _head_cols(0, hd)] = (
                on * zb_ref[rows(j), _head_cols(0, hd)].astype(F32)).astype(yb_ref.dtype)
    for hd in range(GDN_HEADS):
        state_ref[hd] = s[hd]


def _mix_kernel(tiles_per_seq,
                h_ref, nw_ref, w_ref, wab_ref, wabt_ref, lng_ref, lnb_ref, wsp_ref, bspt_ref,
                cw_ref, alog_c_ref, dtb_c_ref, alog_r_ref, dtb_r_ref, gnw_ref,
                ya_ref, yb_ref,
                cs_ref, state_ref, qkv_ref, zb_ref, col_ref, row_ref):
    @pl.when(pl.program_id(0) % tiles_per_seq == 0)
    def _():
        cs_ref[...] = jnp.zeros(cs_ref.shape, F32)
        state_ref[...] = jnp.zeros(state_ref.shape, F32)

    xn = _bf(_rms(h_ref[...], nw_ref[...]))
    _gmlp_branch(xn, w_ref, lng_ref, lnb_ref, wsp_ref, bspt_ref, ya_ref)
    _gdn_prepare(xn, w_ref, wab_ref, wabt_ref, cw_ref, alog_c_ref, dtb_c_ref, alog_r_ref, dtb_r_ref,
                 cs_ref, qkv_ref, zb_ref, col_ref, row_ref)
    for r0 in range(0, h_ref.shape[0], GDN_GROUP_ROWS):
        rs = slice(r0, r0 + GDN_GROUP_ROWS)
        _gdn_chunks(qkv_ref.at[rs], zb_ref.at[rs], col_ref.at[rs], row_ref.at[:, rs], gnw_ref, state_ref,
                    yb_ref.at[rs])


def _layer_spec(layer, shape):
    return pl.BlockSpec((None,) + shape, lambda i: (layer,) + (0,) * len(shape),
                        pipeline_mode=pl.Buffered(1))


def _mix(layer, h, nw, w_in_t, wab, wabt, lng, lnb, wsp, bspt, cw, alog_c, dtb_c, alog_r, dtb_r, gnw, seq):
    t = h.shape[0]
    tm = TILE_MIX
    spec = functools.partial(_layer_spec, layer)
    return pl.pallas_call(
        functools.partial(_mix_kernel, seq // tm),
        out_shape=(
            jax.ShapeDtypeStruct((t, GMLP_WIDTH), BF16),
            jax.ShapeDtypeStruct((t, GDN_WIDTH), BF16),
        ),
        grid=(t // tm,),
        in_specs=[
            pl.BlockSpec((tm, D_MODEL), lambda i: (i, 0)),
            spec((1, D_MODEL)),
            spec((MAIN_COLS, D_MODEL)),
            spec((D_MODEL, 2 * GDN_HEADS)),
            spec((GDN_HEADS, D_MODEL)),
            spec((1, GMLP_WIDTH)),
            spec((1, GMLP_WIDTH)),
            spec((GMLP_GROUPS, SPATIAL_BLOCK, SPATIAL_BLOCK)),
            spec((SPATIAL_BLOCK, GMLP_GROUPS)),
            spec((CONV_K, 3 * GDN_WIDTH)),
            spec((1, GDN_HEADS)),
            spec((1, GDN_HEADS)),
            spec((GDN_HEADS, 1)),
            spec((GDN_HEADS, 1)),
            spec((1, HEAD_DIM)),
        ],
        out_specs=(
            pl.BlockSpec((tm, GMLP_WIDTH), lambda i: (i, 0)),
            pl.BlockSpec((tm, GDN_WIDTH), lambda i: (i, 0)),
        ),
        scratch_shapes=[
            pltpu.VMEM((CONV_PAD, 3 * GDN_WIDTH), F32),
            pltpu.VMEM((GDN_HEADS, HEAD_DIM, HEAD_DIM), F32),
            pltpu.VMEM((tm, 3 * GDN_WIDTH), BF16),
            pltpu.VMEM((tm, GDN_WIDTH), BF16),
            pltpu.VMEM((tm, 2 * GDN_HEADS), F32),
            pltpu.VMEM((GDN_HEADS, tm), F32),
        ],
        compiler_params=pltpu.CompilerParams(
            dimension_semantics=("arbitrary",), vmem_limit_bytes=VMEM_LIMIT_BYTES),
        name="mix",
    )(h, nw, w_in_t, wab, wabt, lng, lnb, wsp, bspt, cw, alog_c, dtb_c, alog_r, dtb_r, gnw)


def _out_ple_kernel(apply_final_norm,
                    h_ref, ya_ref, yb_ref, p_ref, wout_ref, wple_ref, pnw_ref, gnw_ref, wg_ref, fnw_ref,
                    o_ref):
    h1 = (h_ref[...] + _dot(ya_ref[...], wout_ref[0:GMLP_WIDTH, :])
          + _dot(yb_ref[...], wout_ref[GMLP_WIDTH:GMLP_WIDTH + GDN_WIDTH, :]))
    e = _rms(_dot(_bf(p_ref[...]), wple_ref[...]), pnw_ref[...])
    gate = jax.nn.sigmoid(_dot(_bf(_rms(h1, gnw_ref[...])), wg_ref[...]))
    h2 = h1 + gate * e
    if apply_final_norm:
        h2 = _rms(h2, fnw_ref[...])
    o_ref[...] = h2


def _out_ple(layer, h, ya, yb, p, wout, wple, pnw, gnw, wg, fnw, apply_final_norm):
    t = h.shape[0]
    tm = TILE_OUT
    spec = functools.partial(_layer_spec, layer)
    tok = lambda width: pl.BlockSpec((tm, width), lambda i: (i, 0))
    return pl.pallas_call(
        functools.partial(_out_ple_kernel, apply_final_norm),
        out_shape=jax.ShapeDtypeStruct((t, D_MODEL), F32),
        grid=(t // tm,),
        in_specs=[
            tok(D_MODEL), tok(GMLP_WIDTH), tok(GDN_WIDTH),
            pl.BlockSpec((None, tm, PLE_DIM), lambda i: (layer, i, 0)),
            spec((GMLP_WIDTH + GDN_WIDTH, D_MODEL)),
            spec((PLE_DIM, D_MODEL)),
            spec((1, D_MODEL)),
            spec((1, D_MODEL)),
            spec((D_MODEL, D_MODEL)),
            pl.BlockSpec((1, D_MODEL), lambda i: (0, 0)),
        ],
        out_specs=tok(D_MODEL),
        compiler_params=pltpu.CompilerParams(
            dimension_semantics=("arbitrary",), vmem_limit_bytes=VMEM_LIMIT_BYTES),
        name="out_ple",
    )(h, ya, yb, p, wout, wple, pnw, gnw, wg, fnw)


def kernel(x, p, norm_w, w_in, ln_v_g, ln_v_b, w_spatial, b_spatial, conv_w, A_log, dt_bias,
           gdn_norm_w, w_out, w_ple, ple_norm_w, ple_gate_norm_w, w_ple_gate, final_norm_w):
    batch, seq, d = x.shape
    depth = w_in.shape[0]
    assert d == D_MODEL and seq % TILE_MIX == 0 and (batch * seq) % TILE_OUT == 0
    assert w_in.shape[2] == MAIN_COLS + 2 * GDN_HEADS
    t = batch * seq
    h = x.reshape(t, d)
    p2 = p.reshape(depth, t, PLE_DIM)
    w_in_t = _bf(jnp.swapaxes(w_in, 1, 2))
    wabt = w_in_t[:, MAIN_COLS:MAIN_COLS + GDN_HEADS, :]
    wab = jnp.swapaxes(w_in_t[:, MAIN_COLS:, :], 1, 2)
    wsp_b, wout_b, wple_b, wg_b = _bf(w_spatial), _bf(w_out), _bf(w_ple), _bf(w_ple_gate)
    bspt = jnp.swapaxes(b_spatial, 1, 2)
    row3 = lambda a: a[:, None, :]
    col3 = lambda a: a[:, :, None]
    for i in range(depth):
        ya, yb = _mix(i, h, row3(norm_w), w_in_t, wab, wabt, row3(ln_v_g), row3(ln_v_b), wsp_b, bspt,
                      conv_w, row3(A_log), row3(dt_bias), col3(A_log), col3(dt_bias), row3(gdn_norm_w), seq)
        h = _out_ple(i, h, ya, yb, p2, wout_b, wple_b, row3(ple_norm_w), row3(ple_gate_norm_w), wg_b,
                     final_norm_w[None], apply_final_norm=(i == depth - 1))
    return h.reshape(batch, seq, d)
```

```python
import functools

import jax
import jax.numpy as jnp
from jax import lax
from jax.experimental import pallas as pl
from jax.experimental.pallas import tpu as pltpu

D_MODEL = 1024
GMLP_WIDTH = 1024
GMLP_GROUPS = 8
GROUP_DIM = 128
SPATIAL_BLOCK = 128
SPATIAL_CHUNK = 64
GDN_WIDTH = 1024
GDN_HEADS = 8
HEAD_DIM = 128
CONV_K = 4
PLE_DIM = 256
EPS = 1e-6
MAIN_COLS = 3 * GMLP_WIDTH + 4 * GDN_WIDTH

GDN_CHUNK = 128
TILE_MIX = 512
TILE_OUT = 512
GDN_GROUP_ROWS = 256
STRIP_COLS = 256
BF16_ROWS = 16
CONV_PAD = 8
VMEM_LIMIT_BYTES = 56 * 1024 * 1024

F32 = jnp.float32
BF16 = jnp.bfloat16
LOG2E = 1.4426950408889634
GELU_K1 = -2.0 * 0.7978845608028654 * LOG2E
GELU_K3 = GELU_K1 * 0.044715


def _bf(x):
    return x.astype(BF16)


def _dot(a, b):
    return jnp.dot(a, b, preferred_element_type=F32)


def _dot_nt(a, b):
    return lax.dot_general(a, b, (((1,), (1,)), ((), ())), preferred_element_type=F32)


def _dot_tn(a, b):
    return lax.dot_general(a, b, (((0,), (0,)), ((), ())), preferred_element_type=F32)


def _rms(x, w):
    return x * lax.rsqrt(jnp.mean(x * x, axis=-1, keepdims=True) + EPS) * w


def _silu(x):
    return x / (1.0 + jnp.exp2(x * -LOG2E))


def _gelu_tanh(x):
    return x / (1.0 + jnp.exp2(x * (GELU_K1 + GELU_K3 * (x * x))))


def _softplus(x):
    return jnp.maximum(x, 0.0) + jnp.log1p(jnp.exp(-jnp.abs(x)))


def _head_cols(base, hd):
    return slice(base + hd * HEAD_DIM, base + (hd + 1) * HEAD_DIM)


def _gmlp_branch(xn, w_ref, lng_ref, lnb_ref, wsp_ref, bspt_ref, ya_ref):
    tm = xn.shape[0]
    strips = range(0, GMLP_WIDTH, STRIP_COLS)

    def proj(base, c0):
        return _dot_nt(xn, w_ref[base + c0:base + c0 + STRIP_COLS, :])

    v = jnp.concatenate([_gelu_tanh(proj(GMLP_WIDTH, c0)) for c0 in strips], axis=1)
    mu = jnp.mean(v, axis=-1, keepdims=True)
    vc = v - mu
    var = jnp.mean(vc * vc, axis=-1, keepdims=True)
    vln = _bf(vc * lax.rsqrt(var + EPS) * lng_ref[...] + lnb_ref[...])
    uz = jnp.concatenate([_gelu_tanh(proj(0, c0)) * _silu(proj(2 * GMLP_WIDTH, c0)) for c0 in strips], axis=1)
    ri = lax.broadcasted_iota(jnp.int32, (SPATIAL_BLOCK, SPATIAL_BLOCK), 0) // SPATIAL_CHUNK
    ci = lax.broadcasted_iota(jnp.int32, (SPATIAL_BLOCK, SPATIAL_BLOCK), 1) // SPATIAL_CHUNK
    chunk_causal = ci <= ri
    for g in range(GMLP_GROUPS):
        gs = slice(g * GROUP_DIM, (g + 1) * GROUP_DIM)
        wm = jnp.where(chunk_causal, wsp_ref[g], jnp.zeros_like(wsp_ref[g]))
        bias = bspt_ref[:, g:g + 1]
        for r in range(tm // SPATIAL_BLOCK):
            rs = slice(r * SPATIAL_BLOCK, (r + 1) * SPATIAL_BLOCK)
            mixed = _dot(wm, vln[rs, gs]) + bias
            ya_ref[rs, gs] = (uz[rs, gs] * mixed).astype(ya_ref.dtype)


def _gdn_prepare(xn, w_ref, wab_ref, wabt_ref, cw_ref, alog_c_ref, dtb_c_ref, alog_r_ref, dtb_r_ref,
                 cs_ref, qkv_ref, zb_ref, col_ref, row_ref):
    tm = xn.shape[0]
    sub = lax.broadcasted_iota(jnp.int32, (CONV_PAD, STRIP_COLS), 0)
    for c0 in range(0, 3 * GDN_WIDTH, STRIP_COLS):
        cols = slice(c0, c0 + STRIP_COLS)
        x = _dot_nt(xn, w_ref[3 * GMLP_WIDTH + c0:3 * GMLP_WIDTH + c0 + STRIP_COLS, :])
        prev = cs_ref[:, cols]
        acc = x * cw_ref[CONV_K - 1:CONV_K, cols]
        for shift in range(1, CONV_K):
            xs = pltpu.roll(x, shift, axis=0)
            head = jnp.where(sub < shift, pltpu.roll(prev, shift, axis=0), xs[0:CONV_PAD])
            shifted = jnp.concatenate([head, xs[CONV_PAD:]], axis=0)
            acc = acc + shifted * cw_ref[CONV_K - 1 - shift:CONV_K - shift, cols]
        cs_ref[:, cols] = x[tm - CONV_PAD:tm]
        act = _silu(acc)
        if c0 >= 2 * GDN_WIDTH:
            qkv_ref[:, cols] = _bf(act)
        else:
            scale = HEAD_DIM ** -0.5 if c0 < GDN_WIDTH else 1.0
            for hd in range(STRIP_COLS // HEAD_DIM):
                t = act[:, _head_cols(0, hd)]
                t = t * (lax.rsqrt(jnp.sum(t * t, axis=-1, keepdims=True) + EPS) * scale)
                qkv_ref[:, _head_cols(c0, hd)] = _bf(t)

    for c0 in range(0, GDN_WIDTH, STRIP_COLS):
        zb_ref[:, c0:c0 + STRIP_COLS] = _bf(_silu(_dot_nt(
            xn, w_ref[3 * GMLP_WIDTH + 3 * GDN_WIDTH + c0:3 * GMLP_WIDTH + 3 * GDN_WIDTH + c0 + STRIP_COLS, :])))

    ab = _dot(xn, wab_ref[...])
    g_col = -jnp.exp(alog_c_ref[...]) * _softplus(ab[:, 0:GDN_HEADS] + dtb_c_ref[...])
    beta = jax.nn.sigmoid(ab[:, GDN_HEADS:2 * GDN_HEADS])
    a_row = _dot_nt(wabt_ref[...], xn)
    g_row = -jnp.exp(alog_r_ref[...]) * _softplus(a_row + dtb_r_ref[...])
    ii = lax.broadcasted_iota(jnp.int32, (GDN_CHUNK, GDN_CHUNK), 0)
    jj = lax.broadcasted_iota(jnp.int32, (GDN_CHUNK, GDN_CHUNK), 1)
    tri_lower = (jj <= ii).astype(F32)
    tri_upper = (ii <= jj).astype(F32)
    for r in range(tm // GDN_CHUNK):
        rs = slice(r * GDN_CHUNK, (r + 1) * GDN_CHUNK)
        col_ref[rs, 0:GDN_HEADS] = jnp.dot(tri_lower, g_col[rs, :], precision=lax.Precision.HIGHEST,
                                           preferred_element_type=F32)
        col_ref[rs, GDN_HEADS:2 * GDN_HEADS] = beta[rs, :]
        row_ref[:, rs] = jnp.dot(g_row[:, rs], tri_upper, precision=lax.Precision.HIGHEST,
                                 preferred_element_type=F32)


def _gdn_chunks(qkv_ref, zb_ref, col_ref, row_ref, gnw_ref, state_ref, yb_ref):
    c = GDN_CHUNK
    n_chunks = qkv_ref.shape[0] // c
    ii = lax.broadcasted_iota(jnp.int32, (c, c), 0)
    jj = lax.broadcasted_iota(jnp.int32, (c, c), 1)
    causal = jj <= ii
    strict = jj < ii
    eye = (ii == jj).astype(F32)
    units = [(j, hd) for j in range(n_chunks) for hd in range(GDN_HEADS)]
    n_units = range(len(units))

    def rows(j):
        return slice(j * c, (j + 1) * c)

    dcols, betas, e_dec, e_rem, e_last = [], [], [], [], []
    for j in range(n_chunks):
        col = col_ref[rows(j), :]
        d = col[:, 0:GDN_HEADS]
        dlast = d[c - 1:c, :]
        dcols.append(d)
        betas.append(col[:, GDN_HEADS:2 * GDN_HEADS])
        e_dec.append(jnp.exp(d))
        e_rem.append(jnp.exp(dlast - d))
        e_last.append(jnp.exp(dlast))

    def colv(per_chunk, u):
        j, hd = units[u]
        return per_chunk[j][:, hd:hd + 1]

    q, k, kd, lhs_gram, r_b, qd_b = [], [], [], [], [], []
    for u, (j, hd) in enumerate(units):
        beta_b = jnp.broadcast_to(colv(betas, u), (c, HEAD_DIM))
        edec_b = jnp.broadcast_to(colv(e_dec, u), (c, HEAD_DIM))
        q_u = qkv_ref[rows(j), _head_cols(0, hd)]
        k_u = qkv_ref[rows(j), _head_cols(GDN_WIDTH, hd)]
        kf = k_u.astype(F32)
        kb = kf * beta_b
        q.append(q_u)
        k.append(k_u)
        kd.append(_bf(kf * colv(e_rem, u)))
        lhs_gram.append(jnp.concatenate([_bf(kb), q_u], axis=0))
        r_b.append(_bf(jnp.concatenate(
            [qkv_ref[rows(j), _head_cols(2 * GDN_WIDTH, hd)].astype(F32) * beta_b, kb * edec_b], axis=1)))
        qd_b.append(_bf(q_u.astype(F32) * edec_b))
    gram = [_dot_nt(lhs_gram[u], k[u]) for u in n_units]
    lmat = [jnp.exp(jnp.where(causal, colv(dcols, u) - row_ref[hd:hd + 1, rows(j)], -jnp.inf))
            for u, (j, hd) in enumerate(units)]
    a = [jnp.where(strict, gram[u][0:c] * lmat[u], 0.0) for u in n_units]
    qk = [_bf(gram[u][c:2 * c] * lmat[u]) for u in n_units]
    a_b = [_bf(a[u]) for u in n_units]
    tb = [_bf(eye) - jnp.where((ii // 2) == (jj // 2), a_b[u], jnp.zeros_like(a_b[u])) for u in n_units]
    b = 2
    while b < c:
        off = _bf((((ii // (2 * b)) == (jj // (2 * b))) & ((ii // b) != (jj // b))).astype(F32))
        if b % BF16_ROWS == 0:
            lower = [slice(s0 + b, s0 + 2 * b) for s0 in range(0, c, 2 * b)]
            m = [_dot(jnp.concatenate([tb[u][rs] for rs in lower], axis=0), a_b[u] * off) for u in n_units]
            upd = [_bf(_dot(_bf(m[u]), tb[u])) for u in n_units]
            tb = [jnp.concatenate(
                [piece for n, rs in enumerate(lower)
                 for piece in (tb[u][rs.start - b:rs.start], tb[u][rs] - upd[u][n * b:(n + 1) * b])], axis=0)
                 for u in n_units]
        else:
            m = [_dot(tb[u], a_b[u] * off) for u in n_units]
            tb = [tb[u] - _bf(_dot(_bf(m[u]), tb[u])) for u in n_units]
        b *= 2
    uw = [_dot(tb[u], r_b[u]) for u in n_units]
    w_b = [_bf(uw[u][:, HEAD_DIM:2 * HEAD_DIM]) for u in n_units]
    q_qk = [jnp.concatenate([qd_b[u], qk[u]], axis=1) for u in n_units]
    s = [state_ref[hd] for hd in range(GDN_HEADS)]
    for j in range(n_chunks):
        us = [j * GDN_HEADS + hd for hd in range(GDN_HEADS)]
        s_b = [_bf(s[hd]) for hd in range(GDN_HEADS)]
        v_new = [_bf(uw[u][:, 0:HEAD_DIM] - _dot(w_b[u], s_b[hd])) for hd, u in enumerate(us)]
        o = [_dot(q_qk[u], jnp.concatenate([s_b[hd], v_new[hd]], axis=0)) for hd, u in enumerate(us)]
        s = [s[hd] * colv(e_last, u) + _dot_tn(kd[u], v_new[hd]) for hd, u in enumerate(us)]
        for hd in range(GDN_HEADS):
            on = o[hd] * lax.rsqrt(jnp.mean(o[hd] * o[hd], axis=-1, keepdims=True) + EPS) * gnw_ref[...]
            yb_ref[rows(j), _head_cols(0, hd)] = (
                on * zb_ref[rows(j), _head_cols(0, hd)].astype(F32)).astype(yb_ref.dtype)
    for hd in range(GDN_HEADS):
        state_ref[hd] = s[hd]


def _mix_kernel(tiles_per_seq,
                h_ref, nw_ref, w_ref, wab_ref, wabt_ref, lng_ref, lnb_ref, wsp_ref, bspt_ref,
                cw_ref, alog_c_ref, dtb_c_ref, alog_r_ref, dtb_r_ref, gnw_ref,
                ya_ref, yb_ref,
                cs_ref, state_ref, qkv_ref, zb_ref, col_ref, row_ref):
    @pl.when(pl.program_id(0) % tiles_per_seq == 0)
    def _():
        cs_ref[...] = jnp.zeros(cs_ref.shape, F32)
        state_ref[...] = jnp.zeros(state_ref.shape, F32)

    xn = _bf(_rms(h_ref[...], nw_ref[...]))
    _gmlp_branch(xn, w_ref, lng_ref, lnb_ref, wsp_ref, bspt_ref, ya_ref)
    _gdn_prepare(xn, w_ref, wab_ref, wabt_ref, cw_ref, alog_c_ref, dtb_c_ref, alog_r_ref, dtb_r_ref,
                 cs_ref, qkv_ref, zb_ref, col_ref, row_ref)
    for r0 in range(0, h_ref.shape[0], GDN_GROUP_ROWS):
        rs = slice(r0, r0 + GDN_GROUP_ROWS)
        _gdn_chunks(qkv_ref.at[rs], zb_ref.at[rs], col_ref.at[rs], row_ref.at[:, rs], gnw_ref, state_ref,
                    yb_ref.at[rs])


def _layer_spec(layer, shape):
    return pl.BlockSpec((None,) + shape, lambda i: (layer,) + (0,) * len(shape),
                        pipeline_mode=pl.Buffered(1))


def _mix(layer, h, nw, w_in_t, wab, wabt, lng, lnb, wsp, bspt, cw, alog_c, dtb_c, alog_r, dtb_r, gnw, seq):
    t = h.shape[0]
    tm = TILE_MIX
    spec = functools.partial(_layer_spec, layer)
    return pl.pallas_call(
        functools.partial(_mix_kernel, seq // tm),
        out_shape=(
            jax.ShapeDtypeStruct((t, GMLP_WIDTH), BF16),
            jax.ShapeDtypeStruct((t, GDN_WIDTH), BF16),
        ),
        grid=(t // tm,),
        in_specs=[
            pl.BlockSpec((tm, D_MODEL), lambda i: (i, 0)),
            spec((1, D_MODEL)),
            spec((MAIN_COLS, D_MODEL)),
            spec((D_MODEL, 2 * GDN_HEADS)),
            spec((GDN_HEADS, D_MODEL)),
            spec((1, GMLP_WIDTH)),
            spec((1, GMLP_WIDTH)),
            spec((GMLP_GROUPS, SPATIAL_BLOCK, SPATIAL_BLOCK)),
            spec((SPATIAL_BLOCK, GMLP_GROUPS)),
            spec((CONV_K, 3 * GDN_WIDTH)),
            spec((1, GDN_HEADS)),
            spec((1, GDN_HEADS)),
            spec((GDN_HEADS, 1)),
            spec((GDN_HEADS, 1)),
            spec((1, HEAD_DIM)),
        ],
        out_specs=(
            pl.BlockSpec((tm, GMLP_WIDTH), lambda i: (i, 0)),
            pl.BlockSpec((tm, GDN_WIDTH), lambda i: (i, 0)),
        ),
        scratch_shapes=[
            pltpu.VMEM((CONV_PAD, 3 * GDN_WIDTH), F32),
            pltpu.VMEM((GDN_HEADS, HEAD_DIM, HEAD_DIM), F32),
            pltpu.VMEM((tm, 3 * GDN_WIDTH), BF16),
            pltpu.VMEM((tm, GDN_WIDTH), BF16),
            pltpu.VMEM((tm, 2 * GDN_HEADS), F32),
            pltpu.VMEM((GDN_HEADS, tm), F32),
        ],
        compiler_params=pltpu.CompilerParams(
            dimension_semantics=("arbitrary",), vmem_limit_bytes=VMEM_LIMIT_BYTES),
        name="mix",
    )(h, nw, w_in_t, wab, wabt, lng, lnb, wsp, bspt, cw, alog_c, dtb_c, alog_r, dtb_r, gnw)


def _out_ple_kernel(apply_final_norm,
                    h_ref, ya_ref, yb_ref, p_ref, wout_ref, wple_ref, pnw_ref, gnw_ref, wg_ref, fnw_ref,
                    o_ref):
    h1 = (h_ref[...] + _dot(ya_ref[...], wout_ref[0:GMLP_WIDTH, :])
          + _dot(yb_ref[...], wout_ref[GMLP_WIDTH:GMLP_WIDTH + GDN_WIDTH, :]))
    e = _rms(_dot(_bf(p_ref[...]), wple_ref[...]), pnw_ref[...])
    gate = jax.nn.sigmoid(_dot(_bf(_rms(h1, gnw_ref[...])), wg_ref[...]))
    h2 = h1 + gate * e
    if apply_final_norm:
        h2 = _rms(h2, fnw_ref[...])
    o_ref[...] = h2


def _out_ple(layer, h, ya, yb, p, wout, wple, pnw, gnw, wg, fnw, apply_final_norm):
    t = h.shape[0]
    tm = TILE_OUT
    spec = functools.partial(_layer_spec, layer)
    tok = lambda width: pl.BlockSpec((tm, width), lambda i: (i, 0))
    return pl.pallas_call(
        functools.partial(_out_ple_kernel, apply_final_norm),
        out_shape=jax.ShapeDtypeStruct((t, D_MODEL), F32),
        grid=(t // tm,),
        in_specs=[
            tok(D_MODEL), tok(GMLP_WIDTH), tok(GDN_WIDTH),
            pl.BlockSpec((None, tm, PLE_DIM), lambda i: (layer, i, 0)),
            spec((GMLP_WIDTH + GDN_WIDTH, D_MODEL)),
            spec((PLE_DIM, D_MODEL)),
            spec((1, D_MODEL)),
            spec((1, D_MODEL)),
            spec((D_MODEL, D_MODEL)),
            pl.BlockSpec((1, D_MODEL), lambda i: (0, 0)),
        ],
        out_specs=tok(D_MODEL),
        compiler_params=pltpu.CompilerParams(
            dimension_semantics=("arbitrary",), vmem_limit_bytes=VMEM_LIMIT_BYTES),
        name="out_ple",
    )(h, ya, yb, p, wout, wple, pnw, gnw, wg, fnw)


def kernel(x, p, norm_w, w_in, ln_v_g, ln_v_b, w_spatial, b_spatial, conv_w, A_log, dt_bias,
           gdn_norm_w, w_out, w_ple, ple_norm_w, ple_gate_norm_w, w_ple_gate, final_norm_w):
    batch, seq, d = x.shape
    depth = w_in.shape[0]
    assert d == D_MODEL and seq % TILE_MIX == 0 and (batch * seq) % TILE_OUT == 0
    assert w_in.shape[2] == MAIN_COLS + 2 * GDN_HEADS
    t = batch * seq
    h = x.reshape(t, d)
    p2 = p.reshape(depth, t, PLE_DIM)
    w_in_t = _bf(jnp.swapaxes(w_in, 1, 2))
    wabt = w_in_t[:, MAIN_COLS:MAIN_COLS + GDN_HEADS, :]
    wab = jnp.swapaxes(w_in_t[:, MAIN_COLS:, :], 1, 2)
    wsp_b, wout_b, wple_b, wg_b = _bf(w_spatial), _bf(w_out), _bf(w_ple), _bf(w_ple_gate)
    bspt = jnp.swapaxes(b_spatial, 1, 2)
    row3 = lambda a: a[:, None, :]
    col3 = lambda a: a[:, :, None]
    for i in range(depth):
        ya, yb = _mix(i, h, row3(norm_w), w_in_t, wab, wabt, row3(ln_v_g), row3(ln_v_b), wsp_b, bspt,
                      conv_w, row3(A_log), row3(dt_bias), col3(A_log), col3(dt_bias), row3(gdn_norm_w), seq)
        h = _out_ple(i, h, ya, yb, p2, wout_b, wple_b, row3(ple_norm_w), row3(ple_gate_norm_w), wg_b,
                     final_norm_w[None], apply_final_norm=(i == depth - 1))
    return h.reshape(batch, seq, d)
```

```python
import functools

import jax
import jax.numpy as jnp
from jax import lax
from jax.experimental import pallas as pl
from jax.experimental.pallas import tpu as pltpu

D_MODEL = 1024
GMLP_WIDTH = 1024
GMLP_GROUPS = 8
GROUP_DIM = 128
SPATIAL_BLOCK = 128
SPATIAL_CHUNK = 64
GDN_WIDTH = 1024
GDN_HEADS = 8
HEAD_DIM = 128
CONV_K = 4
PLE_DIM = 256
EPS = 1e-6
MAIN_COLS = 3 * GMLP_WIDTH + 4 * GDN_WIDTH

GDN_CHUNK = 128
TILE_MIX = 512
TILE_OUT = 256
GDN_GROUP_ROWS = 256
STRIP_COLS = 256
BF16_ROWS = 16
CONV_PAD = 8
VMEM_LIMIT_BYTES = 56 * 1024 * 1024

F32 = jnp.float32
BF16 = jnp.bfloat16
LOG2E = 1.4426950408889634
GELU_K1 = -2.0 * 0.7978845608028654 * LOG2E
GELU_K3 = GELU_K1 * 0.044715


def _bf(x):
    return x.astype(BF16)


def _dot(a, b):
    return jnp.dot(a, b, preferred_element_type=F32)


def _dot_nt(a, b):
    return lax.dot_general(a, b, (((1,), (1,)), ((), ())), preferred_element_type=F32)


def _dot_tn(a, b):
    return lax.dot_general(a, b, (((0,), (0,)), ((), ())), preferred_element_type=F32)


def _rms(x, w):
    return x * lax.rsqrt(jnp.mean(x * x, axis=-1, keepdims=True) + EPS) * w


def _silu(x):
    return x / (1.0 + jnp.exp2(x * -LOG2E))


def _gelu_tanh(x):
    return x / (1.0 + jnp.exp2(x * (GELU_K1 + GELU_K3 * (x * x))))


def _softplus(x):
    return jnp.maximum(x, 0.0) + jnp.log1p(jnp.exp(-jnp.abs(x)))


def _head_cols(base, hd):
    return slice(base + hd * HEAD_DIM, base + (hd + 1) * HEAD_DIM)


def _gmlp_branch(xn, w_ref, lng_ref, lnb_ref, wsp_ref, bspt_ref, ya_ref):
    tm = xn.shape[0]
    strips = range(0, GMLP_WIDTH, STRIP_COLS)

    def proj(base, c0):
        return _dot_nt(xn, w_ref[base + c0:base + c0 + STRIP_COLS, :])

    v = jnp.concatenate([_gelu_tanh(proj(GMLP_WIDTH, c0)) for c0 in strips], axis=1)
    mu = jnp.mean(v, axis=-1, keepdims=True)
    vc = v - mu
    var = jnp.mean(vc * vc, axis=-1, keepdims=True)
    vln = _bf(vc * lax.rsqrt(var + EPS) * lng_ref[...] + lnb_ref[...])
    uz = jnp.concatenate([_gelu_tanh(proj(0, c0)) * _silu(proj(2 * GMLP_WIDTH, c0)) for c0 in strips], axis=1)
    ri = lax.broadcasted_iota(jnp.int32, (SPATIAL_BLOCK, SPATIAL_BLOCK), 0) // SPATIAL_CHUNK
    ci = lax.broadcasted_iota(jnp.int32, (SPATIAL_BLOCK, SPATIAL_BLOCK), 1) // SPATIAL_CHUNK
    chunk_causal = ci <= ri
    for g in range(GMLP_GROUPS):
        gs = slice(g * GROUP_DIM, (g + 1) * GROUP_DIM)
        wm = jnp.where(chunk_causal, wsp_ref[g], jnp.zeros_like(wsp_ref[g]))
        bias = bspt_ref[:, g:g + 1]
        for r in range(tm // SPATIAL_BLOCK):
            rs = slice(r * SPATIAL_BLOCK, (r + 1) * SPATIAL_BLOCK)
            mixed = _dot(wm, vln[rs, gs]) + bias
            ya_ref[rs, gs] = (uz[rs, gs] * mixed).astype(ya_ref.dtype)


def _gdn_prepare(xn, w_ref, wab_ref, wabt_ref, cw_ref, alog_c_ref, dtb_c_ref, alog_r_ref, dtb_r_ref,
                 cs_ref, qkv_ref, zb_ref, col_ref, row_ref):
    tm = xn.shape[0]
    sub = lax.broadcasted_iota(jnp.int32, (CONV_PAD, STRIP_COLS), 0)
    for c0 in range(0, 3 * GDN_WIDTH, STRIP_COLS):
        cols = slice(c0, c0 + STRIP_COLS)
        x = _dot_nt(xn, w_ref[3 * GMLP_WIDTH + c0:3 * GMLP_WIDTH + c0 + STRIP_COLS, :])
        prev = cs_ref[:, cols]
        acc = x * cw_ref[CONV_K - 1:CONV_K, cols]
        for shift in range(1, CONV_K):
            xs = pltpu.roll(x, shift, axis=0)
            head = jnp.where(sub < shift, pltpu.roll(prev, shift, axis=0), xs[0:CONV_PAD])
            shifted = jnp.concatenate([head, xs[CONV_PAD:]], axis=0)
            acc = acc + shifted * cw_ref[CONV_K - 1 - shift:CONV_K - shift, cols]
        cs_ref[:, cols] = x[tm - CONV_PAD:tm]
        act = _silu(acc)
        if c0 >= 2 * GDN_WIDTH:
            qkv_ref[:, cols] = _bf(act)
        else:
            scale = HEAD_DIM ** -0.5 if c0 < GDN_WIDTH else 1.0
            for hd in range(STRIP_COLS // HEAD_DIM):
                t = act[:, _head_cols(0, hd)]
                t = t * (lax.rsqrt(jnp.sum(t * t, axis=-1, keepdims=True) + EPS) * scale)
                qkv_ref[:, _head_cols(c0, hd)] = _bf(t)

    for c0 in range(0, GDN_WIDTH, STRIP_COLS):
        zb_ref[:, c0:c0 + STRIP_COLS] = _bf(_silu(_dot_nt(
            xn, w_ref[3 * GMLP_WIDTH + 3 * GDN_WIDTH + c0:3 * GMLP_WIDTH + 3 * GDN_WIDTH + c0 + STRIP_COLS, :])))

    ab = _dot(xn, wab_ref[...])
    g_col = -jnp.exp(alog_c_ref[...]) * _softplus(ab[:, 0:GDN_HEADS] + dtb_c_ref[...])
    beta = jax.nn.sigmoid(ab[:, GDN_HEADS:2 * GDN_HEADS])
    a_row = _dot_nt(wabt_ref[...], xn)
    g_row = -jnp.exp(alog_r_ref[...]) * _softplus(a_row + dtb_r_ref[...])
    ii = lax.broadcasted_iota(jnp.int32, (GDN_CHUNK, GDN_CHUNK), 0)
    jj = lax.broadcasted_iota(jnp.int32, (GDN_CHUNK, GDN_CHUNK), 1)
    tri_lower = (jj <= ii).astype(F32)
    tri_upper = (ii <= jj).astype(F32)
    for r in range(tm // GDN_CHUNK):
        rs = slice(r * GDN_CHUNK, (r + 1) * GDN_CHUNK)
        col_ref[rs, 0:GDN_HEADS] = jnp.dot(tri_lower, g_col[rs, :], precision=lax.Precision.HIGHEST,
                                           preferred_element_type=F32)
        col_ref[rs, GDN_HEADS:2 * GDN_HEADS] = beta[rs, :]
        row_ref[:, rs] = jnp.dot(g_row[:, rs], tri_upper, precision=lax.Precision.HIGHEST,
                                 preferred_element_type=F32)


def _gdn_chunks(qkv_ref, zb_ref, col_ref, row_ref, gnw_ref, state_ref, yb_ref):
    c = GDN_CHUNK
    n_chunks = qkv_ref.shape[0] // c
    ii = lax.broadcasted_iota(jnp.int32, (c, c), 0)
    jj = lax.broadcasted_iota(jnp.int32, (c, c), 1)
    causal = jj <= ii
    strict = jj < ii
    eye = (ii == jj).astype(F32)
    units = [(j, hd) for j in range(n_chunks) for hd in range(GDN_HEADS)]
    n_units = range(len(units))

    def rows(j):
        return slice(j * c, (j + 1) * c)

    dcols, betas, e_dec, e_rem, e_last = [], [], [], [], []
    for j in range(n_chunks):
        col = col_ref[rows(j), :]
        d = col[:, 0:GDN_HEADS]
        dlast = d[c - 1:c, :]
        dcols.append(d)
        betas.append(col[:, GDN_HEADS:2 * GDN_HEADS])
        e_dec.append(jnp.exp(d))
        e_rem.append(jnp.exp(dlast - d))
        e_last.append(jnp.exp(dlast))

    def colv(per_chunk, u):
        j, hd = units[u]
        return per_chunk[j][:, hd:hd + 1]

    q = [qkv_ref[rows(j), _head_cols(0, hd)] for j, hd in units]
    k = [qkv_ref[rows(j), _head_cols(GDN_WIDTH, hd)] for j, hd in units]
    kf = [k[u].astype(F32) for u in n_units]
    kb = [kf[u] * colv(betas, u) for u in n_units]
    gram = [_dot_nt(jnp.concatenate([_bf(kb[u]), q[u]], axis=0), k[u]) for u in n_units]
    lmat = [jnp.exp(jnp.where(causal, colv(dcols, u) - row_ref[hd:hd + 1, rows(j)], -jnp.inf))
            for u, (j, hd) in enumerate(units)]
    a = [jnp.where(strict, gram[u][0:c] * lmat[u], 0.0) for u in n_units]
    qk = [_bf(gram[u][c:2 * c] * lmat[u]) for u in n_units]
    a_b = [_bf(a[u]) for u in n_units]
    tb = [_bf(eye) - jnp.where((ii // 2) == (jj // 2), a_b[u], jnp.zeros_like(a_b[u])) for u in n_units]
    b = 2
    while b < c:
        off = _bf((((ii // (2 * b)) == (jj // (2 * b))) & ((ii // b) != (jj // b))).astype(F32))
        if b % BF16_ROWS == 0:
            lower = [slice(s0 + b, s0 + 2 * b) for s0 in range(0, c, 2 * b)]
            m = [_dot(jnp.concatenate([tb[u][rs] for rs in lower], axis=0), a_b[u] * off) for u in n_units]
            upd = [_bf(_dot(_bf(m[u]), tb[u])) for u in n_units]
            tb = [jnp.concatenate(
                [piece for n, rs in enumerate(lower)
                 for piece in (tb[u][rs.start - b:rs.start], tb[u][rs] - upd[u][n * b:(n + 1) * b])], axis=0)
                 for u in n_units]
        else:
            m = [_dot(tb[u], a_b[u] * off) for u in n_units]
            tb = [tb[u] - _bf(_dot(_bf(m[u]), tb[u])) for u in n_units]
        b *= 2
    r = [jnp.concatenate([qkv_ref[rows(j), _head_cols(2 * GDN_WIDTH, hd)].astype(F32) * colv(betas, u),
                          kb[u] * colv(e_dec, u)], axis=1) for u, (j, hd) in enumerate(units)]
    uw = [_dot(tb[u], _bf(r[u])) for u in n_units]
    w_b = [_bf(uw[u][:, HEAD_DIM:2 * HEAD_DIM]) for u in n_units]
    q_qk = [jnp.concatenate([_bf(q[u].astype(F32) * colv(e_dec, u)), qk[u]], axis=1) for u in n_units]
    kd = [_bf(kf[u] * colv(e_rem, u)) for u in n_units]
    s = [state_ref[hd] for hd in range(GDN_HEADS)]
    for j in range(n_chunks):
        us = [j * GDN_HEADS + hd for hd in range(GDN_HEADS)]
        s_b = [_bf(s[hd]) for hd in range(GDN_HEADS)]
        v_new = [_bf(uw[u][:, 0:HEAD_DIM] - _dot(w_b[u], s_b[hd])) for hd, u in enumerate(us)]
        o = [_dot(q_qk[u], jnp.concatenate([s_b[hd], v_new[hd]], axis=0)) for hd, u in enumerate(us)]
        s = [s[hd] * colv(e_last, u) + _dot_tn(kd[u], v_new[hd]) for hd, u in enumerate(us)]
        for hd in range(GDN_HEADS):
            on = o[hd] * lax.rsqrt(jnp.mean(o[hd] * o[hd], axis=-1, keepdims=True) + EPS) * gnw_ref[...]
            yb_ref[rows(j), _head_cols(0, hd)] = (
                on * zb_ref[rows(j), _head_cols(0, hd)].astype(F32)).astype(yb_ref.dtype)
    for hd in range(GDN_HEADS):
        state_ref[hd] = s[hd]


def _mix_kernel(tiles_per_seq,
                h_ref, nw_ref, w_ref, wab_ref, wabt_ref, lng_ref, lnb_ref, wsp_ref, bspt_ref,
                cw_ref, alog_c_ref, dtb_c_ref, alog_r_ref, dtb_r_ref, gnw_ref,
                ya_ref, yb_ref,
                cs_ref, state_ref, qkv_ref, zb_ref, col_ref, row_ref):
    @pl.when(pl.program_id(0) % tiles_per_seq == 0)
    def _():
        cs_ref[...] = jnp.zeros(cs_ref.shape, F32)
        state_ref[...] = jnp.zeros(state_ref.shape, F32)

    xn = _bf(_rms(h_ref[...], nw_ref[...]))
    _gmlp_branch(xn, w_ref, lng_ref, lnb_ref, wsp_ref, bspt_ref, ya_ref)
    _gdn_prepare(xn, w_ref, wab_ref, wabt_ref, cw_ref, alog_c_ref, dtb_c_ref, alog_r_ref, dtb_r_ref,
                 cs_ref, qkv_ref, zb_ref, col_ref, row_ref)
    for r0 in range(0, h_ref.shape[0], GDN_GROUP_ROWS):
        rs = slice(r0, r0 + GDN_GROUP_ROWS)
        _gdn_chunks(qkv_ref.at[rs], zb_ref.at[rs], col_ref.at[rs], row_ref.at[:, rs], gnw_ref, state_ref,
                    yb_ref.at[rs])


def _layer_spec(layer, shape):
    return pl.BlockSpec((None,) + shape, lambda i: (layer,) + (0,) * len(shape),
                        pipeline_mode=pl.Buffered(1))


def _mix(layer, h, nw, w_in_t, wab, wabt, lng, lnb, wsp, bspt, cw, alog_c, dtb_c, alog_r, dtb_r, gnw, seq):
    t = h.shape[0]
    tm = TILE_MIX
    spec = functools.partial(_layer_spec, layer)
    return pl.pallas_call(
        functools.partial(_mix_kernel, seq // tm),
        out_shape=(
            jax.ShapeDtypeStruct((t, GMLP_WIDTH), BF16),
            jax.ShapeDtypeStruct((t, GDN_WIDTH), BF16),
        ),
        grid=(t // tm,),
        in_specs=[
            pl.BlockSpec((tm, D_MODEL), lambda i: (i, 0)),
            spec((1, D_MODEL)),
            spec((MAIN_COLS, D_MODEL)),
            spec((D_MODEL, 2 * GDN_HEADS)),
            spec((GDN_HEADS, D_MODEL)),
            spec((1, GMLP_WIDTH)),
            spec((1, GMLP_WIDTH)),
            spec((GMLP_GROUPS, SPATIAL_BLOCK, SPATIAL_BLOCK)),
            spec((SPATIAL_BLOCK, GMLP_GROUPS)),
            spec((CONV_K, 3 * GDN_WIDTH)),
            spec((1, GDN_HEADS)),
            spec((1, GDN_HEADS)),
            spec((GDN_HEADS, 1)),
            spec((GDN_HEADS, 1)),
            spec((1, HEAD_DIM)),
        ],
        out_specs=(
            pl.BlockSpec((tm, GMLP_WIDTH), lambda i: (i, 0)),
            pl.BlockSpec((tm, GDN_WIDTH), lambda i: (i, 0)),
        ),
        scratch_shapes=[
            pltpu.VMEM((CONV_PAD, 3 * GDN_WIDTH), F32),
            pltpu.VMEM((GDN_HEADS, HEAD_DIM, HEAD_DIM), F32),
            pltpu.VMEM((tm, 3 * GDN_WIDTH), BF16),
            pltpu.VMEM((tm, GDN_WIDTH), BF16),
            pltpu.VMEM((tm, 2 * GDN_HEADS), F32),
            pltpu.VMEM((GDN_HEADS, tm), F32),
        ],
        compiler_params=pltpu.CompilerParams(
            dimension_semantics=("arbitrary",), vmem_limit_bytes=VMEM_LIMIT_BYTES),
        name="mix",
    )(h, nw, w_in_t, wab, wabt, lng, lnb, wsp, bspt, cw, alog_c, dtb_c, alog_r, dtb_r, gnw)


def _out_ple_kernel(apply_final_norm,
                    h_ref, ya_ref, yb_ref, p_ref, wout_ref, wple_ref, pnw_ref, gnw_ref, wg_ref, fnw_ref,
                    o_ref):
    h1 = (h_ref[...] + _dot(ya_ref[...], wout_ref[0:GMLP_WIDTH, :])
          + _dot(yb_ref[...], wout_ref[GMLP_WIDTH:GMLP_WIDTH + GDN_WIDTH, :]))
    e = _rms(_dot(_bf(p_ref[...]), wple_ref[...]), pnw_ref[...])
    gate = jax.nn.sigmoid(_dot(_bf(_rms(h1, gnw_ref[...])), wg_ref[...]))
    h2 = h1 + gate * e
    if apply_final_norm:
        h2 = _rms(h2, fnw_ref[...])
    o_ref[...] = h2


def _out_ple(layer, h, ya, yb, p, wout, wple, pnw, gnw, wg, fnw, apply_final_norm):
    t = h.shape[0]
    tm = TILE_OUT
    spec = functools.partial(_layer_spec, layer)
    tok = lambda width: pl.BlockSpec((tm, width), lambda i: (i, 0))
    return pl.pallas_call(
        functools.partial(_out_ple_kernel, apply_final_norm),
        out_shape=jax.ShapeDtypeStruct((t, D_MODEL), F32),
        grid=(t // tm,),
        in_specs=[
            tok(D_MODEL), tok(GMLP_WIDTH), tok(GDN_WIDTH),
            pl.BlockSpec((None, tm, PLE_DIM), lambda i: (layer, i, 0)),
            spec((GMLP_WIDTH + GDN_WIDTH, D_MODEL)),
            spec((PLE_DIM, D_MODEL)),
            spec((1, D_MODEL)),
            spec((1, D_MODEL)),
            spec((D_MODEL, D_MODEL)),
            pl.BlockSpec((1, D_MODEL), lambda i: (0, 0)),
        ],
        out_specs=tok(D_MODEL),
        compiler_params=pltpu.CompilerParams(
            dimension_semantics=("arbitrary",), vmem_limit_bytes=VMEM_LIMIT_BYTES),
        name="out_ple",
    )(h, ya, yb, p, wout, wple, pnw, gnw, wg, fnw)


def kernel(x, p, norm_w, w_in, ln_v_g, ln_v_b, w_spatial, b_spatial, conv_w, A_log, dt_bias,
           gdn_norm_w, w_out, w_ple, ple_norm_w, ple_gate_norm_w, w_ple_gate, final_norm_w):
    batch, seq, d = x.shape
    depth = w_in.shape[0]
    assert d == D_MODEL and seq % TILE_MIX == 0 and (batch * seq) % TILE_OUT == 0
    assert w_in.shape[2] == MAIN_COLS + 2 * GDN_HEADS
    t = batch * seq
    h = x.reshape(t, d)
    p2 = p.reshape(depth, t, PLE_DIM)
    w_in_t = _bf(jnp.swapaxes(w_in, 1, 2))
    wabt = w_in_t[:, MAIN_COLS:MAIN_COLS + GDN_HEADS, :]
    wab = jnp.swapaxes(w_in_t[:, MAIN_COLS:, :], 1, 2)
    wsp_b, wout_b, wple_b, wg_b = _bf(w_spatial), _bf(w_out), _bf(w_ple), _bf(w_ple_gate)
    bspt = jnp.swapaxes(b_spatial, 1, 2)
    row3 = lambda a: a[:, None, :]
    col3 = lambda a: a[:, :, None]
    for i in range(depth):
        ya, yb = _mix(i, h, row3(norm_w), w_in_t, wab, wabt, row3(ln_v_g), row3(ln_v_b), wsp_b, bspt,
                      conv_w, row3(A_log), row3(dt_bias), col3(A_log), col3(dt_bias), row3(gdn_norm_w), seq)
        h = _out_ple(i, h, ya, yb, p2, wout_b, wple_b, row3(ple_norm_w), row3(ple_gate_norm_w), wg_b,
                     final_norm_w[None], apply_final_norm=(i == depth - 1))
    return h.reshape(batch, seq, d)
```

```python
import functools

import jax
import jax.numpy as jnp
from jax import lax
from jax.experimental import pallas as pl
from jax.experimental.pallas import tpu as pltpu

D_MODEL = 1024
GMLP_WIDTH = 1024
GMLP_GROUPS = 8
GROUP_DIM = 128
SPATIAL_BLOCK = 128
SPATIAL_CHUNK = 64
GDN_WIDTH = 1024
GDN_HEADS = 8
HEAD_DIM = 128
CONV_K = 4
PLE_DIM = 256
EPS = 1e-6
MAIN_COLS = 3 * GMLP_WIDTH + 4 * GDN_WIDTH

GDN_CHUNK = 128
TILE_MIX = 512
TILE_OUT = 512
GDN_GROUP_ROWS = 256
STRIP_COLS = 256
BF16_ROWS = 16
CONV_PAD = 8
VMEM_LIMIT_BYTES = 56 * 1024 * 1024

F32 = jnp.float32
BF16 = jnp.bfloat16
LOG2E = 1.4426950408889634
GELU_K1 = -2.0 * 0.7978845608028654 * LOG2E
GELU_K3 = GELU_K1 * 0.044715


def _bf(x):
    return x.astype(BF16)


def _dot(a, b):
    return jnp.dot(a, b, preferred_element_type=F32)


def _dot_nt(a, b):
    return lax.dot_general(a, b, (((1,), (1,)), ((), ())), preferred_element_type=F32)


def _dot_tn(a, b):
    return lax.dot_general(a, b, (((0,), (0,)), ((), ())), preferred_element_type=F32)


def _rms(x, w):
    return x * lax.rsqrt(jnp.mean(x * x, axis=-1, keepdims=True) + EPS) * w


def _silu(x):
    return x / (1.0 + jnp.exp2(x * -LOG2E))


def _gelu_tanh(x):
    return x / (1.0 + jnp.exp2(x * (GELU_K1 + GELU_K3 * (x * x))))


def _softplus(x):
    return jnp.maximum(x, 0.0) + jnp.log1p(jnp.exp(-jnp.abs(x)))


def _head_cols(base, hd):
    return slice(base + hd * HEAD_DIM, base + (hd + 1) * HEAD_DIM)


def _gmlp_branch(xn, w_ref, lng_ref, lnb_ref, wsp_ref, bspt_ref, ya_ref):
    tm = xn.shape[0]
    strips = range(0, GMLP_WIDTH, STRIP_COLS)

    def proj(base, c0):
        return _dot_nt(xn, w_ref[base + c0:base + c0 + STRIP_COLS, :])

    v = jnp.concatenate([_gelu_tanh(proj(GMLP_WIDTH, c0)) for c0 in strips], axis=1)
    mu = jnp.mean(v, axis=-1, keepdims=True)
    vc = v - mu
    var = jnp.mean(vc * vc, axis=-1, keepdims=True)
    vln = _bf(vc * lax.rsqrt(var + EPS) * lng_ref[...] + lnb_ref[...])
    uz = jnp.concatenate([_gelu_tanh(proj(0, c0)) * _silu(proj(2 * GMLP_WIDTH, c0)) for c0 in strips], axis=1)
    ri = lax.broadcasted_iota(jnp.int32, (SPATIAL_BLOCK, SPATIAL_BLOCK), 0) // SPATIAL_CHUNK
    ci = lax.broadcasted_iota(jnp.int32, (SPATIAL_BLOCK, SPATIAL_BLOCK), 1) // SPATIAL_CHUNK
    chunk_causal = ci <= ri
    for g in range(GMLP_GROUPS):
        gs = slice(g * GROUP_DIM, (g + 1) * GROUP_DIM)
        wm = jnp.where(chunk_causal, wsp_ref[g], jnp.zeros_like(wsp_ref[g]))
        bias = bspt_ref[:, g:g + 1]
        for r in range(tm // SPATIAL_BLOCK):
            rs = slice(r * SPATIAL_BLOCK, (r + 1) * SPATIAL_BLOCK)
            mixed = _dot(wm, vln[rs, gs]) + bias
            ya_ref[rs, gs] = (uz[rs, gs] * mixed).astype(ya_ref.dtype)


def _gdn_prepare(xn, w_ref, wab_ref, wabt_ref, cw_ref, alog_c_ref, dtb_c_ref, alog_r_ref, dtb_r_ref,
                 cs_ref, qkv_ref, zb_ref, col_ref, row_ref):
    tm = xn.shape[0]
    sub = lax.broadcasted_iota(jnp.int32, (CONV_PAD, STRIP_COLS), 0)
    for c0 in range(0, 3 * GDN_WIDTH, STRIP_COLS):
        cols = slice(c0, c0 + STRIP_COLS)
        x = _dot_nt(xn, w_ref[3 * GMLP_WIDTH + c0:3 * GMLP_WIDTH + c0 + STRIP_COLS, :])
        prev = cs_ref[:, cols]
        acc = x * cw_ref[CONV_K - 1:CONV_K, cols]
        for shift in range(1, CONV_K):
            xs = pltpu.roll(x, shift, axis=0)
            head = jnp.where(sub < shift, pltpu.roll(prev, shift, axis=0), xs[0:CONV_PAD])
            shifted = jnp.concatenate([head, xs[CONV_PAD:]], axis=0)
            acc = acc + shifted * cw_ref[CONV_K - 1 - shift:CONV_K - shift, cols]
        cs_ref[:, cols] = x[tm - CONV_PAD:tm]
        act = _silu(acc)
        if c0 >= 2 * GDN_WIDTH:
            qkv_ref[:, cols] = _bf(act)
        else:
            scale = HEAD_DIM ** -0.5 if c0 < GDN_WIDTH else 1.0
            for hd in range(STRIP_COLS // HEAD_DIM):
                t = act[:, _head_cols(0, hd)]
                t = t * (lax.rsqrt(jnp.sum(t * t, axis=-1, keepdims=True) + EPS) * scale)
                qkv_ref[:, _head_cols(c0, hd)] = _bf(t)

    for c0 in range(0, GDN_WIDTH, STRIP_COLS):
        zb_ref[:, c0:c0 + STRIP_COLS] = _bf(_silu(_dot_nt(
            xn, w_ref[3 * GMLP_WIDTH + 3 * GDN_WIDTH + c0:3 * GMLP_WIDTH + 3 * GDN_WIDTH + c0 + STRIP_COLS, :])))

    ab = _dot(xn, wab_ref[...])
    g_col = -jnp.exp(alog_c_ref[...]) * _softplus(ab[:, 0:GDN_HEADS] + dtb_c_ref[...])
    beta = jax.nn.sigmoid(ab[:, GDN_HEADS:2 * GDN_HEADS])
    a_row = _dot_nt(wabt_ref[...], xn)
    g_row = -jnp.exp(alog_r_ref[...]) * _softplus(a_row + dtb_r_ref[...])
    ii = lax.broadcasted_iota(jnp.int32, (GDN_CHUNK, GDN_CHUNK), 0)
    jj = lax.broadcasted_iota(jnp.int32, (GDN_CHUNK, GDN_CHUNK), 1)
    tri_lower = (jj <= ii).astype(F32)
    tri_upper = (ii <= jj).astype(F32)
    for r in range(tm // GDN_CHUNK):
        rs = slice(r * GDN_CHUNK, (r + 1) * GDN_CHUNK)
        col_ref[rs, 0:GDN_HEADS] = jnp.dot(tri_lower, g_col[rs, :], precision=lax.Precision.HIGHEST,
                                           preferred_element_type=F32)
        col_ref[rs, GDN_HEADS:2 * GDN_HEADS] = beta[rs, :]
        row_ref[:, rs] = jnp.dot(g_row[:, rs], tri_upper, precision=lax.Precision.HIGHEST,
                                 preferred_element_type=F32)


def _gdn_chunks(qkv_ref, zb_ref, col_ref, row_ref, gnw_ref, state_ref, yb_ref):
    c = GDN_CHUNK
    n_chunks = qkv_ref.shape[0] // c
    ii = lax.broadcasted_iota(jnp.int32, (c, c), 0)
    jj = lax.broadcasted_iota(jnp.int32, (c, c), 1)
    causal = jj <= ii
    strict = jj < ii
    eye = (ii == jj).astype(F32)
    units = [(j, hd) for j in range(n_chunks) for hd in range(GDN_HEADS)]
    n_units = range(len(units))

    def rows(j):
        return slice(j * c, (j + 1) * c)

    dcols, betas, e_dec, e_rem, e_last = [], [], [], [], []
    for j in range(n_chunks):
        col = col_ref[rows(j), :]
        d = col[:, 0:GDN_HEADS]
        dlast = d[c - 1:c, :]
        dcols.append(d)
        betas.append(col[:, GDN_HEADS:2 * GDN_HEADS])
        e_dec.append(jnp.exp(d))
        e_rem.append(jnp.exp(dlast - d))
        e_last.append(jnp.exp(dlast))

    def colv(per_chunk, u):
        j, hd = units[u]
        return per_chunk[j][:, hd:hd + 1]

    q = [qkv_ref[rows(j), _head_cols(0, hd)] for j, hd in units]
    k = [qkv_ref[rows(j), _head_cols(GDN_WIDTH, hd)] for j, hd in units]
    kf = [k[u].astype(F32) for u in n_units]
    kb = [kf[u] * colv(betas, u) for u in n_units]
    gram = [_dot_nt(jnp.concatenate([_bf(kb[u]), q[u]], axis=0), k[u]) for u in n_units]
    lmat = [jnp.exp(jnp.where(causal, colv(dcols, u) - row_ref[hd:hd + 1, rows(j)], -jnp.inf))
            for u, (j, hd) in enumerate(units)]
    a = [jnp.where(strict, gram[u][0:c] * lmat[u], 0.0) for u in n_units]
    qk = [_bf(gram[u][c:2 * c] * lmat[u]) for u in n_units]
    a_b = [_bf(a[u]) for u in n_units]
    tb = [_bf(eye) - jnp.where((ii // 2) == (jj // 2), a_b[u], jnp.zeros_like(a_b[u])) for u in n_units]
    b = 2
    while b < c:
        off = _bf((((ii // (2 * b)) == (jj // (2 * b))) & ((ii // b) != (jj // b))).astype(F32))
        if b % BF16_ROWS == 0:
            lower = [slice(s0 + b, s0 + 2 * b) for s0 in range(0, c, 2 * b)]
            m = [_dot(jnp.concatenate([tb[u][rs] for rs in lower], axis=0), a_b[u] * off) for u in n_units]
            upd = [_bf(_dot(_bf(m[u]), tb[u])) for u in n_units]
            tb = [jnp.concatenate(
                [piece for n, rs in enumerate(lower)
                 for piece in (tb[u][rs.start - b:rs.start], tb[u][rs] - upd[u][n * b:(n + 1) * b])], axis=0)
                 for u in n_units]
        else:
            m = [_dot(tb[u], a_b[u] * off) for u in n_units]
            tb = [tb[u] - _bf(_dot(_bf(m[u]), tb[u])) for u in n_units]
        b *= 2
    r = [jnp.concatenate([qkv_ref[rows(j), _head_cols(2 * GDN_WIDTH, hd)].astype(F32) * colv(betas, u),
                          kb[u] * colv(e_dec, u)], axis=1) for u, (j, hd) in enumerate(units)]
    uw = [_dot(tb[u], _bf(r[u])) for u in n_units]
    w_b = [_bf(uw[u][:, HEAD_DIM:2 * HEAD_DIM]) for u in n_units]
    q_qk = [jnp.concatenate([_bf(q[u].astype(F32) * colv(e_dec, u)), qk[u]], axis=1) for u in n_units]
    kd = [_bf(kf[u] * colv(e_rem, u)) for u in n_units]
    s = [state_ref[hd] for hd in range(GDN_HEADS)]
    for j in range(n_chunks):
        us = [j * GDN_HEADS + hd for hd in range(GDN_HEADS)]
        s_b = [_bf(s[hd]) for hd in range(GDN_HEADS)]
        v_new = [_bf(uw[u][:, 0:HEAD_DIM] - _dot(w_b[u], s_b[hd])) for hd, u in enumerate(us)]
        o = [_dot(q_qk[u], jnp.concatenate([s_b[hd], v_new[hd]], axis=0)) for hd, u in enumerate(us)]
        s = [s[hd] * colv(e_last, u) + _dot_tn(kd[u], v_new[hd]) for hd, u in enumerate(us)]
        for hd in range(GDN_HEADS):
            on = o[hd] * lax.rsqrt(jnp.mean(o[hd] * o[hd], axis=-1, keepdims=True) + EPS) * gnw_ref[...]
            yb_ref[rows(j), _head_cols(0, hd)] = (
                on * zb_ref[rows(j), _head_cols(0, hd)].astype(F32)).astype(yb_ref.dtype)
    for hd in range(GDN_HEADS):
        state_ref[hd] = s[hd]


def _mix_kernel(tiles_per_seq, apply_final_norm,
                h_ref, nw_ref, w_ref, wab_ref, wabt_ref, lng_ref, lnb_ref, wsp_ref, bspt_ref,
                cw_ref, alog_c_ref, dtb_c_ref, alog_r_ref, dtb_r_ref, gnw_ref,
                p_ref, wout_ref, wple_ref, pnw_ref, pgnw_ref, wg_ref, fnw_ref,
                o_ref,
                cs_ref, state_ref, qkv_ref, zb_ref, col_ref, row_ref, ya_ref, yb_ref):
    @pl.when(pl.program_id(0) % tiles_per_seq == 0)
    def _():
        cs_ref[...] = jnp.zeros(cs_ref.shape, F32)
        state_ref[...] = jnp.zeros(state_ref.shape, F32)

    xn = _bf(_rms(h_ref[...], nw_ref[...]))
    _gmlp_branch(xn, w_ref, lng_ref, lnb_ref, wsp_ref, bspt_ref, ya_ref)
    _gdn_prepare(xn, w_ref, wab_ref, wabt_ref, cw_ref, alog_c_ref, dtb_c_ref, alog_r_ref, dtb_r_ref,
                 cs_ref, qkv_ref, zb_ref, col_ref, row_ref)
    for r0 in range(0, h_ref.shape[0], GDN_GROUP_ROWS):
        rs = slice(r0, r0 + GDN_GROUP_ROWS)
        _gdn_chunks(qkv_ref.at[rs], zb_ref.at[rs], col_ref.at[rs], row_ref.at[:, rs], gnw_ref, state_ref,
                    yb_ref.at[rs])
    _out_ple_kernel(apply_final_norm, h_ref, ya_ref, yb_ref, p_ref, wout_ref, wple_ref, pnw_ref, pgnw_ref,
                    wg_ref, fnw_ref, o_ref)


def _layer_spec(layer, shape):
    return pl.BlockSpec((None,) + shape, lambda i: (layer,) + (0,) * len(shape),
                        pipeline_mode=pl.Buffered(1))


def _mix(layer, h, nw, w_in_t, wab, wabt, lng, lnb, wsp, bspt, cw, alog_c, dtb_c, alog_r, dtb_r, gnw, seq,
         p, wout, wple, pnw, pgnw, wg, fnw, apply_final_norm):
    t = h.shape[0]
    tm = TILE_MIX
    spec = functools.partial(_layer_spec, layer)
    return pl.pallas_call(
        functools.partial(_mix_kernel, seq // tm, apply_final_norm),
        out_shape=jax.ShapeDtypeStruct((t, D_MODEL), F32),
        grid=(t // tm,),
        in_specs=[
            pl.BlockSpec((tm, D_MODEL), lambda i: (i, 0)),
            spec((1, D_MODEL)),
            spec((MAIN_COLS, D_MODEL)),
            spec((D_MODEL, 2 * GDN_HEADS)),
            spec((GDN_HEADS, D_MODEL)),
            spec((1, GMLP_WIDTH)),
            spec((1, GMLP_WIDTH)),
            spec((GMLP_GROUPS, SPATIAL_BLOCK, SPATIAL_BLOCK)),
            spec((SPATIAL_BLOCK, GMLP_GROUPS)),
            spec((CONV_K, 3 * GDN_WIDTH)),
            spec((1, GDN_HEADS)),
            spec((1, GDN_HEADS)),
            spec((GDN_HEADS, 1)),
            spec((GDN_HEADS, 1)),
            spec((1, HEAD_DIM)),
            pl.BlockSpec((None, tm, PLE_DIM), lambda i: (layer, i, 0)),
            spec((GMLP_WIDTH + GDN_WIDTH, D_MODEL)),
            spec((PLE_DIM, D_MODEL)),
            spec((1, D_MODEL)),
            spec((1, D_MODEL)),
            spec((D_MODEL, D_MODEL)),
            pl.BlockSpec((1, D_MODEL), lambda i: (0, 0)),
        ],
        out_specs=pl.BlockSpec((tm, D_MODEL), lambda i: (i, 0)),
        scratch_shapes=[
            pltpu.VMEM((CONV_PAD, 3 * GDN_WIDTH), F32),
            pltpu.VMEM((GDN_HEADS, HEAD_DIM, HEAD_DIM), F32),
            pltpu.VMEM((tm, 3 * GDN_WIDTH), BF16),
            pltpu.VMEM((tm, GDN_WIDTH), BF16),
            pltpu.VMEM((tm, 2 * GDN_HEADS), F32),
            pltpu.VMEM((GDN_HEADS, tm), F32),
            pltpu.VMEM((tm, GMLP_WIDTH), BF16),
            pltpu.VMEM((tm, GDN_WIDTH), BF16),
        ],
        compiler_params=pltpu.CompilerParams(
            dimension_semantics=("arbitrary",), vmem_limit_bytes=VMEM_LIMIT_BYTES),
        name="mix",
    )(h, nw, w_in_t, wab, wabt, lng, lnb, wsp, bspt, cw, alog_c, dtb_c, alog_r, dtb_r, gnw,
      p, wout, wple, pnw, pgnw, wg, fnw)


def _out_ple_kernel(apply_final_norm,
                    h_ref, ya_ref, yb_ref, p_ref, wout_ref, wple_ref, pnw_ref, gnw_ref, wg_ref, fnw_ref,
                    o_ref):
    h1 = (h_ref[...] + _dot(ya_ref[...], wout_ref[0:GMLP_WIDTH, :])
          + _dot(yb_ref[...], wout_ref[GMLP_WIDTH:GMLP_WIDTH + GDN_WIDTH, :]))
    e = _rms(_dot(_bf(p_ref[...]), wple_ref[...]), pnw_ref[...])
    gate = jax.nn.sigmoid(_dot(_bf(_rms(h1, gnw_ref[...])), wg_ref[...]))
    h2 = h1 + gate * e
    if apply_final_norm:
        h2 = _rms(h2, fnw_ref[...])
    o_ref[...] = h2


def _out_ple(layer, h, ya, yb, p, wout, wple, pnw, gnw, wg, fnw, apply_final_norm):
    t = h.shape[0]
    tm = TILE_OUT
    spec = functools.partial(_layer_spec, layer)
    tok = lambda width: pl.BlockSpec((tm, width), lambda i: (i, 0))
    return pl.pallas_call(
        functools.partial(_out_ple_kernel, apply_final_norm),
        out_shape=jax.ShapeDtypeStruct((t, D_MODEL), F32),
        grid=(t // tm,),
        in_specs=[
            tok(D_MODEL), tok(GMLP_WIDTH), tok(GDN_WIDTH),
            pl.BlockSpec((None, tm, PLE_DIM), lambda i: (layer, i, 0)),
            spec((GMLP_WIDTH + GDN_WIDTH, D_MODEL)),
            spec((PLE_DIM, D_MODEL)),
            spec((1, D_MODEL)),
            spec((1, D_MODEL)),
            spec((D_MODEL, D_MODEL)),
            pl.BlockSpec((1, D_MODEL), lambda i: (0, 0)),
        ],
        out_specs=tok(D_MODEL),
        compiler_params=pltpu.CompilerParams(
            dimension_semantics=("arbitrary",), vmem_limit_bytes=VMEM_LIMIT_BYTES),
        name="out_ple",
    )(h, ya, yb, p, wout, wple, pnw, gnw, wg, fnw)


def kernel(x, p, norm_w, w_in, ln_v_g, ln_v_b, w_spatial, b_spatial, conv_w, A_log, dt_bias,
           gdn_norm_w, w_out, w_ple, ple_norm_w, ple_gate_norm_w, w_ple_gate, final_norm_w):
    batch, seq, d = x.shape
    depth = w_in.shape[0]
    assert d == D_MODEL and seq % TILE_MIX == 0 and (batch * seq) % TILE_OUT == 0
    assert w_in.shape[2] == MAIN_COLS + 2 * GDN_HEADS
    t = batch * seq
    h = x.reshape(t, d)
    p2 = p.reshape(depth, t, PLE_DIM)
    w_in_t = _bf(jnp.swapaxes(w_in, 1, 2))
    wabt = w_in_t[:, MAIN_COLS:MAIN_COLS + GDN_HEADS, :]
    wab = jnp.swapaxes(w_in_t[:, MAIN_COLS:, :], 1, 2)
    wsp_b, wout_b, wple_b, wg_b = _bf(w_spatial), _bf(w_out), _bf(w_ple), _bf(w_ple_gate)
    bspt = jnp.swapaxes(b_spatial, 1, 2)
    row3 = lambda a: a[:, None, :]
    col3 = lambda a: a[:, :, None]
    for i in range(depth):
        h = _mix(i, h, row3(norm_w), w_in_t, wab, wabt, row3(ln_v_g), row3(ln_v_b), wsp_b, bspt,
                 conv_w, row3(A_log), row3(dt_bias), col3(A_log), col3(dt_bias), row3(gdn_norm_w), seq,
                 p2, wout_b, wple_b, row3(ple_norm_w), row3(ple_gate_norm_w), wg_b, final_norm_w[None],
                 apply_final_norm=(i == depth - 1))
    return h.reshape(batch, seq, d)
```
